```python
import math
import jax, jax.numpy as jnp
from jax import lax
import numpy as np

D_MODEL = 1024
BATCH = 2
SEQ = 8192
DEPTH = 4

N_MIXERS = 3
LN_EPS = 1e-5
NORM_EPS = 1e-6
CONV_K = 4
GDN_HEADS = 8
GDN_DK = 128
GDN_DV = 128
GDN_QK = GDN_HEADS * GDN_DK
GDN_V = GDN_HEADS * GDN_DV
GDN_CHUNK = 64
LRU_WIDTH = D_MODEL
LRU_BLOCKS = 16
LRU_BW = LRU_WIDTH // LRU_BLOCKS
RG_C = 8.0
NSA_HEADS = 16
NSA_GROUPS = 4
NSA_HPG = NSA_HEADS // NSA_GROUPS
NSA_DK = 64
NSA_Q = NSA_HEADS * NSA_DK
NSA_KV = NSA_GROUPS * NSA_DK
CMP_LEN = 32
CMP_STRIDE = 16
CMP_HID = 256
SLC_LEN = 64
N_SEL = 16
WINDOW = 512
Q_BLOCK = 128
N_GROUPS = 4
EXP_PER_GROUP = 8
N_EXPERTS = N_GROUPS * EXP_PER_GROUP
TOP_K = 2
D_EXPERT = 512
EXPERT_BLOCK = 256
NEG = -1e30
BIG = 1e30

kernel_name = 'hybrid_gdn_rglru_nsa_hmoe_trunk'


def layer_norm(x, g, b):
    xf = x.astype(jnp.float32)
    mu = jnp.mean(xf, -1, keepdims=True)
    xc = xf - mu
    var = jnp.mean(xc * xc, -1, keepdims=True)
    return (xc * lax.rsqrt(var + LN_EPS) * g + b).astype(x.dtype)


def causal_dwconv(x, w):
    k = w.shape[0]
    return lax.conv_general_dilated(x, w[:, None, :].astype(x.dtype), window_strides=(1,), padding=[(k - 1, 0)],
                                    dimension_numbers=('NWC', 'WIO', 'NWC'), feature_group_count=x.shape[-1])


def l2_normalize(x):
    xf = x.astype(jnp.float32)
    return xf * lax.rsqrt(jnp.sum(xf * xf, -1, keepdims=True) + NORM_EPS)


def masked_softmax(s, mask):
    s = jnp.where(mask, s.astype(jnp.float32), NEG)
    return jax.nn.softmax(s, axis=-1) * mask


def chunk_gated_delta_rule(q, k, v, g, beta):
    B, T, H, DK = q.shape
    DV = v.shape[-1]
    C = GDN_CHUNK
    N = T // C

    def chunks(t):
        return jnp.moveaxis(t.reshape((B, N, C, H) + t.shape[3:]), 3, 1)

    q = chunks(q) * (DK ** -0.5)
    k = chunks(k)
    v = chunks(v)
    gc = jnp.cumsum(chunks(g), axis=-1)
    bt = chunks(beta)[..., None]
    kb = k * bt
    vb = v * bt
    ii = jnp.arange(C)
    causal = ii[:, None] >= ii[None, :]
    strict = ii[:, None] > ii[None, :]
    decay = jnp.exp(jnp.where(causal, gc[..., :, None] - gc[..., None, :], -jnp.inf))
    lmat = jnp.einsum('bhnid,bhnjd->bhnij', kb, k) * decay * strict
    amat = lmat + jnp.eye(C, dtype=jnp.float32)
    u = lax.linalg.triangular_solve(amat, vb, left_side=True, lower=True, unit_diagonal=True)
    w = lax.linalg.triangular_solve(amat, kb * jnp.exp(gc)[..., None], left_side=True, lower=True, unit_diagonal=True)
    qk = jnp.einsum('bhnid,bhnjd->bhnij', q, k) * decay

    def step(S, xs):
        qn, kn, un, wn, gn, qkn = xs
        v_new = un - jnp.einsum('bhck,bhkv->bhcv', wn, S)
        o = (jnp.einsum('bhck,bhkv->bhcv', qn * jnp.exp(gn)[..., None], S)
             + jnp.einsum('bhij,bhjv->bhiv', qkn, v_new))
        g_last = gn[..., -1:]
        S = (S * jnp.exp(g_last)[..., None]
             + jnp.einsum('bhck,bhcv->bhkv', kn * jnp.exp(g_last - gn)[..., None], v_new))
        return S, o

    xs = tuple(jnp.moveaxis(t, 2, 0) for t in (q, k, u, w, gc, qk))
    S0 = jnp.zeros((B, H, DK, DV), jnp.float32)
    _, o = lax.scan(step, S0, xs)
    return jnp.moveaxis(o, 0, 2).transpose(0, 2, 3, 1, 4).reshape(B, T, H, DV)


def gdn_mixer(h, w_in, conv_w, a_log, dt_bias, norm_g, w_out):
    B, T, _ = h.shape
    p = h @ w_in
    s0 = 2 * GDN_QK + GDN_V
    qkv, z, a, b = jnp.split(p, [s0, s0 + GDN_V, s0 + GDN_V + GDN_HEADS], axis=-1)
    qkv = jax.nn.silu(causal_dwconv(qkv, conv_w))
    q, k, v = jnp.split(qkv, [GDN_QK, 2 * GDN_QK], axis=-1)
    q = l2_normalize(q.reshape(B, T, GDN_HEADS, GDN_DK))
    k = l2_normalize(k.reshape(B, T, GDN_HEADS, GDN_DK))
    v = v.reshape(B, T, GDN_HEADS, GDN_DV).astype(jnp.float32)
    beta = jax.nn.sigmoid(b.astype(jnp.float32))
    g = -jnp.exp(a_log.astype(jnp.float32)) * jax.nn.softplus(a.astype(jnp.float32) + dt_bias.astype(jnp.float32))
    o = chunk_gated_delta_rule(q, k, v, g, beta)
    o = o * lax.rsqrt(jnp.mean(o * o, -1, keepdims=True) + NORM_EPS) * norm_g
    o = o * jax.nn.silu(z.reshape(B, T, GDN_HEADS, GDN_DV).astype(jnp.float32))
    return o.reshape(B, T, GDN_V).astype(h.dtype) @ w_out


def _lin_rec_combine(left, right):
    a1, b1 = left
    a2, b2 = right
    return a1 * a2, a2 * b1 + b2


def rglru_mixer(h, w_in, conv_w, conv_b, w_a, b_a, w_x, b_x, lam, w_out):
    B, T, _ = h.shape
    gate_br, xb = jnp.split(h @ w_in, [LRU_WIDTH], axis=-1)
    xb = causal_dwconv(xb, conv_w) + conv_b
    xh = xb.reshape(B, T, LRU_BLOCKS, LRU_BW)
    r = jax.nn.sigmoid((jnp.einsum('btni,nij->btnj', xh, w_a).reshape(B, T, LRU_WIDTH) + b_a).astype(jnp.float32))
    i_g = jax.nn.sigmoid((jnp.einsum('btni,nij->btnj', xh, w_x).reshape(B, T, LRU_WIDTH) + b_x).astype(jnp.float32))
    log_a = -RG_C * r * jax.nn.softplus(-lam.astype(jnp.float32))
    a = jnp.exp(log_a)
    mult = jnp.sqrt(jnp.maximum(-jnp.expm1(2.0 * log_a), 0.0))
    u = mult * i_g * xb.astype(jnp.float32)
    _, hs = lax.associative_scan(_lin_rec_combine, (a, u), axis=1)
    y = hs * jax.nn.gelu(gate_br.astype(jnp.float32))
    return y.astype(h.dtype) @ w_out


def nsa_mixer(h, w_in, pe_k, pe_v, ck_w1, ck_w2, cv_w1, cv_w2, w_out):
    B, T, _ = h.shape
    G, P, DK = NSA_GROUPS, NSA_HPG, NSA_DK
    scale = DK ** -0.5
    cuts = [NSA_Q + j * NSA_KV for j in range(7)]
    q, kc, vc, ks, vs, kw, vw, gates = jnp.split(h @ w_in, cuts, axis=-1)
    q = q.reshape(B, T, G, P, DK)
    kc, vc, ks, vs, kw, vw = [t.reshape(B, T, G, DK) for t in (kc, vc, ks, vs, kw, vw)]
    gates = jax.nn.sigmoid(gates.astype(jnp.float32)).reshape(B, T, G, P, 3)

    n_cmp = (T - CMP_LEN) // CMP_STRIDE + 1
    cidx = jnp.arange(n_cmp)[:, None] * CMP_STRIDE + jnp.arange(CMP_LEN)[None, :]

    def compress(t, pe, w1, w2):
        blk = t[:, cidx] + pe[None, None, :, None, :]
        flat = blk.transpose(0, 1, 3, 2, 4).reshape(B, n_cmp, G, CMP_LEN * DK)
        return jax.nn.gelu(flat @ w1) @ w2

    kcmp = compress(kc, pe_k, ck_w1, ck_w2)
    vcmp = compress(vc, pe_v, cv_w1, cv_w2)
    cmp_end = jnp.arange(n_cmp) * CMP_STRIDE + CMP_LEN - 1

    n_slc = T // SLC_LEN
    n_sel = min(N_SEL, n_slc)
    ci = jnp.arange(n_cmp)[:, None] * CMP_STRIDE
    sj = jnp.arange(n_slc)[None, :] * SLC_LEN
    overlap = jnp.clip(jnp.minimum(ci + CMP_LEN, sj + SLC_LEN) - jnp.maximum(ci, sj), 0, None).astype(jnp.float32) / CMP_LEN
    ks_t = ks.reshape(B, n_slc, SLC_LEN, G, DK).transpose(0, 3, 1, 2, 4)
    vs_t = vs.reshape(B, n_slc, SLC_LEN, G, DK).transpose(0, 3, 1, 2, 4)
    b_ix = jnp.arange(B)[:, None, None, None]
    g_ix = jnp.arange(G)[None, :, None, None]
    s_idx = jnp.arange(n_slc)[None, :]

    kw_pad = jnp.pad(kw, ((0, 0), (WINDOW, 0), (0, 0), (0, 0)))
    vw_pad = jnp.pad(vw, ((0, 0), (WINDOW, 0), (0, 0), (0, 0)))

    n_qb = T // Q_BLOCK
    qb = jnp.moveaxis(q.reshape(B, n_qb, Q_BLOCK, G, P, DK), 1, 0)
    gb = jnp.moveaxis(gates.reshape(B, n_qb, Q_BLOCK, G, P, 3), 1, 0)

    def block(args):
        qi, gi, bidx = args
        t0 = bidx * Q_BLOCK
        tpos = t0 + jnp.arange(Q_BLOCK)
        s_c = jnp.einsum('bqgpd,bngd->bgpqn', qi, kcmp) * scale
        p_c = masked_softmax(s_c, cmp_end[None, :] <= tpos[:, None])
        o_c = jnp.einsum('bgpqn,bngd->bqgpd', p_c, vcmp)
        imp = jnp.einsum('bgpqn,ns->bgqs', p_c, overlap)
        cur = (tpos // SLC_LEN)[:, None]
        forced = (s_idx == 0) | (s_idx == cur) | (s_idx == cur - 1)
        imp = jnp.where(forced, BIG, imp)
        imp = jnp.where(sj <= tpos[:, None], imp, -BIG)
        _, sel = lax.top_k(imp, n_sel)
        kg = ks_t[b_ix, g_ix, sel]
        vg = vs_t[b_ix, g_ix, sel]
        kpos = (sel[..., None] * SLC_LEN + jnp.arange(SLC_LEN)).reshape(B, G, Q_BLOCK, n_sel * SLC_LEN)
        s_s = jnp.einsum('bqgpd,bgqnld->bgpqnl', qi, kg).reshape(B, G, P, Q_BLOCK, n_sel * SLC_LEN) * scale
        p_s = masked_softmax(s_s, (kpos <= tpos[None, None, :, None])[:, :, None])
        o_s = jnp.einsum('bgpqm,bgqmd->bqgpd', p_s, vg.reshape(B, G, Q_BLOCK, n_sel * SLC_LEN, DK))
        kwin = lax.dynamic_slice_in_dim(kw_pad, t0, Q_BLOCK + WINDOW, axis=1)
        vwin = lax.dynamic_slice_in_dim(vw_pad, t0, Q_BLOCK + WINDOW, axis=1)
        kpos_w = t0 - WINDOW + jnp.arange(Q_BLOCK + WINDOW)
        mask_w = ((kpos_w[None, :] <= tpos[:, None]) & (kpos_w[None, :] > tpos[:, None] - WINDOW)
                  & (kpos_w[None, :] >= 0))
        s_w = jnp.einsum('bqgpd,bkgd->bgpqk', qi, kwin) * scale
        p_w = masked_softmax(s_w, mask_w)
        o_w = jnp.einsum('bgpqk,bkgd->bqgpd', p_w, vwin)
        return gi[..., 0:1] * o_c + gi[..., 1:2] * o_s + gi[..., 2:3] * o_w

    ob = lax.map(block, (qb, gb, jnp.arange(n_qb)))
    o = jnp.moveaxis(ob, 0, 1).reshape(B, T, NSA_Q)
    return o.astype(h.dtype) @ w_out


def expert_dispatch(xf, expert_idx, gate, w_gate, w_up, w_down):
    N, D = xf.shape
    K = expert_idx.shape[1]
    E = w_gate.shape[0]
    A = N * K
    flat_e = expert_idx.reshape(-1).astype(jnp.int32)
    order = jnp.argsort(flat_e).astype(jnp.int32)
    sorted_e = flat_e[order]
    tok = order // K
    counts = jnp.bincount(flat_e, length=E)
    padded = (counts + EXPERT_BLOCK - 1) // EXPERT_BLOCK * EXPERT_BLOCK
    pad_end = jnp.cumsum(padded)
    pad_start = pad_end - padded
    start = jnp.cumsum(counts) - counts
    dest = pad_start[sorted_e] + jnp.arange(A, dtype=jnp.int32) - start[sorted_e]
    n_blk = -(-A // EXPERT_BLOCK) + E
    rows = n_blk * EXPERT_BLOCK
    src = jnp.full((rows,), N, jnp.int32).at[dest].set(tok)
    blk_e = jnp.minimum(jnp.searchsorted(pad_end, jnp.arange(n_blk) * EXPERT_BLOCK, side='right'), E - 1)
    xpad = jnp.concatenate([xf, jnp.zeros((1, D), xf.dtype)], axis=0)
    xb = xpad[src].reshape(n_blk, EXPERT_BLOCK, D)

    def ffn(args):
        xi, e = args
        return (jax.nn.silu(xi @ w_gate[e]) * (xi @ w_up[e])) @ w_down[e]

    yb = lax.map(ffn, (xb, blk_e)).reshape(rows, D)
    contrib = yb[dest] * gate.reshape(-1)[order][:, None].astype(yb.dtype)
    return jnp.zeros((N, D), yb.dtype).at[tok].add(contrib)


def hier_moe(h, w_grp, b_grp, w_exp, b_exp, w_gate, w_up, w_down):
    B, T, D = h.shape
    xf = h.reshape(B * T, D)
    n = xf.shape[0]
    grp_p = jax.nn.softmax((xf @ w_grp + b_grp).astype(jnp.float32), axis=-1)
    gp, gi = lax.top_k(grp_p, 1)
    exp_logits = (xf @ w_exp + b_exp).astype(jnp.float32).reshape(n, N_GROUPS, EXP_PER_GROUP)
    in_grp = exp_logits[jnp.arange(n), gi[:, 0]]
    wv, wi = lax.top_k(jax.nn.softmax(in_grp, axis=-1), TOP_K)
    wv = wv / jnp.sum(wv, -1, keepdims=True)
    gate = gp * wv
    expert_idx = gi * EXP_PER_GROUP + wi
    return expert_dispatch(xf, expert_idx, gate, w_gate, w_up, w_down).reshape(B, T, D)


def setup_inputs(seed: int = 0) -> dict:
    key = jax.random.key(seed)
    ks = iter(jax.random.split(key, 64))
    f32 = jnp.float32
    D = D_MODEL
    beta = (8.0 * DEPTH) ** -0.25
    n_a = len(range(0, DEPTH, N_MIXERS))
    n_b = len(range(1, DEPTH, N_MIXERS))
    n_c = len(range(2, DEPTH, N_MIXERS))

    def nrm(shape, s):
        return jax.random.normal(next(ks), shape, f32) * s

    def unif(shape, lo, hi):
        return jax.random.uniform(next(ks), shape, f32, lo, hi)

    dt = jnp.exp(unif((n_a, GDN_HEADS), math.log(1e-3), math.log(1e-1)))
    a0 = unif((n_b, LRU_WIDTH), 0.9, 0.999) ** (1.0 / RG_C)
    gdn_cols = 2 * GDN_QK + 2 * GDN_V + 2 * GDN_HEADS
    nsa_cols = NSA_Q + 6 * NSA_KV + 3 * NSA_HEADS
    return {
        'x': nrm((BATCH, SEQ, D), 1.0),
        'c': nrm((BATCH, D), 1.0),
        'ada_w': nrm((DEPTH, D, 6 * D), 0.1 * D ** -0.5),
        'ada_b': nrm((DEPTH, 6 * D), 0.01),
        'ln1_g': 1.0 + nrm((DEPTH, D), 0.02),
        'ln1_b': nrm((DEPTH, D), 0.02),
        'ln2_g': 1.0 + nrm((DEPTH, D), 0.02),
        'ln2_b': nrm((DEPTH, D), 0.02),
        'gdn_w_in': nrm((n_a, D, gdn_cols), D ** -0.5),
        'gdn_conv_w': nrm((n_a, CONV_K, 2 * GDN_QK + GDN_V), CONV_K ** -0.5),
        'gdn_a_log': jnp.log(unif((n_a, GDN_HEADS), 1.0, 16.0)),
        'gdn_dt_bias': dt + jnp.log(-jnp.expm1(-dt)),
        'gdn_norm_g': 1.0 + nrm((n_a, GDN_DV), 0.02),
        'gdn_w_out': nrm((n_a, GDN_V, D), GDN_V ** -0.5 * beta),
        'lru_w_in': nrm((n_b, D, 2 * LRU_WIDTH), D ** -0.5),
        'lru_conv_w': nrm((n_b, CONV_K, LRU_WIDTH), CONV_K ** -0.5),
        'lru_conv_b': nrm((n_b, LRU_WIDTH), 0.01),
        'lru_w_a': nrm((n_b, LRU_BLOCKS, LRU_BW, LRU_BW), LRU_BW ** -0.5),
        'lru_b_a': nrm((n_b, LRU_WIDTH), 0.01),
        'lru_w_x': nrm((n_b, LRU_BLOCKS, LRU_BW, LRU_BW), LRU_BW ** -0.5),
        'lru_b_x': nrm((n_b, LRU_WIDTH), 0.01),
        'lru_lambda': jnp.log(a0) - jnp.log1p(-a0),
        'lru_w_out': nrm((n_b, LRU_WIDTH, D), LRU_WIDTH ** -0.5 * beta),
        'nsa_w_in': nrm((n_c, D, nsa_cols), D ** -0.5),
        'nsa_pe_k': nrm((n_c, CMP_LEN, NSA_DK), 0.02),
        'nsa_pe_v': nrm((n_c, CMP_LEN, NSA_DK), 0.02),
        'nsa_ck_w1': nrm((n_c, CMP_LEN * NSA_DK, CMP_HID), (CMP_LEN * NSA_DK) ** -0.5),
        'nsa_ck_w2': nrm((n_c, CMP_HID, NSA_DK), CMP_HID ** -0.5),
        'nsa_cv_w1': nrm((n_c, CMP_LEN * NSA_DK, CMP_HID), (CMP_LEN * NSA_DK) ** -0.5),
        'nsa_cv_w2': nrm((n_c, CMP_HID, NSA_DK), CMP_HID ** -0.5),
        'nsa_w_out': nrm((n_c, NSA_Q, D), NSA_Q ** -0.5 * beta),
        'moe_w_grp': nrm((DEPTH, D, N_GROUPS), D ** -0.5),
        'moe_b_grp': nrm((DEPTH, N_GROUPS), 0.01),
        'moe_w_exp': nrm((DEPTH, D, N_EXPERTS), D ** -0.5),
        'moe_b_exp': nrm((DEPTH, N_EXPERTS), 0.01),
        'moe_w_gate': nrm((DEPTH, N_EXPERTS, D, D_EXPERT), D ** -0.5),
        'moe_w_up': nrm((DEPTH, N_EXPERTS, D, D_EXPERT), D ** -0.5),
        'moe_w_down': nrm((DEPTH, N_EXPERTS, D_EXPERT, D), D_EXPERT ** -0.5 * beta),
    }


def reference(x, c, ada_w, ada_b, ln1_g, ln1_b, ln2_g, ln2_b,
              gdn_w_in, gdn_conv_w, gdn_a_log, gdn_dt_bias, gdn_norm_g, gdn_w_out,
              lru_w_in, lru_conv_w, lru_conv_b, lru_w_a, lru_b_a, lru_w_x, lru_b_x, lru_lambda, lru_w_out,
              nsa_w_in, nsa_pe_k, nsa_pe_v, nsa_ck_w1, nsa_ck_w2, nsa_cv_w1, nsa_cv_w2, nsa_w_out,
              moe_w_grp, moe_b_grp, moe_w_exp, moe_b_exp, moe_w_gate, moe_w_up, moe_w_down):
    alpha = (2.0 * DEPTH) ** 0.25
    cond = jax.nn.silu(c)
    ja = jb = jc = 0
    for i in range(DEPTH):
        mod = (cond @ ada_w[i] + ada_b[i])[:, None, :]
        sh1, sc1, gt1, sh2, sc2, gt2 = jnp.split(mod, 6, axis=-1)
        hm = x * (1.0 + sc1) + sh1
        kind = i % N_MIXERS
        if kind == 0:
            y = gdn_mixer(hm, gdn_w_in[ja], gdn_conv_w[ja], gdn_a_log[ja], gdn_dt_bias[ja], gdn_norm_g[ja], gdn_w_out[ja])
            ja += 1
        elif kind == 1:
            y = rglru_mixer(hm, lru_w_in[jb], lru_conv_w[jb], lru_conv_b[jb], lru_w_a[jb], lru_b_a[jb],
                            lru_w_x[jb], lru_b_x[jb], lru_lambda[jb], lru_w_out[jb])
            jb += 1
        else:
            y = nsa_mixer(hm, nsa_w_in[jc], nsa_pe_k[jc], nsa_pe_v[jc], nsa_ck_w1[jc], nsa_ck_w2[jc],
                          nsa_cv_w1[jc], nsa_cv_w2[jc], nsa_w_out[jc])
            jc += 1
        x = layer_norm(alpha * x + (1.0 + gt1) * y, ln1_g[i], ln1_b[i])
        hf = x * (1.0 + sc2) + sh2
        y = hier_moe(hf, moe_w_grp[i], moe_b_grp[i], moe_w_exp[i], moe_b_exp[i],
                     moe_w_gate[i], moe_w_up[i], moe_w_down[i])
        x = layer_norm(alpha * x + (1.0 + gt2) * y, ln2_g[i], ln2_b[i])
    return x
```

```python
import functools

import jax
import jax.numpy as jnp
from jax import lax
from jax.experimental import pallas as pl
from jax.experimental.pallas import tpu as pltpu

F32 = jnp.float32
BF16 = jnp.bfloat16

D_MODEL = 1024
DEPTH = 4
N_MIXERS = 3
LN_EPS = 1e-5
NORM_EPS = 1e-6
CONV_K = 4
ALPHA = (2.0 * DEPTH) ** 0.25
GDN_HEADS = 8
GDN_DK = 128
GDN_CHUNK = 64
GDN_QKV = 3 * GDN_HEADS * GDN_DK
LRU_WIDTH = 1024
LRU_BLOCKS = 16
LRU_BW = LRU_WIDTH // LRU_BLOCKS
RG_C = 8.0
NSA_GROUPS = 4
NSA_HPG = 4
NSA_DK = 64
NSA_Q = NSA_GROUPS * NSA_HPG * NSA_DK
NSA_KV = NSA_GROUPS * NSA_DK
CMP_LEN = 32
CMP_STRIDE = 16
SLC_LEN = 64
N_SEL = 16
WINDOW = 512
Q_BLOCK = 128
SEL_KEY_CHUNK = 512
N_GROUPS = 4
EXP_PER_GROUP = 8
N_EXPERTS = N_GROUPS * EXP_PER_GROUP
TOP_K = 2
MOE_BLOCK = 256
NEG = -1e30
BIG = 1e30
LOWEST = -3e38

LANES = 128
SUBLANES = 8
VMEM_LIMIT_BYTES = 48 * 1024 * 1024


def _params(*sem):
    return pltpu.CompilerParams(dimension_semantics=sem, vmem_limit_bytes=VMEM_LIMIT_BYTES)


def _sigmoid(x):
    return 1.0 / (1.0 + jnp.exp(-x))


def _silu(x):
    return x * _sigmoid(x)


def _softplus(x):
    return jnp.maximum(x, 0.0) + jnp.log(1.0 + jnp.exp(-jnp.abs(x)))


def _gelu(x):
    return 0.5 * x * (1.0 + jnp.tanh(0.7978845608028654 * (x + 0.044715 * (x * x * x))))


def _dot(a, b):
    return jnp.dot(a, b, preferred_element_type=F32)


def _dot_nt(a, b):
    return lax.dot_general(a, b, (((1,), (1,)), ((), ())), preferred_element_type=F32)


def _dot_tn(a, b):
    return lax.dot_general(a, b, (((0,), (0,)), ((), ())), preferred_element_type=F32)


def _split3(x):
    hi = x.astype(BF16)
    r1 = x - hi.astype(F32)
    mid = r1.astype(BF16)
    lo = (r1 - mid.astype(F32)).astype(BF16)
    return hi, mid, lo


def _dot_lhs_exact(m_bf16, x):
    hi, mid, lo = _split3(x)
    return _dot(m_bf16, hi) + _dot(m_bf16, mid) + _dot(m_bf16, lo)


def _dot_rhs_exact(x, m_bf16):
    hi, mid, lo = _split3(x)
    return _dot(hi, m_bf16) + _dot(mid, m_bf16) + _dot(lo, m_bf16)


def _dot_f32(a, b):
    ah, am, al = _split3(a)
    bh, bm, bl = _split3(b)
    return (_dot(ah, bh) + _dot(ah, bm) + _dot(am, bh)) + (_dot(ah, bl) + _dot(al, bh) + _dot(am, bm))


def _layer_norm(z, g, b):
    mu = jnp.mean(z, axis=-1, keepdims=True)
    zc = z - mu
    var = jnp.mean(zc * zc, axis=-1, keepdims=True)
    return zc * lax.rsqrt(var + LN_EPS) * g + b


def _causal_conv(xbuf, cw_ref, col0, width, tb):
    cols = slice(col0, col0 + width)
    acc = cw_ref[0:1, cols] * xbuf[SUBLANES - 3:SUBLANES - 3 + tb, cols]
    for k in range(1, CONV_K):
        off = SUBLANES - 3 + k
        acc = acc + cw_ref[k:k + 1, cols] * xbuf[off:off + tb, cols]
    return acc


def _ada_kernel(c_ref, w_ref, b_ref, o_ref):
    cond = _silu(c_ref[...])
    o_ref[...] = _dot_f32(cond, w_ref[...]) + b_ref[...]


def ada_modulation(c, ada_w, ada_b, col_tile=1536):
    depth, d, n6 = ada_w.shape
    bsz = c.shape[0]
    return pl.pallas_call(
        _ada_kernel,
        grid=(depth, n6 // col_tile),
        in_specs=[
            pl.BlockSpec((bsz, d), lambda i, j: (0, 0)),
            pl.BlockSpec((None, d, col_tile), lambda i, j: (i, 0, j)),
            pl.BlockSpec((None, 1, col_tile), lambda i, j: (i, 0, j)),
        ],
        out_specs=pl.BlockSpec((None, bsz, col_tile), lambda i, j: (i, 0, j)),
        out_shape=jax.ShapeDtypeStruct((depth, bsz, n6), F32),
        compiler_params=_params("arbitrary", "arbitrary"),
        name="ada_modulation",
    )(c, ada_w, ada_b.reshape(depth, 1, n6))


def _modproj_kernel(x_ref, sc_ref, sh_ref, w_ref, o_ref, *, col_chunk):
    h = (x_ref[...] * (1.0 + sc_ref[...]) + sh_ref[...]).astype(BF16)
    ncols = o_ref.shape[-1]
    for c0 in range(0, ncols, col_chunk):
        c1 = min(c0 + col_chunk, ncols)
        o_ref[:, c0:c1] = _dot(h, w_ref[:, c0:c1])


def modproj(x, sc, sh, w, tm=256, col_chunk=1024):
    bsz, t, d = x.shape
    ncols = w.shape[1]
    return pl.pallas_call(
        functools.partial(_modproj_kernel, col_chunk=col_chunk),
        grid=(bsz, t // tm),
        in_specs=[
            pl.BlockSpec((None, tm, d), lambda b, i: (b, i, 0)),
            pl.BlockSpec((None, 1, d), lambda b, i: (b, 0, 0)),
            pl.BlockSpec((None, 1, d), lambda b, i: (b, 0, 0)),
            pl.BlockSpec((d, ncols), lambda b, i: (0, 0)),
        ],
        out_specs=pl.BlockSpec((None, tm, ncols), lambda b, i: (b, i, 0)),
        out_shape=jax.ShapeDtypeStruct((bsz, t, ncols), F32),
        compiler_params=_params("arbitrary", "arbitrary"),
        name="modproj",
    )(x, sc, sh, w)


def _gdn_kernel(qkv_ref, z_ref, ab_ref, cw_ref, alog_ref, dtb_ref, ng_ref, o_ref, xbuf, s_ref, *, tb):
    nh, dk, chunk = GDN_HEADS, GDN_DK, GDN_CHUNK

    @pl.when(pl.program_id(1) == 0)
    def _():
        xbuf[0:SUBLANES, :] = jnp.zeros((SUBLANES, GDN_QKV), F32)
        s_ref[...] = jnp.zeros_like(s_ref)

    xbuf[SUBLANES:SUBLANES + tb, :] = qkv_ref[...]

    ab = ab_ref[...]
    g_all = -jnp.exp(alog_ref[...]) * _softplus(ab + dtb_ref[...])
    beta_all = _sigmoid(ab)
    ri = lax.broadcasted_iota(jnp.int32, (tb, tb), 0)
    ci = lax.broadcasted_iota(jnp.int32, (tb, tb), 1)
    same = (ri // chunk) == (ci // chunk)
    causal = same & (ci <= ri)
    strict = same & (ci < ri)
    eye = jnp.where(ri == ci, 1.0, 0.0)
    tri = jnp.where(causal, 1.0, 0.0).astype(BF16)
    ones_blk = jnp.where(same, 1.0, 0.0).astype(BF16)
    gc_all = _dot_lhs_exact(tri, g_all)
    glast_all = _dot_lhs_exact(ones_blk, g_all)
    gc_t = gc_all.T
    egc_all = jnp.exp(gc_all)
    kdec_all = jnp.exp(glast_all - gc_all)

    for h in range(nh):
        q = _silu(_causal_conv(xbuf, cw_ref, h * dk, dk, tb))
        k = _silu(_causal_conv(xbuf, cw_ref, (nh + h) * dk, dk, tb))
        v = _silu(_causal_conv(xbuf, cw_ref, (2 * nh + h) * dk, dk, tb))
        q = q * lax.rsqrt(jnp.sum(q * q, axis=-1, keepdims=True) + NORM_EPS)
        k = k * lax.rsqrt(jnp.sum(k * k, axis=-1, keepdims=True) + NORM_EPS)
        gc = gc_all[:, h:h + 1]
        gl = glast_all[:, h:h + 1]
        beta = beta_all[:, nh + h:nh + h + 1]
        eg = egc_all[:, h:h + 1]
        decay = jnp.exp(jnp.where(causal, gc - gc_t[h:h + 1, :], NEG))

        kb = k * beta
        vb = v * beta
        k16 = k.astype(BF16)
        lmat = jnp.where(strict, _dot_nt(kb.astype(BF16), k16) * decay, 0.0)
        xpow = lmat
        tinv = eye - lmat
        for _ in range(5):
            x16 = xpow.astype(BF16)
            xpow = _dot(x16, x16)
            tinv = tinv + _dot(tinv.astype(BF16), xpow.astype(BF16))
        rhs = jnp.concatenate([vb, kb * eg], axis=1).astype(BF16)
        uw = _dot(tinv.astype(BF16), rhs)
        u = uw[:, :dk]
        w16 = uw[:, dk:].astype(BF16)
        qs = q * (dk ** -0.5)
        qk16 = (_dot_nt(qs.astype(BF16), k16) * decay).astype(BF16)
        qg16 = (qs * eg).astype(BF16)
        ks16 = (k * kdec_all[:, h:h + 1]).astype(BF16)

        state = s_ref[h]
        o_parts = []
        vn_parts = []
        for c in range(tb // chunk):
            rows = slice(c * chunk, (c + 1) * chunk)
            s16 = state.astype(BF16)
            vn16 = (u[rows] - _dot(w16[rows], s16)).astype(BF16)
            o_parts.append(_dot(qg16[rows], s16))
            state = state * jnp.exp(gl[c * chunk:c * chunk + 1, :]) + _dot_tn(ks16[rows], vn16)
            vn_parts.append(vn16)
        s_ref[h] = state
        o = jnp.concatenate(o_parts, axis=0) + _dot(qk16, jnp.concatenate(vn_parts, axis=0))
        o = o * lax.rsqrt(jnp.mean(o * o, axis=-1, keepdims=True) + NORM_EPS) * ng_ref[...]
        o_ref[:, h * dk:(h + 1) * dk] = o * _silu(z_ref[:, h * dk:(h + 1) * dk])

    xbuf[0:SUBLANES, :] = xbuf[tb:tb + SUBLANES, :]


def gdn_core(p, conv_w, a_log, dt_bias, norm_g, tb=256):
    bsz, t, _ = p.shape
    nh, dk = GDN_HEADS, GDN_DK
    v_w = nh * dk
    alog = jnp.zeros((1, LANES), F32).at[0, :nh].set(a_log)
    dtb = jnp.zeros((1, LANES), F32).at[0, :nh].set(dt_bias)
    ab_blk = (GDN_QKV + v_w) // LANES
    return pl.pallas_call(
        functools.partial(_gdn_kernel, tb=tb),
        grid=(bsz, t // tb),
        in_specs=[
            pl.BlockSpec((None, tb, GDN_QKV), lambda b, i: (b, i, 0)),
            pl.BlockSpec((None, tb, v_w), lambda b, i: (b, i, GDN_QKV // v_w)),
            pl.BlockSpec((None, tb, LANES), lambda b, i: (b, i, ab_blk)),
            pl.BlockSpec((CONV_K, GDN_QKV), lambda b, i: (0, 0)),
            pl.BlockSpec((1, LANES), lambda b, i: (0, 0)),
            pl.BlockSpec((1, LANES), lambda b, i: (0, 0)),
            pl.BlockSpec((1, dk), lambda b, i: (0, 0)),
        ],
        out_specs=pl.BlockSpec((None, tb, v_w), lambda b, i: (b, i, 0)),
        out_shape=jax.ShapeDtypeStruct((bsz, t, v_w), F32),
        scratch_shapes=[pltpu.VMEM((tb + SUBLANES, GDN_QKV), F32), pltpu.VMEM((nh, dk, dk), F32)],
        compiler_params=_params("arbitrary", "arbitrary"),
        name="gdn_core",
    )(p, p, p, conv_w, alog, dtb, norm_g.reshape(1, dk))


def _lru_kernel(gate_ref, xb_ref, cw_ref, cb_ref, wax_ref, ba_ref, bx_ref, lam_ref, o_ref, xbuf, h_ref, *, tb):
    width = LRU_WIDTH
    grp = 2 * LANES

    @pl.when(pl.program_id(1) == 0)
    def _():
        xbuf[0:SUBLANES, :] = jnp.zeros((SUBLANES, width), F32)
        h_ref[...] = jnp.zeros_like(h_ref)

    xbuf[SUBLANES:SUBLANES + tb, :] = xb_ref[...]
    row = lax.broadcasted_iota(jnp.int32, (tb, grp), 0)

    for j in range(width // grp):
        cols = slice(j * grp, (j + 1) * grp)
        xc = _causal_conv(xbuf, cw_ref, j * grp, grp, tb) + cb_ref[:, cols]
        res = _dot(xc.astype(BF16), wax_ref[j])
        r = _sigmoid(res[:, :grp] + ba_ref[:, cols])
        ig = _sigmoid(res[:, grp:] + bx_ref[:, cols])
        log_a = -RG_C * r * _softplus(-lam_ref[:, cols])
        a = jnp.exp(log_a)
        mult = jnp.sqrt(jnp.maximum(1.0 - jnp.exp(2.0 * log_a), 0.0))
        u = mult * ig * xc
        s = 1
        while s < tb:
            a_sh = jnp.where(row < s, 1.0, pltpu.roll(a, s, 0))
            u_sh = jnp.where(row < s, 0.0, pltpu.roll(u, s, 0))
            u = a * u_sh + u
            a = a * a_sh
            s *= 2
        hs = u + a * h_ref[:, cols]
        h_ref[:, cols] = hs[tb - 1:tb, :]
        o_ref[:, cols] = hs * _gelu(gate_ref[:, cols])

    xbuf[0:SUBLANES, :] = xbuf[tb:tb + SUBLANES, :]


def _block_diag_groups(w):
    per = (2 * LANES) // LRU_BW
    wg = w.reshape(LRU_BLOCKS // per, per, LRU_BW, LRU_BW)
    eye = jnp.eye(per, dtype=w.dtype)
    return jnp.einsum('gaij,ab->gaibj', wg, eye).reshape(LRU_BLOCKS // per, per * LRU_BW, per * LRU_BW)


def lru_core(p, conv_w, conv_b, w_a, b_a, w_x, b_x, lam, tb=256):
    bsz, t, _ = p.shape
    width = LRU_WIDTH
    grp = 2 * LANES
    wax = jnp.concatenate([_block_diag_groups(w_a), _block_diag_groups(w_x)], axis=-1).astype(BF16)
    vec = lambda a: a.reshape(1, width)
    return pl.pallas_call(
        functools.partial(_lru_kernel, tb=tb),
        grid=(bsz, t // tb),
        in_specs=[
            pl.BlockSpec((None, tb, width), lambda b, i: (b, i, 0)),
            pl.BlockSpec((None, tb, width), lambda b, i: (b, i, 1)),
            pl.BlockSpec((CONV_K, width), lambda b, i: (0, 0)),
            pl.BlockSpec((1, width), lambda b, i: (0, 0)),
            pl.BlockSpec((width // grp, grp, 2 * grp), lambda b, i: (0, 0, 0)),
            pl.BlockSpec((1, width), lambda b, i: (0, 0)),
            pl.BlockSpec((1, width), lambda b, i: (0, 0)),
            pl.BlockSpec((1, width), lambda b, i: (0, 0)),
        ],
        out_specs=pl.BlockSpec((None, tb, width), lambda b, i: (b, i, 0)),
        out_shape=jax.ShapeDtypeStruct((bsz, t, width), F32),
        scratch_shapes=[pltpu.VMEM((tb + SUBLANES, width), F32), pltpu.VMEM((1, width), F32)],
        compiler_params=_params("arbitrary", "arbitrary"),
        name="lru_core",
    )(p, p, conv_w, vec(conv_b), wax, vec(b_a), vec(b_x), vec(lam))


def _cmp_kernel(x_ref, pelo_ref, pehi_ref, w1a_ref, w1b_ref, w2_ref, o_ref):
    x = x_ref[...]
    nseg = x.shape[0]
    first = _dot((x + pelo_ref[...]).astype(BF16), w1a_ref[...])
    second = _dot((x + pehi_ref[...]).astype(BF16), w1b_ref[...])
    hid = _gelu(first + pltpu.roll(second, nseg - 1, 0))
    o_ref[...] = _dot(hid.astype(BF16), w2_ref[...])


def nsa_compress(xseg, pe, w1, w2):
    _, bsz, g, nseg, seg_w = xseg.shape
    hid = w1.shape[-1]
    pelo = pe[:, :CMP_STRIDE].reshape(2, 1, seg_w)
    pehi = pe[:, CMP_STRIDE:].reshape(2, 1, seg_w)
    w1a = w1[:, :seg_w].astype(BF16)
    w1b = w1[:, seg_w:].astype(BF16)
    return pl.pallas_call(
        _cmp_kernel,
        grid=(2, bsz, g),
        in_specs=[
            pl.BlockSpec((None, None, None, nseg, seg_w), lambda a, b, c: (a, b, c, 0, 0)),
            pl.BlockSpec((None, 1, seg_w), lambda a, b, c: (a, 0, 0)),
            pl.BlockSpec((None, 1, seg_w), lambda a, b, c: (a, 0, 0)),
            pl.BlockSpec((None, seg_w, hid), lambda a, b, c: (a, 0, 0)),
            pl.BlockSpec((None, seg_w, hid), lambda a, b, c: (a, 0, 0)),
            pl.BlockSpec((None, hid, NSA_DK), lambda a, b, c: (a, 0, 0)),
        ],
        out_specs=pl.BlockSpec((None, None, None, nseg, NSA_DK), lambda a, b, c: (a, b, c, 0, 0)),
        out_shape=jax.ShapeDtypeStruct((2, bsz, g, nseg, NSA_DK), F32),
        compiler_params=_params("arbitrary", "arbitrary", "arbitrary"),
        name="nsa_compress",
    )(xseg, pelo, pehi, w1a, w1b, w2.astype(BF16))


def _masked_softmax_rows(s, mask):
    s = jnp.where(mask, s, NEG)
    m = jnp.max(s, axis=-1, keepdims=True)
    e = jnp.where(mask, jnp.exp(s - m), 0.0)
    l = jnp.sum(e, axis=-1, keepdims=True)
    return e * (1.0 / jnp.where(l > 0.0, l, 1.0))


def _nsa_kernel(q_ref, gate_ref, kc_ref, vc_ref, ks_ref, vs_ref, kw_ref, vw_ref, o_ref, *, t_len):
    p_heads, dk, qb = NSA_HPG, NSA_DK, Q_BLOCK
    rows = p_heads * qb
    n_cmp_rows = kc_ref.shape[0]
    n_slc = t_len // SLC_LEN
    n_sel = min(N_SEL, n_slc)
    kc_len = SEL_KEY_CHUNK
    blk_per_chunk = kc_len // SLC_LEN
    t0 = pl.program_id(2) * qb

    qg = q_ref[...]
    q16 = (jnp.concatenate([qg[:, p * dk:(p + 1) * dk] for p in range(p_heads)], axis=0) * (dk ** -0.5)).astype(BF16)
    tpos = t0 + (lax.broadcasted_iota(jnp.int32, (rows, 1), 0) % qb)

    s_c = _dot_nt(q16, kc_ref[...].astype(BF16))
    n_idx = lax.broadcasted_iota(jnp.int32, (rows, n_cmp_rows), 1)
    p_c = _masked_softmax_rows(s_c, (n_idx * CMP_STRIDE + (CMP_LEN - 1)) <= tpos)
    o_c = _dot(p_c.astype(BF16), vc_ref[...].astype(BF16))

    p_sum = p_c[0:qb]
    for p in range(1, p_heads):
        p_sum = p_sum + p_c[p * qb:(p + 1) * qb]
    cn = lax.broadcasted_iota(jnp.int32, (n_cmp_rows, n_slc), 0) * CMP_STRIDE
    sj = lax.broadcasted_iota(jnp.int32, (n_cmp_rows, n_slc), 1) * SLC_LEN
    ov = jnp.maximum(jnp.minimum(cn + CMP_LEN, sj + SLC_LEN) - jnp.maximum(cn, sj), 0)
    overlap = (ov.astype(F32) * (1.0 / CMP_LEN)).astype(BF16)
    imp = _dot_rhs_exact(p_sum, overlap)
    s_idx = lax.broadcasted_iota(jnp.int32, (qb, n_slc), 1)
    tq = t0 + lax.broadcasted_iota(jnp.int32, (qb, n_slc), 0)
    cur = tq // SLC_LEN
    forced = (s_idx == 0) | (s_idx == cur) | (s_idx == cur - 1)
    imp = jnp.where(forced, BIG, imp)
    imp = jnp.where(s_idx * SLC_LEN <= tq, imp, -BIG)
    s_idx_f = s_idx.astype(F32)
    sel = jnp.zeros((qb, n_slc), F32)
    for _ in range(n_sel):
        m = jnp.max(imp, axis=-1, keepdims=True)
        first = jnp.min(jnp.where(imp == m, s_idx_f, float(n_slc)), axis=-1, keepdims=True)
        pick = s_idx_f == first
        sel = jnp.where(pick, 1.0, sel)
        imp = jnp.where(pick, LOWEST, imp)
    sel16 = sel.astype(BF16)

    blk_row = lax.broadcasted_iota(jnp.int32, (n_slc, kc_len), 0)
    blk_col = lax.broadcasted_iota(jnp.int32, (n_slc, kc_len), 1) // SLC_LEN
    key_off = lax.broadcasted_iota(jnp.int32, (rows, kc_len), 1)

    def sel_body(c, carry):
        m, l, acc = carry
        k0 = pl.multiple_of(c * kc_len, kc_len)
        s = _dot_nt(q16, ks_ref[pl.ds(k0, kc_len), :])
        expand = jnp.where(blk_row == c * blk_per_chunk + blk_col, 1.0, 0.0).astype(BF16)
        selx = _dot(sel16, expand)
        selx = jnp.concatenate([selx] * p_heads, axis=0)
        mask = (selx > 0.5) & ((k0 + key_off) <= tpos)
        s = jnp.where(mask, s, NEG)
        m_new = jnp.maximum(m, jnp.max(s, axis=-1, keepdims=True))
        scale = jnp.exp(m - m_new)
        e = jnp.where(mask, jnp.exp(s - m_new), 0.0)
        l = scale * l + jnp.sum(e, axis=-1, keepdims=True)
        acc = scale * acc + _dot(e.astype(BF16), vs_ref[pl.ds(k0, kc_len), :])
        return m_new, l, acc

    n_chunks = (t0 + qb + kc_len - 1) // kc_len
    init = (jnp.full((rows, 1), NEG, F32), jnp.zeros((rows, 1), F32), jnp.zeros((rows, dk), F32))
    _, l_s, acc_s = lax.fori_loop(0, n_chunks, sel_body, init)
    o_s = acc_s * (1.0 / jnp.where(l_s > 0.0, l_s, 1.0))

    w_len = qb + WINDOW
    w0 = pl.multiple_of(t0, qb)
    s_w = _dot_nt(q16, kw_ref[pl.ds(w0, w_len), :])
    kpos_w = t0 - WINDOW + lax.broadcasted_iota(jnp.int32, (rows, w_len), 1)
    mask_w = (kpos_w <= tpos) & (kpos_w > tpos - WINDOW) & (kpos_w >= 0)
    p_w = _masked_softmax_rows(s_w, mask_w)
    o_w = _dot(p_w.astype(BF16), vw_ref[pl.ds(w0, w_len), :])

    gts = _sigmoid(gate_ref[...])
    outs = []
    for p in range(p_heads):
        r = slice(p * qb, (p + 1) * qb)
        outs.append(gts[:, 3 * p:3 * p + 1] * o_c[r] + gts[:, 3 * p + 1:3 * p + 2] * o_s[r]
                    + gts[:, 3 * p + 2:3 * p + 3] * o_w[r])
    o_ref[...] = jnp.concatenate(outs, axis=1)


def nsa_attention(p, gates, kvcmp, kv16):
    bsz, t, _ = p.shape
    g, dk, qb = NSA_GROUPS, NSA_DK, Q_BLOCK
    nseg = kvcmp.shape[3]
    kwin = jnp.pad(kv16[2:4], ((0, 0), (0, 0), (0, 0), (WINDOW, 0), (0, 0)))
    qw = NSA_HPG * dk
    full = lambda n: pl.BlockSpec((None, None, n, dk), lambda b, c, j: (b, c, 0, 0))
    return pl.pallas_call(
        functools.partial(_nsa_kernel, t_len=t),
        grid=(bsz, g, t // qb),
        in_specs=[
            pl.BlockSpec((None, qb, qw), lambda b, c, j: (b, j, c)),
            pl.BlockSpec((None, None, qb, NSA_HPG * 3), lambda b, c, j: (b, c, j, 0)),
            full(nseg), full(nseg), full(t), full(t), full(t + WINDOW), full(t + WINDOW),
        ],
        out_specs=pl.BlockSpec((None, qb, qw), lambda b, c, j: (b, j, c)),
        out_shape=jax.ShapeDtypeStruct((bsz, t, g * qw), F32),
        compiler_params=_params("arbitrary", "arbitrary", "arbitrary"),
        name="nsa_attention",
    )(p, gates, kvcmp[0], kvcmp[1], kv16[0], kv16[1], kwin[0], kwin[1])


def nsa_mixer_core(x, sc, sh, w_in, pe_k, pe_v, ck_w1, ck_w2, cv_w1, cv_w2):
    bsz, t, _ = x.shape
    g, dk = NSA_GROUPS, NSA_DK
    p = modproj(x, sc, sh, _pad_cols(w_in).astype(BF16))
    kv = p[:, :, NSA_Q:NSA_Q + 6 * NSA_KV].reshape(bsz, t, 6, g, dk).transpose(2, 0, 3, 1, 4)
    xseg = kv[0:2].reshape(2, bsz, g, t // CMP_STRIDE, CMP_STRIDE * dk)
    kvcmp = nsa_compress(xseg, jnp.stack([pe_k, pe_v]), jnp.stack([ck_w1, cv_w1]), jnp.stack([ck_w2, cv_w2]))
    gates = p[:, :, NSA_Q + 6 * NSA_KV:NSA_Q + 6 * NSA_KV + 3 * g * NSA_HPG]
    gates = gates.reshape(bsz, t, g, 3 * NSA_HPG).transpose(0, 2, 1, 3)
    return nsa_attention(p, gates, kvcmp, kv[2:6].astype(BF16))


def _outproj_ln_kernel(o_ref, w_ref, x_ref, gt_ref, g_ref, b_ref, sc_ref, sh_ref, wr_ref, br_ref,
                       x1_ref, hf_ref, rt_ref):
    y = _dot(o_ref[...].astype(BF16), w_ref[...])
    x1 = _layer_norm(ALPHA * x_ref[...] + (1.0 + gt_ref[...]) * y, g_ref[...], b_ref[...])
    x1_ref[...] = x1
    hf = x1 * (1.0 + sc_ref[...]) + sh_ref[...]
    hf_ref[...] = hf.astype(BF16)

    logits = _dot_f32(hf, wr_ref[...]) + br_ref[...]
    lane = lax.broadcasted_iota(jnp.int32, logits.shape, 1).astype(F32)
    is_grp = lane < N_GROUPS
    gl = jnp.where(is_grp, logits, LOWEST)
    gmax = jnp.max(gl, axis=-1, keepdims=True)
    gi = jnp.min(jnp.where(gl == gmax, lane, float(LANES)), axis=-1, keepdims=True)
    gp = 1.0 / jnp.sum(jnp.where(is_grp, jnp.exp(logits - gmax), 0.0), axis=-1, keepdims=True)
    lo = N_GROUPS + EXP_PER_GROUP * gi
    el = jnp.where((lane >= lo) & (lane < lo + EXP_PER_GROUP), logits, LOWEST)
    m1 = jnp.max(el, axis=-1, keepdims=True)
    i1 = jnp.min(jnp.where(el == m1, lane, float(LANES)), axis=-1, keepdims=True)
    el2 = jnp.where(lane == i1, LOWEST, el)
    m2 = jnp.max(el2, axis=-1, keepdims=True)
    i2 = jnp.min(jnp.where(el2 == m2, lane, float(LANES)), axis=-1, keepdims=True)
    e2 = jnp.exp(m2 - m1)
    g1 = gp / (1.0 + e2)
    g2 = gp * e2 / (1.0 + e2)
    rt_ref[...] = jnp.where(lane == 0.0, i1 - N_GROUPS,
                            jnp.where(lane == 1.0, i2 - N_GROUPS,
                                      jnp.where(lane == 2.0, g1, jnp.where(lane == 3.0, g2, 0.0))))


def outproj_ln(o, w_out, x, gt, ln_g, ln_b, sc2, sh2, w_router, b_router, tm=256):
    bsz, t, d = x.shape
    din = o.shape[-1]
    row = lambda: pl.BlockSpec((None, tm, d), lambda b, i: (b, i, 0))
    per_b = lambda: pl.BlockSpec((None, 1, d), lambda b, i: (b, 0, 0))
    const = lambda r, c: pl.BlockSpec((r, c), lambda b, i: (0, 0))
    return pl.pallas_call(
        _outproj_ln_kernel,
        grid=(bsz, t // tm),
        in_specs=[
            pl.BlockSpec((None, tm, din), lambda b, i: (b, i, 0)),
            const(din, d), row(), per_b(), const(1, d), const(1, d), per_b(), per_b(),
            const(d, LANES), const(1, LANES),
        ],
        out_specs=[row(), row(), pl.BlockSpec((None, tm, LANES), lambda b, i: (b, i, 0))],
        out_shape=[jax.ShapeDtypeStruct((bsz, t, d), F32), jax.ShapeDtypeStruct((bsz, t, d), BF16),
                   jax.ShapeDtypeStruct((bsz, t, LANES), F32)],
        compiler_params=_params("arbitrary", "arbitrary"),
        name="outproj_ln",
    )(o, w_out, x, gt, ln_g.reshape(1, d), ln_b.reshape(1, d), sc2, sh2, w_router, b_router)


def _res_ln_kernel(x_ref, y_ref, gt_ref, g_ref, b_ref, o_ref):
    o_ref[...] = _layer_norm(ALPHA * x_ref[...] + (1.0 + gt_ref[...]) * y_ref[...], g_ref[...], b_ref[...])


def res_ln(x, y, gt, ln_g, ln_b, tm=512):
    bsz, t, d = x.shape
    row = lambda: pl.BlockSpec((None, tm, d), lambda b, i: (b, i, 0))
    return pl.pallas_call(
        _res_ln_kernel,
        grid=(bsz, t // tm),
        in_specs=[row(), row(), pl.BlockSpec((None, 1, d), lambda b, i: (b, 0, 0)),
                  pl.BlockSpec((1, d), lambda b, i: (0, 0)), pl.BlockSpec((1, d), lambda b, i: (0, 0))],
        out_specs=row(),
        out_shape=jax.ShapeDtypeStruct((bsz, t, d), F32),
        compiler_params=_params("arbitrary", "arbitrary"),
        name="res_ln",
    )(x, y, gt, ln_g.reshape(1, d), ln_b.reshape(1, d))


def _moe_kernel(blk_e_ref, nused_ref, x_ref, wg_ref, wu_ref, wd_ref, o_ref, wg16, wu16, wd16):
    i = pl.program_id(0)
    e = blk_e_ref[i]
    e_prev = blk_e_ref[jnp.maximum(i - 1, 0)]

    @pl.when((i == 0) | (e != e_prev))
    def _():
        wg16[...] = wg_ref[...].astype(BF16)
        wu16[...] = wu_ref[...].astype(BF16)
        wd16[...] = wd_ref[...].astype(BF16)

    @pl.when(i < nused_ref[0])
    def _():
        x = x_ref[...]
        gate = _dot(x, wg16[...])
        up = _dot(x, wu16[...])
        o_ref[...] = _dot((_silu(gate) * up).astype(BF16), wd16[...])

    @pl.when(i >= nused_ref[0])
    def _():
        o_ref[...] = jnp.zeros_like(o_ref)


def moe_ffn(xg, blk_e, n_used, w_gate, w_up, w_down):
    rows, d = xg.shape
    de = w_gate.shape[-1]
    n_blk = rows // MOE_BLOCK
    grid_spec = pltpu.PrefetchScalarGridSpec(
        num_scalar_prefetch=2,
        grid=(n_blk,),
        in_specs=[
            pl.BlockSpec((MOE_BLOCK, d), lambda i, be, nu: (i, 0)),
            pl.BlockSpec((None, d, de), lambda i, be, nu: (be[i], 0, 0)),
            pl.BlockSpec((None, d, de), lambda i, be, nu: (be[i], 0, 0)),
            pl.BlockSpec((None, de, d), lambda i, be, nu: (be[i], 0, 0)),
        ],
        out_specs=pl.BlockSpec((MOE_BLOCK, d), lambda i, be, nu: (i, 0)),
        scratch_shapes=[pltpu.VMEM((d, de), BF16), pltpu.VMEM((d, de), BF16), pltpu.VMEM((de, d), BF16)],
    )
    return pl.pallas_call(
        _moe_kernel,
        grid_spec=grid_spec,
        out_shape=jax.ShapeDtypeStruct((rows, d), F32),
        compiler_params=_params("arbitrary"),
        name="moe_ffn",
    )(blk_e, n_used, xg, w_gate, w_up, w_down)


def hier_moe_apply(hf16, route, w_gate, w_up, w_down):
    bsz, t, d = hf16.shape
    n = bsz * t
    n_exp = w_gate.shape[0]
    route = route.reshape(n, LANES)
    flat_e = route[:, :TOP_K].astype(jnp.int32).reshape(-1)
    gate = route[:, TOP_K:2 * TOP_K].reshape(-1)
    n_assign = n * TOP_K
    onehot = (flat_e[:, None] == jnp.arange(n_exp, dtype=jnp.int32)[None, :]).astype(jnp.int32)
    rank = jnp.sum((jnp.cumsum(onehot, axis=0) - onehot) * onehot, axis=1)
    counts = jnp.sum(onehot, axis=0)
    padded = (counts + MOE_BLOCK - 1) // MOE_BLOCK * MOE_BLOCK
    pad_end = jnp.cumsum(padded)
    pad_start = pad_end - padded
    dest = pad_start[flat_e] + rank
    n_blk = -(-n_assign // MOE_BLOCK) + n_exp
    rows = n_blk * MOE_BLOCK
    tok = jnp.arange(n_assign, dtype=jnp.int32) // TOP_K
    src = jnp.full((rows,), n, jnp.int32).at[dest].set(tok)
    blk_e = jnp.minimum(jnp.searchsorted(pad_end, jnp.arange(n_blk, dtype=jnp.int32) * MOE_BLOCK, side='right'),
                        n_exp - 1).astype(jnp.int32)
    n_used = (pad_end[-1:] // MOE_BLOCK).astype(jnp.int32)
    xpad = jnp.concatenate([hf16.reshape(n, d), jnp.zeros((1, d), hf16.dtype)], axis=0)
    yb = moe_ffn(xpad[src], blk_e, n_used, w_gate, w_up, w_down)
    contrib = yb[dest] * gate[:, None]
    return jnp.sum(contrib.reshape(n, TOP_K, d), axis=1).reshape(bsz, t, d)


def _pad_cols(w, mult=LANES):
    pad = (-w.shape[-1]) % mult
    return jnp.pad(w, ((0, 0), (0, pad))) if pad else w


def kernel(x, c, ada_w, ada_b, ln1_g, ln1_b, ln2_g, ln2_b, gdn_w_in, gdn_conv_w, gdn_a_log, gdn_dt_bias, gdn_norm_g, gdn_w_out, lru_w_in, lru_conv_w, lru_conv_b, lru_w_a, lru_b_a, lru_w_x, lru_b_x, lru_lambda, lru_w_out, nsa_w_in, nsa_pe_k, nsa_pe_v, nsa_ck_w1, nsa_ck_w2, nsa_cv_w1, nsa_cv_w2, nsa_w_out, moe_w_grp, moe_b_grp, moe_w_exp, moe_b_exp, moe_w_gate, moe_w_up, moe_w_down):
    bsz, t, d = x.shape
    depth = ada_w.shape[0]
    mod = ada_modulation(c, ada_w, ada_b).reshape(depth, bsz, 6, 1, d)
    ja = jb = jc = 0
    for i in range(depth):
        sh1, sc1, gt1, sh2, sc2, gt2 = (mod[i, :, k] for k in range(6))
        kind = i % N_MIXERS
        if kind == 0:
            p = modproj(x, sc1, sh1, _pad_cols(gdn_w_in[ja]).astype(BF16))
            o = gdn_core(p, gdn_conv_w[ja], gdn_a_log[ja], gdn_dt_bias[ja], gdn_norm_g[ja])
            w_out = gdn_w_out[ja]
            ja += 1
        elif kind == 1:
            p = modproj(x, sc1, sh1, lru_w_in[jb].astype(BF16))
            o = lru_core(p, lru_conv_w[jb], lru_conv_b[jb], lru_w_a[jb], lru_b_a[jb], lru_w_x[jb], lru_b_x[jb],
                         lru_lambda[jb])
            w_out = lru_w_out[jb]
            jb += 1
        else:
            o = nsa_mixer_core(x, sc1, sh1, nsa_w_in[jc], nsa_pe_k[jc], nsa_pe_v[jc], nsa_ck_w1[jc], nsa_ck_w2[jc],
                               nsa_cv_w1[jc], nsa_cv_w2[jc])
            w_out = nsa_w_out[jc]
            jc += 1
        w_router = _pad_cols(jnp.concatenate([moe_w_grp[i], moe_w_exp[i]], axis=1))
        b_router = _pad_cols(jnp.concatenate([moe_b_grp[i], moe_b_exp[i]])[None, :])
        x1, hf16, route = outproj_ln(o, w_out.astype(BF16), x, gt1, ln1_g[i], ln1_b[i], sc2, sh2, w_router, b_router)
        y = hier_moe_apply(hf16, route, moe_w_gate[i], moe_w_up[i], moe_w_down[i])
        x = res_ln(x1, y, gt2, ln2_g[i], ln2_b[i])
    return x
```

```python
import functools

import jax
import jax.numpy as jnp
from jax import lax
from jax.experimental import pallas as pl
from jax.experimental.pallas import tpu as pltpu

F32 = jnp.float32
BF16 = jnp.bfloat16

D_MODEL = 1024
DEPTH = 4
N_MIXERS = 3
LN_EPS = 1e-5
NORM_EPS = 1e-6
CONV_K = 4
ALPHA = (2.0 * DEPTH) ** 0.25
GDN_HEADS = 8
GDN_DK = 128
GDN_CHUNK = 64
GDN_QKV = 3 * GDN_HEADS * GDN_DK
LRU_WIDTH = 1024
LRU_BLOCKS = 16
LRU_BW = LRU_WIDTH // LRU_BLOCKS
RG_C = 8.0
NSA_GROUPS = 4
NSA_HPG = 4
NSA_DK = 64
NSA_Q = NSA_GROUPS * NSA_HPG * NSA_DK
NSA_KV = NSA_GROUPS * NSA_DK
CMP_LEN = 32
CMP_STRIDE = 16
SLC_LEN = 64
N_SEL = 16
WINDOW = 512
Q_BLOCK = 128
SEL_KEY_CHUNK = 512
N_GROUPS = 4
EXP_PER_GROUP = 8
N_EXPERTS = N_GROUPS * EXP_PER_GROUP
TOP_K = 2
MOE_BLOCK = 256
NEG = -1e30
BIG = 1e30
LOWEST = -3e38

LANES = 128
SUBLANES = 8
VMEM_LIMIT_BYTES = 48 * 1024 * 1024


def _params(*sem):
    return pltpu.CompilerParams(dimension_semantics=sem, vmem_limit_bytes=VMEM_LIMIT_BYTES)


def _sigmoid(x):
    return 1.0 / (1.0 + jnp.exp(-x))


def _silu(x):
    return x * _sigmoid(x)


def _softplus(x):
    return jnp.maximum(x, 0.0) + jnp.log(1.0 + jnp.exp(-jnp.abs(x)))


def _gelu(x):
    return 0.5 * x * (1.0 + jnp.tanh(0.7978845608028654 * (x + 0.044715 * (x * x * x))))


def _dot(a, b):
    return jnp.dot(a, b, preferred_element_type=F32)


def _dot_nt(a, b):
    return lax.dot_general(a, b, (((1,), (1,)), ((), ())), preferred_element_type=F32)


def _dot_tn(a, b):
    return lax.dot_general(a, b, (((0,), (0,)), ((), ())), preferred_element_type=F32)


def _split3(x):
    hi = x.astype(BF16)
    r1 = x - hi.astype(F32)
    mid = r1.astype(BF16)
    lo = (r1 - mid.astype(F32)).astype(BF16)
    return hi, mid, lo


def _dot_lhs_exact(m_bf16, x):
    hi, mid, lo = _split3(x)
    return _dot(m_bf16, hi) + _dot(m_bf16, mid) + _dot(m_bf16, lo)


def _dot_rhs_exact(x, m_bf16):
    hi, mid, lo = _split3(x)
    return _dot(hi, m_bf16) + _dot(mid, m_bf16) + _dot(lo, m_bf16)


def _dot_f32(a, b):
    ah, am, al = _split3(a)
    bh, bm, bl = _split3(b)
    return (_dot(ah, bh) + _dot(ah, bm) + _dot(am, bh)) + (_dot(ah, bl) + _dot(al, bh) + _dot(am, bm))


def _layer_norm(z, g, b):
    mu = jnp.mean(z, axis=-1, keepdims=True)
    zc = z - mu
    var = jnp.mean(zc * zc, axis=-1, keepdims=True)
    return zc * lax.rsqrt(var + LN_EPS) * g + b


def _causal_conv(xbuf, cw_ref, col0, width, tb):
    cols = slice(col0, col0 + width)
    acc = cw_ref[0:1, cols] * xbuf[SUBLANES - 3:SUBLANES - 3 + tb, cols]
    for k in range(1, CONV_K):
        off = SUBLANES - 3 + k
        acc = acc + cw_ref[k:k + 1, cols] * xbuf[off:off + tb, cols]
    return acc


def _ada_kernel(c_ref, w_ref, b_ref, o_ref):
    cond = _silu(c_ref[...])
    o_ref[...] = _dot_f32(cond, w_ref[...]) + b_ref[...]


def ada_modulation(c, ada_w, ada_b, col_tile=1536):
    depth, d, n6 = ada_w.shape
    bsz = c.shape[0]
    return pl.pallas_call(
        _ada_kernel,
        grid=(depth, n6 // col_tile),
        in_specs=[
            pl.BlockSpec((bsz, d), lambda i, j: (0, 0)),
            pl.BlockSpec((None, d, col_tile), lambda i, j: (i, 0, j)),
            pl.BlockSpec((None, 1, col_tile), lambda i, j: (i, 0, j)),
        ],
        out_specs=pl.BlockSpec((None, bsz, col_tile), lambda i, j: (i, 0, j)),
        out_shape=jax.ShapeDtypeStruct((depth, bsz, n6), F32),
        compiler_params=_params("arbitrary", "arbitrary"),
        name="ada_modulation",
    )(c, ada_w, ada_b.reshape(depth, 1, n6))


def _modproj_kernel(x_ref, sc_ref, sh_ref, w_ref, o_ref, *, col_chunk):
    h = (x_ref[...] * (1.0 + sc_ref[...]) + sh_ref[...]).astype(BF16)
    ncols = o_ref.shape[-1]
    for c0 in range(0, ncols, col_chunk):
        c1 = min(c0 + col_chunk, ncols)
        o_ref[:, c0:c1] = _dot(h, w_ref[:, c0:c1])


def modproj(x, sc, sh, w, tm=256, col_chunk=1024):
    bsz, t, d = x.shape
    ncols = w.shape[1]
    return pl.pallas_call(
        functools.partial(_modproj_kernel, col_chunk=col_chunk),
        grid=(bsz, t // tm),
        in_specs=[
            pl.BlockSpec((None, tm, d), lambda b, i: (b, i, 0)),
            pl.BlockSpec((None, 1, d), lambda b, i: (b, 0, 0)),
            pl.BlockSpec((None, 1, d), lambda b, i: (b, 0, 0)),
            pl.BlockSpec((d, ncols), lambda b, i: (0, 0)),
        ],
        out_specs=pl.BlockSpec((None, tm, ncols), lambda b, i: (b, i, 0)),
        out_shape=jax.ShapeDtypeStruct((bsz, t, ncols), F32),
        compiler_params=_params("arbitrary", "arbitrary"),
        name="modproj",
    )(x, sc, sh, w)


def _gdn_kernel(qkv_ref, z_ref, ab_ref, cw_ref, alog_ref, dtb_ref, ng_ref, o_ref, xbuf, s_ref, *, tb):
    nh, dk, chunk = GDN_HEADS, GDN_DK, GDN_CHUNK

    @pl.when(pl.program_id(1) == 0)
    def _():
        xbuf[0:SUBLANES, :] = jnp.zeros((SUBLANES, GDN_QKV), F32)
        s_ref[...] = jnp.zeros_like(s_ref)

    xbuf[SUBLANES:SUBLANES + tb, :] = qkv_ref[...]

    ab = ab_ref[...]
    g_all = -jnp.exp(alog_ref[...]) * _softplus(ab + dtb_ref[...])
    beta_all = _sigmoid(ab)
    ri = lax.broadcasted_iota(jnp.int32, (tb, tb), 0)
    ci = lax.broadcasted_iota(jnp.int32, (tb, tb), 1)
    same = (ri // chunk) == (ci // chunk)
    causal = same & (ci <= ri)
    strict = same & (ci < ri)
    eye = jnp.where(ri == ci, 1.0, 0.0)
    tri = jnp.where(causal, 1.0, 0.0).astype(BF16)
    ones_blk = jnp.where(same, 1.0, 0.0).astype(BF16)
    gc_all = _dot_lhs_exact(tri, g_all)
    glast_all = _dot_lhs_exact(ones_blk, g_all)
    gc_t = gc_all.T
    egc_all = jnp.exp(gc_all)
    kdec_all = jnp.exp(glast_all - gc_all)

    for h in range(nh):
        q = _silu(_causal_conv(xbuf, cw_ref, h * dk, dk, tb))
        k = _silu(_causal_conv(xbuf, cw_ref, (nh + h) * dk, dk, tb))
        v = _silu(_causal_conv(xbuf, cw_ref, (2 * nh + h) * dk, dk, tb))
        q = q * lax.rsqrt(jnp.sum(q * q, axis=-1, keepdims=True) + NORM_EPS)
        k = k * lax.rsqrt(jnp.sum(k * k, axis=-1, keepdims=True) + NORM_EPS)
        gc = gc_all[:, h:h + 1]
        gl = glast_all[:, h:h + 1]
        beta = beta_all[:, nh + h:nh + h + 1]
        eg = egc_all[:, h:h + 1]
        decay = jnp.exp(jnp.where(causal, gc - gc_t[h:h + 1, :], NEG))

        kb = k * beta
        vb = v * beta
        k16 = k.astype(BF16)
        lmat = jnp.where(strict, _dot_nt(kb.astype(BF16), k16) * decay, 0.0)
        xpow = lmat
        tinv = eye - lmat
        for _ in range(5):
            x16 = xpow.astype(BF16)
            xpow = _dot(x16, x16)
            tinv = tinv + _dot(tinv.astype(BF16), xpow.astype(BF16))
        rhs = jnp.concatenate([vb, kb * eg], axis=1).astype(BF16)
        uw = _dot(tinv.astype(BF16), rhs)
        u = uw[:, :dk]
        w16 = uw[:, dk:].astype(BF16)
        qs = q * (dk ** -0.5)
        qk16 = (_dot_nt(qs.astype(BF16), k16) * decay).astype(BF16)
        qg16 = (qs * eg).astype(BF16)
        ks16 = (k * kdec_all[:, h:h + 1]).astype(BF16)

        state = s_ref[h]
        o_parts = []
        vn_parts = []
        for c in range(tb // chunk):
            rows = slice(c * chunk, (c + 1) * chunk)
            s16 = state.astype(BF16)
            vn16 = (u[rows] - _dot(w16[rows], s16)).astype(BF16)
            o_parts.append(_dot(qg16[rows], s16))
            state = state * jnp.exp(gl[c * chunk:c * chunk + 1, :]) + _dot_tn(ks16[rows], vn16)
            vn_parts.append(vn16)
        s_ref[h] = state
        o = jnp.concatenate(o_parts, axis=0) + _dot(qk16, jnp.concatenate(vn_parts, axis=0))
        o = o * lax.rsqrt(jnp.mean(o * o, axis=-1, keepdims=True) + NORM_EPS) * ng_ref[...]
        o_ref[:, h * dk:(h + 1) * dk] = o * _silu(z_ref[:, h * dk:(h + 1) * dk])

    xbuf[0:SUBLANES, :] = xbuf[tb:tb + SUBLANES, :]


def gdn_core(p, conv_w, a_log, dt_bias, norm_g, tb=256):
    bsz, t, _ = p.shape
    nh, dk = GDN_HEADS, GDN_DK
    v_w = nh * dk
    alog = jnp.zeros((1, LANES), F32).at[0, :nh].set(a_log)
    dtb = jnp.zeros((1, LANES), F32).at[0, :nh].set(dt_bias)
    ab_blk = (GDN_QKV + v_w) // LANES
    return pl.pallas_call(
        functools.partial(_gdn_kernel, tb=tb),
        grid=(bsz, t // tb),
        in_specs=[
            pl.BlockSpec((None, tb, GDN_QKV), lambda b, i: (b, i, 0)),
            pl.BlockSpec((None, tb, v_w), lambda b, i: (b, i, GDN_QKV // v_w)),
            pl.BlockSpec((None, tb, LANES), lambda b, i: (b, i, ab_blk)),
            pl.BlockSpec((CONV_K, GDN_QKV), lambda b, i: (0, 0)),
            pl.BlockSpec((1, LANES), lambda b, i: (0, 0)),
            pl.BlockSpec((1, LANES), lambda b, i: (0, 0)),
            pl.BlockSpec((1, dk), lambda b, i: (0, 0)),
        ],
        out_specs=pl.BlockSpec((None, tb, v_w), lambda b, i: (b, i, 0)),
        out_shape=jax.ShapeDtypeStruct((bsz, t, v_w), F32),
        scratch_shapes=[pltpu.VMEM((tb + SUBLANES, GDN_QKV), F32), pltpu.VMEM((nh, dk, dk), F32)],
        compiler_params=_params("arbitrary", "arbitrary"),
        name="gdn_core",
    )(p, p, p, conv_w, alog, dtb, norm_g.reshape(1, dk))


def _lru_kernel(gate_ref, xb_ref, cw_ref, cb_ref, wax_ref, ba_ref, bx_ref, lam_ref, o_ref, xbuf, h_ref, *, tb):
    width = LRU_WIDTH
    grp = 2 * LANES

    @pl.when(pl.program_id(1) == 0)
    def _():
        xbuf[0:SUBLANES, :] = jnp.zeros((SUBLANES, width), F32)
        h_ref[...] = jnp.zeros_like(h_ref)

    xbuf[SUBLANES:SUBLANES + tb, :] = xb_ref[...]
    row = lax.broadcasted_iota(jnp.int32, (tb, grp), 0)

    for j in range(width // grp):
        cols = slice(j * grp, (j + 1) * grp)
        xc = _causal_conv(xbuf, cw_ref, j * grp, grp, tb) + cb_ref[:, cols]
        res = _dot(xc.astype(BF16), wax_ref[j])
        r = _sigmoid(res[:, :grp] + ba_ref[:, cols])
        ig = _sigmoid(res[:, grp:] + bx_ref[:, cols])
        log_a = -RG_C * r * _softplus(-lam_ref[:, cols])
        a = jnp.exp(log_a)
        mult = jnp.sqrt(jnp.maximum(1.0 - jnp.exp(2.0 * log_a), 0.0))
        u = mult * ig * xc
        s = 1
        while s < tb:
            a_sh = jnp.where(row < s, 1.0, pltpu.roll(a, s, 0))
            u_sh = jnp.where(row < s, 0.0, pltpu.roll(u, s, 0))
            u = a * u_sh + u
            a = a * a_sh
            s *= 2
        hs = u + a * h_ref[:, cols]
        h_ref[:, cols] = hs[tb - 1:tb, :]
        o_ref[:, cols] = hs * _gelu(gate_ref[:, cols])

    xbuf[0:SUBLANES, :] = xbuf[tb:tb + SUBLANES, :]


def _block_diag_groups(w):
    per = (2 * LANES) // LRU_BW
    wg = w.reshape(LRU_BLOCKS // per, per, LRU_BW, LRU_BW)
    eye = jnp.eye(per, dtype=w.dtype)
    return jnp.einsum('gaij,ab->gaibj', wg, eye).reshape(LRU_BLOCKS // per, per * LRU_BW, per * LRU_BW)


def lru_core(p, conv_w, conv_b, w_a, b_a, w_x, b_x, lam, tb=256):
    bsz, t, _ = p.shape
    width = LRU_WIDTH
    grp = 2 * LANES
    wax = jnp.concatenate([_block_diag_groups(w_a), _block_diag_groups(w_x)], axis=-1).astype(BF16)
    vec = lambda a: a.reshape(1, width)
    return pl.pallas_call(
        functools.partial(_lru_kernel, tb=tb),
        grid=(bsz, t // tb),
        in_specs=[
            pl.BlockSpec((None, tb, width), lambda b, i: (b, i, 0)),
            pl.BlockSpec((None, tb, width), lambda b, i: (b, i, 1)),
            pl.BlockSpec((CONV_K, width), lambda b, i: (0, 0)),
            pl.BlockSpec((1, width), lambda b, i: (0, 0)),
            pl.BlockSpec((width // grp, grp, 2 * grp), lambda b, i: (0, 0, 0)),
            pl.BlockSpec((1, width), lambda b, i: (0, 0)),
            pl.BlockSpec((1, width), lambda b, i: (0, 0)),
            pl.BlockSpec((1, width), lambda b, i: (0, 0)),
        ],
        out_specs=pl.BlockSpec((None, tb, width), lambda b, i: (b, i, 0)),
        out_shape=jax.ShapeDtypeStruct((bsz, t, width), F32),
        scratch_shapes=[pltpu.VMEM((tb + SUBLANES, width), F32), pltpu.VMEM((1, width), F32)],
        compiler_params=_params("arbitrary", "arbitrary"),
        name="lru_core",
    )(p, p, conv_w, vec(conv_b), wax, vec(b_a), vec(b_x), vec(lam))


def _cmp_kernel(x_ref, pelo_ref, pehi_ref, w1a_ref, w1b_ref, w2_ref, o_ref):
    x = x_ref[...]
    nseg = x.shape[0]
    first = _dot((x + pelo_ref[...]).astype(BF16), w1a_ref[...])
    second = _dot((x + pehi_ref[...]).astype(BF16), w1b_ref[...])
    hid = _gelu(first + pltpu.roll(second, nseg - 1, 0))
    o_ref[...] = _dot(hid.astype(BF16), w2_ref[...])


def nsa_compress(xseg, pe, w1, w2):
    _, bsz, g, nseg, seg_w = xseg.shape
    hid = w1.shape[-1]
    pelo = pe[:, :CMP_STRIDE].reshape(2, 1, seg_w)
    pehi = pe[:, CMP_STRIDE:].reshape(2, 1, seg_w)
    w1a = w1[:, :seg_w].astype(BF16)
    w1b = w1[:, seg_w:].astype(BF16)
    return pl.pallas_call(
        _cmp_kernel,
        grid=(2, bsz, g),
        in_specs=[
            pl.BlockSpec((None, None, None, nseg, seg_w), lambda a, b, c: (a, b, c, 0, 0)),
            pl.BlockSpec((None, 1, seg_w), lambda a, b, c: (a, 0, 0)),
            pl.BlockSpec((None, 1, seg_w), lambda a, b, c: (a, 0, 0)),
            pl.BlockSpec((None, seg_w, hid), lambda a, b, c: (a, 0, 0)),
            pl.BlockSpec((None, seg_w, hid), lambda a, b, c: (a, 0, 0)),
            pl.BlockSpec((None, hid, NSA_DK), lambda a, b, c: (a, 0, 0)),
        ],
        out_specs=pl.BlockSpec((None, None, None, nseg, NSA_DK), lambda a, b, c: (a, b, c, 0, 0)),
        out_shape=jax.ShapeDtypeStruct((2, bsz, g, nseg, NSA_DK), F32),
        compiler_params=_params("arbitrary", "arbitrary", "arbitrary"),
        name="nsa_compress",
    )(xseg, pelo, pehi, w1a, w1b, w2.astype(BF16))


def _masked_softmax_rows(s, mask):
    s = jnp.where(mask, s, NEG)
    m = jnp.max(s, axis=-1, keepdims=True)
    e = jnp.where(mask, jnp.exp(s - m), 0.0)
    l = jnp.sum(e, axis=-1, keepdims=True)
    return e * (1.0 / jnp.where(l > 0.0, l, 1.0))


def _nsa_kernel(q_ref, gate_ref, kc_ref, vc_ref, ks_ref, vs_ref, kw_ref, vw_ref, o_ref, *, t_len):
    p_heads, dk, qb = NSA_HPG, NSA_DK, Q_BLOCK
    rows = p_heads * qb
    n_cmp_rows = kc_ref.shape[0]
    n_slc = t_len // SLC_LEN
    n_sel = min(N_SEL, n_slc)
    kc_len = SEL_KEY_CHUNK
    blk_per_chunk = kc_len // SLC_LEN
    t0 = pl.program_id(2) * qb

    qg = q_ref[...]
    q16 = (jnp.concatenate([qg[:, p * dk:(p + 1) * dk] for p in range(p_heads)], axis=0) * (dk ** -0.5)).astype(BF16)
    tpos = t0 + (lax.broadcasted_iota(jnp.int32, (rows, 1), 0) % qb)

    s_c = _dot_nt(q16, kc_ref[...].astype(BF16))
    n_idx = lax.broadcasted_iota(jnp.int32, (rows, n_cmp_rows), 1)
    p_c = _masked_softmax_rows(s_c, (n_idx * CMP_STRIDE + (CMP_LEN - 1)) <= tpos)
    o_c = _dot(p_c.astype(BF16), vc_ref[...].astype(BF16))

    p_sum = p_c[0:qb]
    for p in range(1, p_heads):
        p_sum = p_sum + p_c[p * qb:(p + 1) * qb]
    cn = lax.broadcasted_iota(jnp.int32, (n_cmp_rows, n_slc), 0) * CMP_STRIDE
    sj = lax.broadcasted_iota(jnp.int32, (n_cmp_rows, n_slc), 1) * SLC_LEN
    ov = jnp.maximum(jnp.minimum(cn + CMP_LEN, sj + SLC_LEN) - jnp.maximum(cn, sj), 0)
    overlap = (ov.astype(F32) * (1.0 / CMP_LEN)).astype(BF16)
    imp = _dot_rhs_exact(p_sum, overlap)
    s_idx = lax.broadcasted_iota(jnp.int32, (qb, n_slc), 1)
    tq = t0 + lax.broadcasted_iota(jnp.int32, (qb, n_slc), 0)
    cur = tq // SLC_LEN
    forced = (s_idx == 0) | (s_idx == cur) | (s_idx == cur - 1)
    imp = jnp.where(forced, BIG, imp)
    imp = jnp.where(s_idx * SLC_LEN <= tq, imp, -BIG)
    s_idx_f = s_idx.astype(F32)
    sel = jnp.zeros((qb, n_slc), F32)
    for _ in range(n_sel):
        m = jnp.max(imp, axis=-1, keepdims=True)
        first = jnp.min(jnp.where(imp == m, s_idx_f, float(n_slc)), axis=-1, keepdims=True)
        pick = s_idx_f == first
        sel = jnp.where(pick, 1.0, sel)
        imp = jnp.where(pick, LOWEST, imp)
    sel16 = sel.astype(BF16)

    blk_row = lax.broadcasted_iota(jnp.int32, (n_slc, kc_len), 0)
    blk_col = lax.broadcasted_iota(jnp.int32, (n_slc, kc_len), 1) // SLC_LEN
    key_off = lax.broadcasted_iota(jnp.int32, (rows, kc_len), 1)

    def sel_body(c, carry):
        m, l, acc = carry
        k0 = pl.multiple_of(c * kc_len, kc_len)
        s = _dot_nt(q16, ks_ref[pl.ds(k0, kc_len), :])
        expand = jnp.where(blk_row == c * blk_per_chunk + blk_col, 1.0, 0.0).astype(BF16)
        selx = _dot(sel16, expand)
        selx = jnp.concatenate([selx] * p_heads, axis=0)
        mask = (selx > 0.5) & ((k0 + key_off) <= tpos)
        s = jnp.where(mask, s, NEG)
        m_new = jnp.maximum(m, jnp.max(s, axis=-1, keepdims=True))
        scale = jnp.exp(m - m_new)
        e = jnp.where(mask, jnp.exp(s - m_new), 0.0)
        l = scale * l + jnp.sum(e, axis=-1, keepdims=True)
        acc = scale * acc + _dot(e.astype(BF16), vs_ref[pl.ds(k0, kc_len), :])
        return m_new, l, acc

    n_chunks = (t0 + qb + kc_len - 1) // kc_len
    init = (jnp.full((rows, 1), NEG, F32), jnp.zeros((rows, 1), F32), jnp.zeros((rows, dk), F32))
    _, l_s, acc_s = lax.fori_loop(0, n_chunks, sel_body, init)
    o_s = acc_s * (1.0 / jnp.where(l_s > 0.0, l_s, 1.0))

    w_len = qb + WINDOW
    w0 = pl.multiple_of(t0, qb)
    s_w = _dot_nt(q16, kw_ref[pl.ds(w0, w_len), :])
    kpos_w = t0 - WINDOW + lax.broadcasted_iota(jnp.int32, (rows, w_len), 1)
    mask_w = (kpos_w <= tpos) & (kpos_w > tpos - WINDOW) & (kpos_w >= 0)
    p_w = _masked_softmax_rows(s_w, mask_w)
    o_w = _dot(p_w.astype(BF16), vw_ref[pl.ds(w0, w_len), :])

    gts = _sigmoid(gate_ref[...])
    outs = []
    for p in range(p_heads):
        r = slice(p * qb, (p + 1) * qb)
        outs.append(gts[:, 3 * p:3 * p + 1] * o_c[r] + gts[:, 3 * p + 1:3 * p + 2] * o_s[r]
                    + gts[:, 3 * p + 2:3 * p + 3] * o_w[r])
    o_ref[...] = jnp.concatenate(outs, axis=1)


def nsa_attention(p, gates, kvcmp, kv16):
    bsz, t, _ = p.shape
    g, dk, qb = NSA_GROUPS, NSA_DK, Q_BLOCK
    nseg = kvcmp.shape[3]
    kwin = jnp.pad(kv16[2:4], ((0, 0), (0, 0), (0, 0), (WINDOW, 0), (0, 0)))
    qw = NSA_HPG * dk
    full = lambda n: pl.BlockSpec((None, None, n, dk), lambda b, c, j: (b, c, 0, 0))
    return pl.pallas_call(
        functools.partial(_nsa_kernel, t_len=t),
        grid=(bsz, g, t // qb),
        in_specs=[
            pl.BlockSpec((None, qb, qw), lambda b, c, j: (b, j, c)),
            pl.BlockSpec((None, None, qb, NSA_HPG * 3), lambda b, c, j: (b, c, j, 0)),
            full(nseg), full(nseg), full(t), full(t), full(t + WINDOW), full(t + WINDOW),
        ],
        out_specs=pl.BlockSpec((None, qb, qw), lambda b, c, j: (b, j, c)),
        out_shape=jax.ShapeDtypeStruct((bsz, t, g * qw), F32),
        compiler_params=_params("arbitrary", "arbitrary", "arbitrary"),
        name="nsa_attention",
    )(p, gates, kvcmp[0], kvcmp[1], kv16[0], kv16[1], kwin[0], kwin[1])


def nsa_mixer_core(x, sc, sh, w_in, pe_k, pe_v, ck_w1, ck_w2, cv_w1, cv_w2):
    bsz, t, _ = x.shape
    g, dk = NSA_GROUPS, NSA_DK
    p = modproj(x, sc, sh, _pad_cols(w_in).astype(BF16))
    kv = p[:, :, NSA_Q:NSA_Q + 6 * NSA_KV].reshape(bsz, t, 6, g, dk).transpose(2, 0, 3, 1, 4)
    xseg = kv[0:2].reshape(2, bsz, g, t // CMP_STRIDE, CMP_STRIDE * dk)
    kvcmp = nsa_compress(xseg, jnp.stack([pe_k, pe_v]), jnp.stack([ck_w1, cv_w1]), jnp.stack([ck_w2, cv_w2]))
    gates = p[:, :, NSA_Q + 6 * NSA_KV:NSA_Q + 6 * NSA_KV + 3 * g * NSA_HPG]
    gates = gates.reshape(bsz, t, g, 3 * NSA_HPG).transpose(0, 2, 1, 3)
    return nsa_attention(p, gates, kvcmp, kv[2:6].astype(BF16))


def _outproj_ln_kernel(o_ref, w_ref, x_ref, gt_ref, g_ref, b_ref, sc_ref, sh_ref, wr_ref, br_ref,
                       x1_ref, hf_ref, rt_ref, cnt_ref, cnt_acc):
    @pl.when((pl.program_id(0) == 0) & (pl.program_id(1) == 0))
    def _():
        cnt_acc[...] = jnp.zeros_like(cnt_acc)

    y = _dot(o_ref[...].astype(BF16), w_ref[...])
    x1 = _layer_norm(ALPHA * x_ref[...] + (1.0 + gt_ref[...]) * y, g_ref[...], b_ref[...])
    x1_ref[...] = x1
    hf = x1 * (1.0 + sc_ref[...]) + sh_ref[...]
    hf_ref[...] = hf

    logits = _dot_f32(hf, wr_ref[...]) + br_ref[...]
    lane = lax.broadcasted_iota(jnp.int32, logits.shape, 1).astype(F32)
    is_grp = lane < N_GROUPS
    gl = jnp.where(is_grp, logits, LOWEST)
    gmax = jnp.max(gl, axis=-1, keepdims=True)
    gi = jnp.min(jnp.where(gl == gmax, lane, float(LANES)), axis=-1, keepdims=True)
    gp = 1.0 / jnp.sum(jnp.where(is_grp, jnp.exp(logits - gmax), 0.0), axis=-1, keepdims=True)
    lo = N_GROUPS + EXP_PER_GROUP * gi
    el = jnp.where((lane >= lo) & (lane < lo + EXP_PER_GROUP), logits, LOWEST)
    m1 = jnp.max(el, axis=-1, keepdims=True)
    i1 = jnp.min(jnp.where(el == m1, lane, float(LANES)), axis=-1, keepdims=True)
    el2 = jnp.where(lane == i1, LOWEST, el)
    m2 = jnp.max(el2, axis=-1, keepdims=True)
    i2 = jnp.min(jnp.where(el2 == m2, lane, float(LANES)), axis=-1, keepdims=True)
    e2 = jnp.exp(m2 - m1)
    g1 = gp / (1.0 + e2)
    g2 = gp * e2 / (1.0 + e2)

    tm = logits.shape[0]
    onehot = jnp.where((lane == i1) | (lane == i2), 1.0, 0.0)
    ri = lax.broadcasted_iota(jnp.int32, (tm, tm), 0)
    ci = lax.broadcasted_iota(jnp.int32, (tm, tm), 1)
    before = jnp.where(ci < ri, 1.0, 0.0).astype(BF16)
    seen = _dot(before, onehot.astype(BF16)) + cnt_acc[...]
    rank1 = jnp.sum(jnp.where(lane == i1, seen, 0.0), axis=-1, keepdims=True)
    rank2 = jnp.sum(jnp.where(lane == i2, seen, 0.0), axis=-1, keepdims=True)
    cnt_acc[...] = cnt_acc[...] + jnp.sum(onehot, axis=0, keepdims=True)
    cnt_ref[...] = cnt_acc[...]

    vals = (i1 - N_GROUPS, i2 - N_GROUPS, g1, g2, rank1, rank2)
    out = jnp.zeros_like(logits)
    for pos, val in enumerate(vals):
        out = jnp.where(lane == float(pos), val, out)
    rt_ref[...] = out


def outproj_ln(o, w_out, x, gt, ln_g, ln_b, sc2, sh2, w_router, b_router, tm=256):
    bsz, t, d = x.shape
    din = o.shape[-1]
    row = lambda: pl.BlockSpec((None, tm, d), lambda b, i: (b, i, 0))
    per_b = lambda: pl.BlockSpec((None, 1, d), lambda b, i: (b, 0, 0))
    const = lambda r, c: pl.BlockSpec((r, c), lambda b, i: (0, 0))
    return pl.pallas_call(
        _outproj_ln_kernel,
        grid=(bsz, t // tm),
        in_specs=[
            pl.BlockSpec((None, tm, din), lambda b, i: (b, i, 0)),
            const(din, d), row(), per_b(), const(1, d), const(1, d), per_b(), per_b(),
            const(d, LANES), const(1, LANES),
        ],
        out_specs=[row(), row(), pl.BlockSpec((None, tm, LANES), lambda b, i: (b, i, 0)), const(1, LANES)],
        out_shape=[jax.ShapeDtypeStruct((bsz, t, d), F32), jax.ShapeDtypeStruct((bsz, t, d), F32),
                   jax.ShapeDtypeStruct((bsz, t, LANES), F32), jax.ShapeDtypeStruct((1, LANES), F32)],
        scratch_shapes=[pltpu.VMEM((1, LANES), F32)],
        compiler_params=_params("arbitrary", "arbitrary"),
        name="outproj_ln",
    )(o, w_out, x, gt, ln_g.reshape(1, d), ln_b.reshape(1, d), sc2, sh2, w_router, b_router)


def _res_ln_kernel(x_ref, y_ref, gt_ref, g_ref, b_ref, o_ref):
    o_ref[...] = _layer_norm(ALPHA * x_ref[...] + (1.0 + gt_ref[...]) * y_ref[...], g_ref[...], b_ref[...])


def res_ln(x, y, gt, ln_g, ln_b, tm=512):
    bsz, t, d = x.shape
    row = lambda: pl.BlockSpec((None, tm, d), lambda b, i: (b, i, 0))
    return pl.pallas_call(
        _res_ln_kernel,
        grid=(bsz, t // tm),
        in_specs=[row(), row(), pl.BlockSpec((None, 1, d), lambda b, i: (b, 0, 0)),
                  pl.BlockSpec((1, d), lambda b, i: (0, 0)), pl.BlockSpec((1, d), lambda b, i: (0, 0))],
        out_specs=row(),
        out_shape=jax.ShapeDtypeStruct((bsz, t, d), F32),
        compiler_params=_params("arbitrary", "arbitrary"),
        name="res_ln",
    )(x, y, gt, ln_g.reshape(1, d), ln_b.reshape(1, d))


def _moe_kernel(blk_e_ref, nused_ref, x_ref, wg_ref, wu_ref, wd_ref, o_ref, wg16, wu16, wd16):
    i = pl.program_id(0)
    e = blk_e_ref[i]
    e_prev = blk_e_ref[jnp.maximum(i - 1, 0)]

    @pl.when((i == 0) | (e != e_prev))
    def _():
        wg16[...] = wg_ref[...].astype(BF16)
        wu16[...] = wu_ref[...].astype(BF16)
        wd16[...] = wd_ref[...].astype(BF16)

    @pl.when(i < nused_ref[0])
    def _():
        x = x_ref[...].astype(BF16)
        gate = _dot(x, wg16[...])
        up = _dot(x, wu16[...])
        o_ref[...] = _dot((_silu(gate) * up).astype(BF16), wd16[...])

    @pl.when(i >= nused_ref[0])
    def _():
        o_ref[...] = jnp.zeros_like(o_ref)


def moe_ffn(xg, blk_e, n_used, w_gate, w_up, w_down):
    rows, d = xg.shape
    de = w_gate.shape[-1]
    n_blk = rows // MOE_BLOCK
    grid_spec = pltpu.PrefetchScalarGridSpec(
        num_scalar_prefetch=2,
        grid=(n_blk,),
        in_specs=[
            pl.BlockSpec((MOE_BLOCK, d), lambda i, be, nu: (i, 0)),
            pl.BlockSpec((None, d, de), lambda i, be, nu: (be[i], 0, 0)),
            pl.BlockSpec((None, d, de), lambda i, be, nu: (be[i], 0, 0)),
            pl.BlockSpec((None, de, d), lambda i, be, nu: (be[i], 0, 0)),
        ],
        out_specs=pl.BlockSpec((MOE_BLOCK, d), lambda i, be, nu: (i, 0)),
        scratch_shapes=[pltpu.VMEM((d, de), BF16), pltpu.VMEM((d, de), BF16), pltpu.VMEM((de, d), BF16)],
    )
    return pl.pallas_call(
        _moe_kernel,
        grid_spec=grid_spec,
        out_shape=jax.ShapeDtypeStruct((rows, d), F32),
        compiler_params=_params("arbitrary"),
        name="moe_ffn",
    )(blk_e, n_used, xg, w_gate, w_up, w_down)


def _dispatch_kernel(ps_ref, idx_ref, x_ref, xg_in, xg_out, dest_ref, sem, *, tile):
    del xg_in

    def row_copy(r, d):
        return pltpu.make_async_copy(x_ref.at[pl.ds(r, 1)], xg_out.at[pl.ds(d, 1)], sem)

    def issue(r, carry):
        for k in range(TOP_K):
            d = ps_ref[idx_ref[0, 2 * TOP_K * r + k]] + idx_ref[0, 2 * TOP_K * r + TOP_K + k]
            dest_ref[0, TOP_K * r + k] = d
            row_copy(r, d).start()
        return carry

    lax.fori_loop(0, tile, issue, 0)

    def drain(r, carry):
        for k in range(TOP_K):
            row_copy(r, dest_ref[0, TOP_K * r + k]).wait()
        return carry

    lax.fori_loop(0, tile, drain, 0)


def moe_dispatch(hf, idx, pad_start, rows, tile=256):
    n, d = hf.shape
    n_tiles = n // tile
    grid_spec = pltpu.PrefetchScalarGridSpec(
        num_scalar_prefetch=1,
        grid=(n_tiles,),
        in_specs=[
            pl.BlockSpec((None, 1, 2 * TOP_K * tile), lambda i, ps: (i, 0, 0), memory_space=pltpu.SMEM),
            pl.BlockSpec((tile, d), lambda i, ps: (i, 0)),
            pl.BlockSpec(memory_space=pl.ANY),
        ],
        out_specs=[
            pl.BlockSpec(memory_space=pl.ANY),
            pl.BlockSpec((None, 1, TOP_K * tile), lambda i, ps: (i, 0, 0), memory_space=pltpu.SMEM),
        ],
        scratch_shapes=[pltpu.SemaphoreType.DMA(())],
    )
    xg, dest = pl.pallas_call(
        functools.partial(_dispatch_kernel, tile=tile),
        grid_spec=grid_spec,
        out_shape=[jax.ShapeDtypeStruct((rows, d), hf.dtype),
                   jax.ShapeDtypeStruct((n_tiles, 1, TOP_K * tile), jnp.int32)],
        input_output_aliases={3: 0},
        compiler_params=_params("arbitrary"),
        name="moe_dispatch",
    )(pad_start, idx.reshape(n_tiles, 1, 2 * TOP_K * tile), hf, jnp.zeros((rows, d), hf.dtype))
    return xg, dest.reshape(n * TOP_K)


def hier_moe_apply(hf, route, counts, w_gate, w_up, w_down):
    bsz, t, d = hf.shape
    n = bsz * t
    n_exp = w_gate.shape[0]
    route = route.reshape(n, LANES)
    idx = jnp.concatenate([route[:, :TOP_K], route[:, 2 * TOP_K:3 * TOP_K]], axis=1).astype(jnp.int32)
    gate = route[:, TOP_K:2 * TOP_K].reshape(-1)
    counts = counts[0, N_GROUPS:N_GROUPS + n_exp].astype(jnp.int32)
    padded = (counts + MOE_BLOCK - 1) // MOE_BLOCK * MOE_BLOCK
    pad_end = jnp.cumsum(padded)
    pad_start = (pad_end - padded).astype(jnp.int32)
    n_blk = -(-(n * TOP_K) // MOE_BLOCK) + n_exp
    blk_first = jnp.arange(n_blk, dtype=jnp.int32) * MOE_BLOCK
    blk_e = jnp.minimum(jnp.sum((pad_end[None, :] <= blk_first[:, None]).astype(jnp.int32), axis=1), n_exp - 1)
    n_used = (pad_end[-1:] // MOE_BLOCK).astype(jnp.int32)
    xg, dest = moe_dispatch(hf.reshape(n, d), idx, pad_start, n_blk * MOE_BLOCK)
    yb = moe_ffn(xg, blk_e.astype(jnp.int32), n_used, w_gate, w_up, w_down)
    contrib = yb[dest] * gate[:, None]
    return jnp.sum(contrib.reshape(n, TOP_K, d), axis=1).reshape(bsz, t, d)


def _pad_cols(w, mult=LANES):
    pad = (-w.shape[-1]) % mult
    return jnp.pad(w, ((0, 0), (0, pad))) if pad else w


def kernel(x, c, ada_w, ada_b, ln1_g, ln1_b, ln2_g, ln2_b, gdn_w_in, gdn_conv_w, gdn_a_log, gdn_dt_bias, gdn_norm_g, gdn_w_out, lru_w_in, lru_conv_w, lru_conv_b, lru_w_a, lru_b_a, lru_w_x, lru_b_x, lru_lambda, lru_w_out, nsa_w_in, nsa_pe_k, nsa_pe_v, nsa_ck_w1, nsa_ck_w2, nsa_cv_w1, nsa_cv_w2, nsa_w_out, moe_w_grp, moe_b_grp, moe_w_exp, moe_b_exp, moe_w_gate, moe_w_up, moe_w_down):
    bsz, t, d = x.shape
    depth = ada_w.shape[0]
    mod = ada_modulation(c, ada_w, ada_b).reshape(depth, bsz, 6, 1, d)
    ja = jb = jc = 0
    for i in range(depth):
        sh1, sc1, gt1, sh2, sc2, gt2 = (mod[i, :, k] for k in range(6))
        kind = i % N_MIXERS
        if kind == 0:
            p = modproj(x, sc1, sh1, _pad_cols(gdn_w_in[ja]).astype(BF16))
            o = gdn_core(p, gdn_conv_w[ja], gdn_a_log[ja], gdn_dt_bias[ja], gdn_norm_g[ja])
            w_out = gdn_w_out[ja]
            ja += 1
        elif kind == 1:
            p = modproj(x, sc1, sh1, lru_w_in[jb].astype(BF16))
            o = lru_core(p, lru_conv_w[jb], lru_conv_b[jb], lru_w_a[jb], lru_b_a[jb], lru_w_x[jb], lru_b_x[jb],
                         lru_lambda[jb])
            w_out = lru_w_out[jb]
            jb += 1
        else:
            o = nsa_mixer_core(x, sc1, sh1, nsa_w_in[jc], nsa_pe_k[jc], nsa_pe_v[jc], nsa_ck_w1[jc], nsa_ck_w2[jc],
                               nsa_cv_w1[jc], nsa_cv_w2[jc])
            w_out = nsa_w_out[jc]
            jc += 1
        w_router = _pad_cols(jnp.concatenate([moe_w_grp[i], moe_w_exp[i]], axis=1))
        b_router = _pad_cols(jnp.concatenate([moe_b_grp[i], moe_b_exp[i]])[None, :])
        x1, hf, route, counts = outproj_ln(o, w_out.astype(BF16), x, gt1, ln1_g[i], ln1_b[i], sc2, sh2,
                                           w_router, b_router)
        y = hier_moe_apply(hf, route, counts, moe_w_gate[i], moe_w_up[i], moe_w_down[i])
        x = res_ln(x1, y, gt2, ln2_g[i], ln2_b[i])
    return x
```

```python
import functools

import jax
import jax.numpy as jnp
from jax import lax
from jax.experimental import pallas as pl
from jax.experimental.pallas import tpu as pltpu

F32 = jnp.float32
BF16 = jnp.bfloat16

D_MODEL = 1024
DEPTH = 4
N_MIXERS = 3
LN_EPS = 1e-5
NORM_EPS = 1e-6
CONV_K = 4
ALPHA = (2.0 * DEPTH) ** 0.25
GDN_HEADS = 8
GDN_DK = 128
GDN_CHUNK = 64
GDN_QKV = 3 * GDN_HEADS * GDN_DK
LRU_WIDTH = 1024
LRU_BLOCKS = 16
LRU_BW = LRU_WIDTH // LRU_BLOCKS
RG_C = 8.0
NSA_GROUPS = 4
NSA_HPG = 4
NSA_DK = 64
NSA_Q = NSA_GROUPS * NSA_HPG * NSA_DK
NSA_KV = NSA_GROUPS * NSA_DK
CMP_LEN = 32
CMP_STRIDE = 16
SLC_LEN = 64
N_SEL = 16
WINDOW = 512
Q_BLOCK = 128
SEL_KEY_CHUNK = 512
N_GROUPS = 4
EXP_PER_GROUP = 8
N_EXPERTS = N_GROUPS * EXP_PER_GROUP
TOP_K = 2
MOE_BLOCK = 256
NEG = -1e30
BIG = 1e30
LOWEST = -3e38

LANES = 128
SUBLANES = 8
VMEM_LIMIT_BYTES = 48 * 1024 * 1024


def _params(*sem):
    return pltpu.CompilerParams(dimension_semantics=sem, vmem_limit_bytes=VMEM_LIMIT_BYTES)


def _sigmoid(x):
    return 1.0 / (1.0 + jnp.exp(-x))


def _silu(x):
    return x * _sigmoid(x)


def _softplus(x):
    return jnp.maximum(x, 0.0) + jnp.log(1.0 + jnp.exp(-jnp.abs(x)))


def _gelu(x):
    return 0.5 * x * (1.0 + jnp.tanh(0.7978845608028654 * (x + 0.044715 * (x * x * x))))


def _dot(a, b):
    return jnp.dot(a, b, preferred_element_type=F32)


def _dot_nt(a, b):
    return lax.dot_general(a, b, (((1,), (1,)), ((), ())), preferred_element_type=F32)


def _dot_tn(a, b):
    return lax.dot_general(a, b, (((0,), (0,)), ((), ())), preferred_element_type=F32)


def _split3(x):
    hi = x.astype(BF16)
    r1 = x - hi.astype(F32)
    mid = r1.astype(BF16)
    lo = (r1 - mid.astype(F32)).astype(BF16)
    return hi, mid, lo


def _dot_lhs_exact(m_bf16, x):
    hi, mid, lo = _split3(x)
    return _dot(m_bf16, hi) + _dot(m_bf16, mid) + _dot(m_bf16, lo)


def _dot_rhs_exact(x, m_bf16):
    hi, mid, lo = _split3(x)
    return _dot(hi, m_bf16) + _dot(mid, m_bf16) + _dot(lo, m_bf16)


def _dot_f32(a, b):
    ah, am, al = _split3(a)
    bh, bm, bl = _split3(b)
    return (_dot(ah, bh) + _dot(ah, bm) + _dot(am, bh)) + (_dot(ah, bl) + _dot(al, bh) + _dot(am, bm))


def _layer_norm(z, g, b):
    mu = jnp.mean(z, axis=-1, keepdims=True)
    zc = z - mu
    var = jnp.mean(zc * zc, axis=-1, keepdims=True)
    return zc * lax.rsqrt(var + LN_EPS) * g + b


def _causal_conv(xbuf, cw_ref, col0, width, tb):
    cols = slice(col0, col0 + width)
    acc = cw_ref[0:1, cols] * xbuf[SUBLANES - 3:SUBLANES - 3 + tb, cols]
    for k in range(1, CONV_K):
        off = SUBLANES - 3 + k
        acc = acc + cw_ref[k:k + 1, cols] * xbuf[off:off + tb, cols]
    return acc


def _ada_kernel(c_ref, w_ref, b_ref, o_ref):
    cond = _silu(c_ref[...])
    o_ref[...] = _dot_f32(cond, w_ref[...]) + b_ref[...]


def ada_modulation(c, ada_w, ada_b, col_tile=1536):
    depth, d, n6 = ada_w.shape
    bsz = c.shape[0]
    return pl.pallas_call(
        _ada_kernel,
        grid=(depth, n6 // col_tile),
        in_specs=[
            pl.BlockSpec((bsz, d), lambda i, j: (0, 0)),
            pl.BlockSpec((None, d, col_tile), lambda i, j: (i, 0, j)),
            pl.BlockSpec((None, 1, col_tile), lambda i, j: (i, 0, j)),
        ],
        out_specs=pl.BlockSpec((None, bsz, col_tile), lambda i, j: (i, 0, j)),
        out_shape=jax.ShapeDtypeStruct((depth, bsz, n6), F32),
        compiler_params=_params("arbitrary", "arbitrary"),
        name="ada_modulation",
    )(c, ada_w, ada_b.reshape(depth, 1, n6))


def _modproj_kernel(x_ref, sc_ref, sh_ref, w_ref, o_ref, *, col_chunk):
    h = (x_ref[...] * (1.0 + sc_ref[...]) + sh_ref[...]).astype(BF16)
    ncols = o_ref.shape[-1]
    for c0 in range(0, ncols, col_chunk):
        c1 = min(c0 + col_chunk, ncols)
        o_ref[:, c0:c1] = _dot(h, w_ref[:, c0:c1])


def modproj(x, sc, sh, w, tm=256, col_chunk=1024):
    bsz, t, d = x.shape
    ncols = w.shape[1]
    return pl.pallas_call(
        functools.partial(_modproj_kernel, col_chunk=col_chunk),
        grid=(bsz, t // tm),
        in_specs=[
            pl.BlockSpec((None, tm, d), lambda b, i: (b, i, 0)),
            pl.BlockSpec((None, 1, d), lambda b, i: (b, 0, 0)),
            pl.BlockSpec((None, 1, d), lambda b, i: (b, 0, 0)),
            pl.BlockSpec((d, ncols), lambda b, i: (0, 0)),
        ],
        out_specs=pl.BlockSpec((None, tm, ncols), lambda b, i: (b, i, 0)),
        out_shape=jax.ShapeDtypeStruct((bsz, t, ncols), F32),
        compiler_params=_params("arbitrary", "arbitrary"),
        name="modproj",
    )(x, sc, sh, w)


def _gdn_kernel(qkv_ref, z_ref, ab_ref, cw_ref, alog_ref, dtb_ref, ng_ref, o_ref, xbuf, s_ref, *, tb):
    nh, dk, chunk = GDN_HEADS, GDN_DK, GDN_CHUNK

    @pl.when(pl.program_id(1) == 0)
    def _():
        xbuf[0:SUBLANES, :] = jnp.zeros((SUBLANES, GDN_QKV), F32)
        s_ref[...] = jnp.zeros_like(s_ref)

    xbuf[SUBLANES:SUBLANES + tb, :] = qkv_ref[...]

    ab = ab_ref[...]
    g_all = -jnp.exp(alog_ref[...]) * _softplus(ab + dtb_ref[...])
    beta_all = _sigmoid(ab)
    ri = lax.broadcasted_iota(jnp.int32, (tb, tb), 0)
    ci = lax.broadcasted_iota(jnp.int32, (tb, tb), 1)
    same = (ri // chunk) == (ci // chunk)
    causal = same & (ci <= ri)
    strict = same & (ci < ri)
    eye = jnp.where(ri == ci, 1.0, 0.0)
    tri = jnp.where(causal, 1.0, 0.0).astype(BF16)
    ones_blk = jnp.where(same, 1.0, 0.0).astype(BF16)
    gc_all = _dot_lhs_exact(tri, g_all)
    glast_all = _dot_lhs_exact(ones_blk, g_all)
    gc_t = gc_all.T
    egc_all = jnp.exp(gc_all)
    kdec_all = jnp.exp(glast_all - gc_all)

    heads = range(nh)
    lmats, rhss, qk16s, qg16s, ks16s, gls = [], [], [], [], [], []
    for h in heads:
        q = _silu(_causal_conv(xbuf, cw_ref, h * dk, dk, tb))
        k = _silu(_causal_conv(xbuf, cw_ref, (nh + h) * dk, dk, tb))
        v = _silu(_causal_conv(xbuf, cw_ref, (2 * nh + h) * dk, dk, tb))
        q = q * lax.rsqrt(jnp.sum(q * q, axis=-1, keepdims=True) + NORM_EPS)
        k = k * lax.rsqrt(jnp.sum(k * k, axis=-1, keepdims=True) + NORM_EPS)
        beta = beta_all[:, nh + h:nh + h + 1]
        eg = egc_all[:, h:h + 1]
        decay = jnp.exp(jnp.where(causal, gc_all[:, h:h + 1] - gc_t[h:h + 1, :], NEG))
        kb = k * beta
        k16 = k.astype(BF16)
        qs = q * (dk ** -0.5)
        lmats.append(jnp.where(strict, _dot_nt(kb.astype(BF16), k16) * decay, 0.0))
        rhss.append(jnp.concatenate([v * beta, kb * eg], axis=1).astype(BF16))
        qk16s.append((_dot_nt(qs.astype(BF16), k16) * decay).astype(BF16))
        qg16s.append((qs * eg).astype(BF16))
        ks16s.append((k * kdec_all[:, h:h + 1]).astype(BF16))
        gls.append(glast_all[:, h:h + 1])
    xbuf[0:SUBLANES, :] = xbuf[tb:tb + SUBLANES, :]

    xpows = lmats
    tinvs = [eye - lm for lm in lmats]
    for _ in range(5):
        xpows = [_dot(x16, x16) for x16 in [x.astype(BF16) for x in xpows]]
        tinvs = [t + _dot(t.astype(BF16), x.astype(BF16)) for t, x in zip(tinvs, xpows)]
    uws = [_dot(t.astype(BF16), rhs) for t, rhs in zip(tinvs, rhss)]
    us = [uw[:, :dk] for uw in uws]
    w16s = [uw[:, dk:].astype(BF16) for uw in uws]

    states = [s_ref[h] for h in heads]
    o_parts = [[] for _ in heads]
    vn_parts = [[] for _ in heads]
    for c in range(tb // chunk):
        rows = slice(c * chunk, (c + 1) * chunk)
        for h in heads:
            s16 = states[h].astype(BF16)
            vn16 = (us[h][rows] - _dot(w16s[h][rows], s16)).astype(BF16)
            o_parts[h].append(_dot(qg16s[h][rows], s16))
            states[h] = (states[h] * jnp.exp(gls[h][c * chunk:c * chunk + 1, :])
                         + _dot_tn(ks16s[h][rows], vn16))
            vn_parts[h].append(vn16)
    for h in heads:
        s_ref[h] = states[h]
        o = jnp.concatenate(o_parts[h], axis=0) + _dot(qk16s[h], jnp.concatenate(vn_parts[h], axis=0))
        o = o * lax.rsqrt(jnp.mean(o * o, axis=-1, keepdims=True) + NORM_EPS) * ng_ref[...]
        o_ref[:, h * dk:(h + 1) * dk] = o * _silu(z_ref[:, h * dk:(h + 1) * dk])


def gdn_core(p, conv_w, a_log, dt_bias, norm_g, tb=256):
    bsz, t, _ = p.shape
    nh, dk = GDN_HEADS, GDN_DK
    v_w = nh * dk
    alog = jnp.zeros((1, LANES), F32).at[0, :nh].set(a_log)
    dtb = jnp.zeros((1, LANES), F32).at[0, :nh].set(dt_bias)
    ab_blk = (GDN_QKV + v_w) // LANES
    return pl.pallas_call(
        functools.partial(_gdn_kernel, tb=tb),
        grid=(bsz, t // tb),
        in_specs=[
            pl.BlockSpec((None, tb, GDN_QKV), lambda b, i: (b, i, 0)),
            pl.BlockSpec((None, tb, v_w), lambda b, i: (b, i, GDN_QKV // v_w)),
            pl.BlockSpec((None, tb, LANES), lambda b, i: (b, i, ab_blk)),
            pl.BlockSpec((CONV_K, GDN_QKV), lambda b, i: (0, 0)),
            pl.BlockSpec((1, LANES), lambda b, i: (0, 0)),
            pl.BlockSpec((1, LANES), lambda b, i: (0, 0)),
            pl.BlockSpec((1, dk), lambda b, i: (0, 0)),
        ],
        out_specs=pl.BlockSpec((None, tb, v_w), lambda b, i: (b, i, 0)),
        out_shape=jax.ShapeDtypeStruct((bsz, t, v_w), F32),
        scratch_shapes=[pltpu.VMEM((tb + SUBLANES, GDN_QKV), F32), pltpu.VMEM((nh, dk, dk), F32)],
        compiler_params=_params("arbitrary", "arbitrary"),
        name="gdn_core",
    )(p, p, p, conv_w, alog, dtb, norm_g.reshape(1, dk))


def _lru_kernel(gate_ref, xb_ref, cw_ref, cb_ref, wax_ref, ba_ref, bx_ref, lam_ref, o_ref, xbuf, h_ref, *, tb):
    width = LRU_WIDTH
    grp = 2 * LANES

    @pl.when(pl.program_id(1) == 0)
    def _():
        xbuf[0:SUBLANES, :] = jnp.zeros((SUBLANES, width), F32)
        h_ref[...] = jnp.zeros_like(h_ref)

    xbuf[SUBLANES:SUBLANES + tb, :] = xb_ref[...]
    row = lax.broadcasted_iota(jnp.int32, (tb, grp), 0)

    for j in range(width // grp):
        cols = slice(j * grp, (j + 1) * grp)
        xc = _causal_conv(xbuf, cw_ref, j * grp, grp, tb) + cb_ref[:, cols]
        res = _dot(xc.astype(BF16), wax_ref[j])
        r = _sigmoid(res[:, :grp] + ba_ref[:, cols])
        ig = _sigmoid(res[:, grp:] + bx_ref[:, cols])
        log_a = -RG_C * r * _softplus(-lam_ref[:, cols])
        a = jnp.exp(log_a)
        mult = jnp.sqrt(jnp.maximum(1.0 - jnp.exp(2.0 * log_a), 0.0))
        u = mult * ig * xc
        s = 1
        while s < tb:
            a_sh = jnp.where(row < s, 1.0, pltpu.roll(a, s, 0))
            u_sh = jnp.where(row < s, 0.0, pltpu.roll(u, s, 0))
            u = a * u_sh + u
            a = a * a_sh
            s *= 2
        hs = u + a * h_ref[:, cols]
        h_ref[:, cols] = hs[tb - 1:tb, :]
        o_ref[:, cols] = hs * _gelu(gate_ref[:, cols])

    xbuf[0:SUBLANES, :] = xbuf[tb:tb + SUBLANES, :]


def _block_diag_groups(w):
    per = (2 * LANES) // LRU_BW
    wg = w.reshape(LRU_BLOCKS // per, per, LRU_BW, LRU_BW)
    eye = jnp.eye(per, dtype=w.dtype)
    return jnp.einsum('gaij,ab->gaibj', wg, eye).reshape(LRU_BLOCKS // per, per * LRU_BW, per * LRU_BW)


def lru_core(p, conv_w, conv_b, w_a, b_a, w_x, b_x, lam, tb=256):
    bsz, t, _ = p.shape
    width = LRU_WIDTH
    grp = 2 * LANES
    wax = jnp.concatenate([_block_diag_groups(w_a), _block_diag_groups(w_x)], axis=-1).astype(BF16)
    vec = lambda a: a.reshape(1, width)
    return pl.pallas_call(
        functools.partial(_lru_kernel, tb=tb),
        grid=(bsz, t // tb),
        in_specs=[
            pl.BlockSpec((None, tb, width), lambda b, i: (b, i, 0)),
            pl.BlockSpec((None, tb, width), lambda b, i: (b, i, 1)),
            pl.BlockSpec((CONV_K, width), lambda b, i: (0, 0)),
            pl.BlockSpec((1, width), lambda b, i: (0, 0)),
            pl.BlockSpec((width // grp, grp, 2 * grp), lambda b, i: (0, 0, 0)),
            pl.BlockSpec((1, width), lambda b, i: (0, 0)),
            pl.BlockSpec((1, width), lambda b, i: (0, 0)),
            pl.BlockSpec((1, width), lambda b, i: (0, 0)),
        ],
        out_specs=pl.BlockSpec((None, tb, width), lambda b, i: (b, i, 0)),
        out_shape=jax.ShapeDtypeStruct((bsz, t, width), F32),
        scratch_shapes=[pltpu.VMEM((tb + SUBLANES, width), F32), pltpu.VMEM((1, width), F32)],
        compiler_params=_params("arbitrary", "arbitrary"),
        name="lru_core",
    )(p, p, conv_w, vec(conv_b), wax, vec(b_a), vec(b_x), vec(lam))


def _cmp_kernel(x_ref, pelo_ref, pehi_ref, w1a_ref, w1b_ref, w2_ref, o_ref):
    x = x_ref[...]
    nseg = x.shape[0]
    first = _dot((x + pelo_ref[...]).astype(BF16), w1a_ref[...])
    second = _dot((x + pehi_ref[...]).astype(BF16), w1b_ref[...])
    hid = _gelu(first + pltpu.roll(second, nseg - 1, 0))
    o_ref[...] = _dot(hid.astype(BF16), w2_ref[...]).astype(o_ref.dtype)


def nsa_compress(xseg, pe, w1, w2):
    _, bsz, g, nseg, seg_w = xseg.shape
    hid = w1.shape[-1]
    pelo = pe[:, :CMP_STRIDE].reshape(2, 1, seg_w)
    pehi = pe[:, CMP_STRIDE:].reshape(2, 1, seg_w)
    w1a = w1[:, :seg_w].astype(BF16)
    w1b = w1[:, seg_w:].astype(BF16)
    w2p = jnp.pad(w2, ((0, 0), (0, 0), (0, LANES - NSA_DK))).astype(BF16)
    return pl.pallas_call(
        _cmp_kernel,
        grid=(2, bsz, g),
        in_specs=[
            pl.BlockSpec((None, None, None, nseg, seg_w), lambda a, b, c: (a, b, c, 0, 0)),
            pl.BlockSpec((None, 1, seg_w), lambda a, b, c: (a, 0, 0)),
            pl.BlockSpec((None, 1, seg_w), lambda a, b, c: (a, 0, 0)),
            pl.BlockSpec((None, seg_w, hid), lambda a, b, c: (a, 0, 0)),
            pl.BlockSpec((None, seg_w, hid), lambda a, b, c: (a, 0, 0)),
            pl.BlockSpec((None, hid, LANES), lambda a, b, c: (a, 0, 0)),
        ],
        out_specs=pl.BlockSpec((None, None, None, nseg, LANES), lambda a, b, c: (a, b, c, 0, 0)),
        out_shape=jax.ShapeDtypeStruct((2, bsz, g, nseg, LANES), BF16),
        compiler_params=_params("arbitrary", "arbitrary", "arbitrary"),
        name="nsa_compress",
    )(xseg, pelo, pehi, w1a, w1b, w2p)


DEN_LANE = NSA_DK


def _nsa_kernel(q_ref, gate_ref, kc_ref, vc_ref, ks_ref, vs_ref, kw_ref, vw_ref, wbias_ref, o_ref, *, t_len):
    p_heads, dk, qb = NSA_HPG, NSA_DK, Q_BLOCK
    rows = p_heads * qb
    n_cmp_rows = kc_ref.shape[0]
    n_slc = t_len // SLC_LEN
    n_sel = min(N_SEL, n_slc)
    kc_len = SEL_KEY_CHUNK
    t0 = pl.program_id(2) * qb

    qg = q_ref[...]
    halves = []
    for p in range(p_heads):
        tile = qg[:, (p // 2) * LANES:(p // 2 + 1) * LANES]
        halves.append(tile if p % 2 == 0 else pltpu.roll(tile, dk, 1))
    lane = lax.broadcasted_iota(jnp.int32, (rows, LANES), 1)
    qz = jnp.where(lane < dk, jnp.concatenate(halves, axis=0) * (dk ** -0.5), 0.0)
    qz16 = qz.astype(BF16)
    tpos = t0 + (lax.broadcasted_iota(jnp.int32, (rows, 1), 0) % qb)

    w_len = qb + WINDOW
    w0 = pl.multiple_of(t0, qb)
    qw16 = jnp.where(lane == DEN_LANE, NEG, qz).astype(BF16)
    s_w = _dot_nt(qw16, kw_ref[pl.ds(w0, w_len), :]) + wbias_ref[...]
    e_w = jnp.exp(s_w - jnp.max(s_w, axis=-1, keepdims=True))
    acc_w = _dot(e_w.astype(BF16), vw_ref[pl.ds(w0, w_len), :])
    o_w = acc_w * (1.0 / acc_w[:, DEN_LANE:DEN_LANE + 1])

    s_c = _dot_nt(qz16, kc_ref[...])
    n_idx = lax.broadcasted_iota(jnp.int32, (rows, n_cmp_rows), 1)
    vis = (n_idx * CMP_STRIDE + (CMP_LEN - 1)) <= tpos
    s_c = jnp.where(vis, s_c, NEG)
    e_c = jnp.where(vis, jnp.exp(s_c - jnp.max(s_c, axis=-1, keepdims=True)), 0.0)
    l_c = jnp.sum(e_c, axis=-1, keepdims=True)
    p_c = e_c * (1.0 / jnp.where(l_c > 0.0, l_c, 1.0))
    o_c = _dot(p_c.astype(BF16), vc_ref[...])

    p_sum = p_c[0:qb]
    for p in range(1, p_heads):
        p_sum = p_sum + p_c[p * qb:(p + 1) * qb]
    sj = lax.broadcasted_iota(jnp.int32, (n_slc, n_cmp_rows), 0) * SLC_LEN
    cn = lax.broadcasted_iota(jnp.int32, (n_slc, n_cmp_rows), 1) * CMP_STRIDE
    ov = jnp.maximum(jnp.minimum(cn + CMP_LEN, sj + SLC_LEN) - jnp.maximum(cn, sj), 0)
    overlap_t = (ov.astype(F32) * (1.0 / CMP_LEN)).astype(BF16)
    imp = None
    for piece in _split3(p_sum):
        part = _dot_nt(overlap_t, piece)
        imp = part if imp is None else imp + part
    s_idx = lax.broadcasted_iota(jnp.int32, (n_slc, qb), 0)
    tq = t0 + lax.broadcasted_iota(jnp.int32, (n_slc, qb), 1)
    cur = tq // SLC_LEN
    forced = (s_idx == 0) | (s_idx == cur) | (s_idx == cur - 1)
    imp = jnp.where(forced, BIG, imp)
    imp = jnp.where(s_idx * SLC_LEN <= tq, imp, -BIG)
    s_idx_f = s_idx.astype(F32)
    sel_t = jnp.zeros((n_slc, qb), F32)
    for _ in range(n_sel):
        m = jnp.max(imp, axis=0, keepdims=True)
        first = jnp.min(jnp.where(imp == m, s_idx_f, float(n_slc)), axis=0, keepdims=True)
        pick = s_idx_f == first
        sel_t = jnp.where(pick, 1.0, sel_t)
        imp = jnp.where(pick, LOWEST, imp)
    selneg = ((sel_t.T - 1.0) * BIG).astype(BF16)

    q_aug = jnp.concatenate([jnp.concatenate([selneg] * p_heads, axis=0), qz16], axis=1)
    key_off = lax.broadcasted_iota(jnp.int32, (rows, kc_len), 1)

    def sel_chunk(c, carry, diagonal):
        m, acc = carry
        k0 = pl.multiple_of(c * kc_len, kc_len)
        s = _dot_nt(q_aug, ks_ref[pl.ds(k0, kc_len), :])
        if diagonal:
            s = jnp.where((k0 + key_off) <= tpos, s, NEG)
        m_new = jnp.maximum(m, jnp.max(s, axis=-1, keepdims=True))
        prob = jnp.exp((s - m_new).astype(BF16))
        return m_new, jnp.exp(m - m_new) * acc + _dot(prob, vs_ref[pl.ds(k0, kc_len), :])

    n_full = t0 // kc_len
    init = (jnp.full((rows, 1), NEG, F32), jnp.zeros((rows, LANES), F32))
    carry = lax.fori_loop(0, n_full, lambda c, cr: sel_chunk(c, cr, False), init)
    _, acc_s = sel_chunk(n_full, carry, True)
    o_s = acc_s * (1.0 / acc_s[:, DEN_LANE:DEN_LANE + 1])

    gts = _sigmoid(gate_ref[...])
    outs = []
    for p in range(p_heads):
        r = slice(p * qb, (p + 1) * qb)
        outs.append(gts[:, 3 * p:3 * p + 1] * o_c[r] + gts[:, 3 * p + 1:3 * p + 2] * o_s[r]
                    + gts[:, 3 * p + 2:3 * p + 3] * o_w[r])
    lane_o = lax.broadcasted_iota(jnp.int32, (qb, LANES), 1)
    for t in range(p_heads // 2):
        o_ref[:, t * LANES:(t + 1) * LANES] = jnp.where(lane_o < dk, outs[2 * t], pltpu.roll(outs[2 * t + 1], dk, 1))


def nsa_attention(p, gates, kvcmp, ks_aug, vs_aug, kw_aug, vw_aug):
    bsz, t, _ = p.shape
    g, dk, qb = NSA_GROUPS, NSA_DK, Q_BLOCK
    rows = NSA_HPG * qb
    w_len = qb + WINDOW
    qw = NSA_HPG * dk
    ri = jnp.arange(rows, dtype=jnp.int32)[:, None] % qb
    ki = jnp.arange(w_len, dtype=jnp.int32)[None, :]
    wbias = jnp.where((ki > ri) & (ki <= ri + WINDOW), 0.0, NEG).astype(F32)
    full = lambda a: pl.BlockSpec((None, None) + a.shape[2:], lambda b, c, j: (b, c, 0, 0))
    return pl.pallas_call(
        functools.partial(_nsa_kernel, t_len=t),
        grid=(bsz, g, t // qb),
        in_specs=[
            pl.BlockSpec((None, qb, qw), lambda b, c, j: (b, j, c)),
            pl.BlockSpec((None, None, qb, NSA_HPG * 3), lambda b, c, j: (b, c, j, 0)),
            full(kvcmp[0]), full(kvcmp[1]), full(ks_aug), full(vs_aug), full(kw_aug), full(vw_aug),
            pl.BlockSpec((rows, w_len), lambda b, c, j: (0, 0)),
        ],
        out_specs=pl.BlockSpec((None, qb, qw), lambda b, c, j: (b, j, c)),
        out_shape=jax.ShapeDtypeStruct((bsz, t, g * qw), F32),
        compiler_params=_params("arbitrary", "arbitrary", "arbitrary"),
        name="nsa_attention",
    )(p, gates, kvcmp[0], kvcmp[1], ks_aug, vs_aug, kw_aug, vw_aug, wbias)


def nsa_mixer_core(x, sc, sh, w_in, pe_k, pe_v, ck_w1, ck_w2, cv_w1, cv_w2):
    bsz, t, _ = x.shape
    g, dk = NSA_GROUPS, NSA_DK
    n_slc = t // SLC_LEN
    p = modproj(x, sc, sh, _pad_cols(w_in).astype(BF16))
    kv = p[:, :, NSA_Q:NSA_Q + 6 * NSA_KV].reshape(bsz, t, 6, g, dk).transpose(2, 0, 3, 1, 4)
    xseg = kv[0:2].reshape(2, bsz, g, t // CMP_STRIDE, CMP_STRIDE * dk)
    kvcmp = nsa_compress(xseg, jnp.stack([pe_k, pe_v]), jnp.stack([ck_w1, cv_w1]), jnp.stack([ck_w2, cv_w2]))
    kv16 = kv[2:6].astype(BF16)
    lead = (bsz, g, t)
    zeros = lambda n: jnp.zeros(lead + (n,), BF16)
    blk_onehot = (jnp.arange(t, dtype=jnp.int32)[:, None] // SLC_LEN
                  == jnp.arange(n_slc, dtype=jnp.int32)[None, :]).astype(BF16)
    ks_aug = jnp.concatenate([jnp.broadcast_to(blk_onehot, lead + (n_slc,)), kv16[0], zeros(LANES - dk)], axis=-1)
    den = jnp.ones(lead + (1,), BF16)
    vs_aug = jnp.concatenate([kv16[1], den, zeros(LANES - dk - 1)], axis=-1)
    pad_k = jnp.zeros((bsz, g, WINDOW, LANES), BF16).at[..., DEN_LANE].set(1.0)
    kw_aug = jnp.concatenate([pad_k, jnp.concatenate([kv16[2], zeros(LANES - dk)], axis=-1)], axis=2)
    vw_aug = jnp.concatenate([jnp.zeros((bsz, g, WINDOW, LANES), BF16),
                              jnp.concatenate([kv16[3], den, zeros(LANES - dk - 1)], axis=-1)], axis=2)
    gates = p[:, :, NSA_Q + 6 * NSA_KV:NSA_Q + 6 * NSA_KV + 3 * g * NSA_HPG]
    gates = gates.reshape(bsz, t, g, 3 * NSA_HPG).transpose(0, 2, 1, 3)
    return nsa_attention(p, gates, kvcmp, ks_aug, vs_aug, kw_aug, vw_aug)


def _outproj_ln_kernel(o_ref, w_ref, x_ref, gt_ref, g_ref, b_ref, sc_ref, sh_ref, wr_ref, br_ref,
                       x1_ref, hf_ref, rt_ref, cnt_ref, cnt_acc):
    @pl.when((pl.program_id(0) == 0) & (pl.program_id(1) == 0))
    def _():
        cnt_acc[...] = jnp.zeros_like(cnt_acc)

    y = _dot(o_ref[...].astype(BF16), w_ref[...])
    x1 = _layer_norm(ALPHA * x_ref[...] + (1.0 + gt_ref[...]) * y, g_ref[...], b_ref[...])
    x1_ref[...] = x1
    hf = x1 * (1.0 + sc_ref[...]) + sh_ref[...]
    hf_ref[...] = hf

    logits = _dot_f32(hf, wr_ref[...]) + br_ref[...]
    lane = lax.broadcasted_iota(jnp.int32, logits.shape, 1).astype(F32)
    is_grp = lane < N_GROUPS
    gl = jnp.where(is_grp, logits, LOWEST)
    gmax = jnp.max(gl, axis=-1, keepdims=True)
    gi = jnp.min(jnp.where(gl == gmax, lane, float(LANES)), axis=-1, keepdims=True)
    gp = 1.0 / jnp.sum(jnp.where(is_grp, jnp.exp(logits - gmax), 0.0), axis=-1, keepdims=True)
    lo = N_GROUPS + EXP_PER_GROUP * gi
    el = jnp.where((lane >= lo) & (lane < lo + EXP_PER_GROUP), logits, LOWEST)
    m1 = jnp.max(el, axis=-1, keepdims=True)
    i1 = jnp.min(jnp.where(el == m1, lane, float(LANES)), axis=-1, keepdims=True)
    el2 = jnp.where(lane == i1, LOWEST, el)
    m2 = jnp.max(el2, axis=-1, keepdims=True)
    i2 = jnp.min(jnp.where(el2 == m2, lane, float(LANES)), axis=-1, keepdims=True)
    e2 = jnp.exp(m2 - m1)
    g1 = gp / (1.0 + e2)
    g2 = gp * e2 / (1.0 + e2)

    tm = logits.shape[0]
    onehot = jnp.where((lane == i1) | (lane == i2), 1.0, 0.0)
    ri = lax.broadcasted_iota(jnp.int32, (tm, tm), 0)
    ci = lax.broadcasted_iota(jnp.int32, (tm, tm), 1)
    before = jnp.where(ci < ri, 1.0, 0.0).astype(BF16)
    seen = _dot(before, onehot.astype(BF16)) + cnt_acc[...]
    rank1 = jnp.sum(jnp.where(lane == i1, seen, 0.0), axis=-1, keepdims=True)
    rank2 = jnp.sum(jnp.where(lane == i2, seen, 0.0), axis=-1, keepdims=True)
    cnt_acc[...] = cnt_acc[...] + jnp.sum(onehot, axis=0, keepdims=True)
    cnt_ref[...] = cnt_acc[...]

    vals = (i1 - N_GROUPS, i2 - N_GROUPS, g1, g2, rank1, rank2)
    out = jnp.zeros_like(logits)
    for pos, val in enumerate(vals):
        out = jnp.where(lane == float(pos), val, out)
    rt_ref[...] = out


def outproj_ln(o, w_out, x, gt, ln_g, ln_b, sc2, sh2, w_router, b_router, tm=256):
    bsz, t, d = x.shape
    din = o.shape[-1]
    row = lambda: pl.BlockSpec((None, tm, d), lambda b, i: (b, i, 0))
    per_b = lambda: pl.BlockSpec((None, 1, d), lambda b, i: (b, 0, 0))
    const = lambda r, c: pl.BlockSpec((r, c), lambda b, i: (0, 0))
    return pl.pallas_call(
        _outproj_ln_kernel,
        grid=(bsz, t // tm),
        in_specs=[
            pl.BlockSpec((None, tm, din), lambda b, i: (b, i, 0)),
            const(din, d), row(), per_b(), const(1, d), const(1, d), per_b(), per_b(),
            const(d, LANES), const(1, LANES),
        ],
        out_specs=[row(), row(), pl.BlockSpec((None, tm, LANES), lambda b, i: (b, i, 0)), const(1, LANES)],
        out_shape=[jax.ShapeDtypeStruct((bsz, t, d), F32), jax.ShapeDtypeStruct((bsz, t, d), F32),
                   jax.ShapeDtypeStruct((bsz, t, LANES), F32), jax.ShapeDtypeStruct((1, LANES), F32)],
        scratch_shapes=[pltpu.VMEM((1, LANES), F32)],
        compiler_params=_params("arbitrary", "arbitrary"),
        name="outproj_ln",
    )(o, w_out, x, gt, ln_g.reshape(1, d), ln_b.reshape(1, d), sc2, sh2, w_router, b_router)


def _res_ln_kernel(x_ref, y_ref, gt_ref, g_ref, b_ref, o_ref):
    o_ref[...] = _layer_norm(ALPHA * x_ref[...] + (1.0 + gt_ref[...]) * y_ref[...], g_ref[...], b_ref[...])


def res_ln(x, y, gt, ln_g, ln_b, tm=512):
    bsz, t, d = x.shape
    row = lambda: pl.BlockSpec((None, tm, d), lambda b, i: (b, i, 0))
    return pl.pallas_call(
        _res_ln_kernel,
        grid=(bsz, t // tm),
        in_specs=[row(), row(), pl.BlockSpec((None, 1, d), lambda b, i: (b, 0, 0)),
                  pl.BlockSpec((1, d), lambda b, i: (0, 0)), pl.BlockSpec((1, d), lambda b, i: (0, 0))],
        out_specs=row(),
        out_shape=jax.ShapeDtypeStruct((bsz, t, d), F32),
        compiler_params=_params("arbitrary", "arbitrary"),
        name="res_ln",
    )(x, y, gt, ln_g.reshape(1, d), ln_b.reshape(1, d))


def _moe_kernel(blk_e_ref, nused_ref, x_ref, wg_ref, wu_ref, wd_ref, o_ref, wg16, wu16, wd16):
    i = pl.program_id(0)
    e = blk_e_ref[i]
    e_prev = blk_e_ref[jnp.maximum(i - 1, 0)]

    @pl.when((i == 0) | (e != e_prev))
    def _():
        wg16[...] = wg_ref[...].astype(BF16)
        wu16[...] = wu_ref[...].astype(BF16)
        wd16[...] = wd_ref[...].astype(BF16)

    @pl.when(i < nused_ref[0])
    def _():
        x = x_ref[...].astype(BF16)
        gate = _dot(x, wg16[...])
        up = _dot(x, wu16[...])
        o_ref[...] = _dot((_silu(gate) * up).astype(BF16), wd16[...])

    @pl.when(i >= nused_ref[0])
    def _():
        o_ref[...] = jnp.zeros_like(o_ref)


def moe_ffn(xg, blk_e, n_used, w_gate, w_up, w_down):
    rows, d = xg.shape
    de = w_gate.shape[-1]
    n_blk = rows // MOE_BLOCK
    grid_spec = pltpu.PrefetchScalarGridSpec(
        num_scalar_prefetch=2,
        grid=(n_blk,),
        in_specs=[
            pl.BlockSpec((MOE_BLOCK, d), lambda i, be, nu: (i, 0)),
            pl.BlockSpec((None, d, de), lambda i, be, nu: (be[i], 0, 0)),
            pl.BlockSpec((None, d, de), lambda i, be, nu: (be[i], 0, 0)),
            pl.BlockSpec((None, de, d), lambda i, be, nu: (be[i], 0, 0)),
        ],
        out_specs=pl.BlockSpec((MOE_BLOCK, d), lambda i, be, nu: (i, 0)),
        scratch_shapes=[pltpu.VMEM((d, de), BF16), pltpu.VMEM((d, de), BF16), pltpu.VMEM((de, d), BF16)],
    )
    return pl.pallas_call(
        _moe_kernel,
        grid_spec=grid_spec,
        out_shape=jax.ShapeDtypeStruct((rows, d), F32),
        compiler_params=_params("arbitrary"),
        name="moe_ffn",
    )(blk_e, n_used, xg, w_gate, w_up, w_down)


def _dispatch_kernel(ps_ref, idx_ref, x_ref, xg_in, xg_out, dest_ref, sem, *, tile):
    del xg_in

    def row_copy(r, d):
        return pltpu.make_async_copy(x_ref.at[pl.ds(r, 1)], xg_out.at[pl.ds(d, 1)], sem)

    def issue(r, carry):
        for k in range(TOP_K):
            d = ps_ref[idx_ref[0, 2 * TOP_K * r + k]] + idx_ref[0, 2 * TOP_K * r + TOP_K + k]
            dest_ref[0, TOP_K * r + k] = d
            row_copy(r, d).start()
        return carry

    lax.fori_loop(0, tile, issue, 0)

    def drain(r, carry):
        for k in range(TOP_K):
            row_copy(r, dest_ref[0, TOP_K * r + k]).wait()
        return carry

    lax.fori_loop(0, tile, drain, 0)


def moe_dispatch(hf, idx, pad_start, rows, tile=256):
    n, d = hf.shape
    n_tiles = n // tile
    grid_spec = pltpu.PrefetchScalarGridSpec(
        num_scalar_prefetch=1,
        grid=(n_tiles,),
        in_specs=[
            pl.BlockSpec((None, 1, 2 * TOP_K * tile), lambda i, ps: (i, 0, 0), memory_space=pltpu.SMEM),
            pl.BlockSpec((tile, d), lambda i, ps: (i, 0)),
            pl.BlockSpec(memory_space=pl.ANY),
        ],
        out_specs=[
            pl.BlockSpec(memory_space=pl.ANY),
            pl.BlockSpec((None, 1, TOP_K * tile), lambda i, ps: (i, 0, 0), memory_space=pltpu.SMEM),
        ],
        scratch_shapes=[pltpu.SemaphoreType.DMA(())],
    )
    xg, dest = pl.pallas_call(
        functools.partial(_dispatch_kernel, tile=tile),
        grid_spec=grid_spec,
        out_shape=[jax.ShapeDtypeStruct((rows, d), hf.dtype),
                   jax.ShapeDtypeStruct((n_tiles, 1, TOP_K * tile), jnp.int32)],
        input_output_aliases={3: 0},
        compiler_params=_params("arbitrary"),
        name="moe_dispatch",
    )(pad_start, idx.reshape(n_tiles, 1, 2 * TOP_K * tile), hf, jnp.zeros((rows, d), hf.dtype))
    return xg, dest.reshape(n * TOP_K)


def hier_moe_apply(hf, route, counts, w_gate, w_up, w_down):
    bsz, t, d = hf.shape
    n = bsz * t
    n_exp = w_gate.shape[0]
    route = route.reshape(n, LANES)
    idx = jnp.concatenate([route[:, :TOP_K], route[:, 2 * TOP_K:3 * TOP_K]], axis=1).astype(jnp.int32)
    gate = route[:, TOP_K:2 * TOP_K].reshape(-1)
    counts = counts[0, N_GROUPS:N_GROUPS + n_exp].astype(jnp.int32)
    padded = (counts + MOE_BLOCK - 1) // MOE_BLOCK * MOE_BLOCK
    pad_end = jnp.cumsum(padded)
    pad_start = (pad_end - padded).astype(jnp.int32)
    n_blk = -(-(n * TOP_K) // MOE_BLOCK) + n_exp
    blk_first = jnp.arange(n_blk, dtype=jnp.int32) * MOE_BLOCK
    blk_e = jnp.minimum(jnp.sum((pad_end[None, :] <= blk_first[:, None]).astype(jnp.int32), axis=1), n_exp - 1)
    n_used = (pad_end[-1:] // MOE_BLOCK).astype(jnp.int32)
    xg, dest = moe_dispatch(hf.reshape(n, d), idx, pad_start, n_blk * MOE_BLOCK)
    yb = moe_ffn(xg, blk_e.astype(jnp.int32), n_used, w_gate, w_up, w_down)
    contrib = yb[dest] * gate[:, None]
    return jnp.sum(contrib.reshape(n, TOP_K, d), axis=1).reshape(bsz, t, d)


def _pad_cols(w, mult=LANES):
    pad = (-w.shape[-1]) % mult
    return jnp.pad(w, ((0, 0), (0, pad))) if pad else w


def kernel(x, c, ada_w, ada_b, ln1_g, ln1_b, ln2_g, ln2_b, gdn_w_in, gdn_conv_w, gdn_a_log, gdn_dt_bias, gdn_norm_g, gdn_w_out, lru_w_in, lru_conv_w, lru_conv_b, lru_w_a, lru_b_a, lru_w_x, lru_b_x, lru_lambda, lru_w_out, nsa_w_in, nsa_pe_k, nsa_pe_v, nsa_ck_w1, nsa_ck_w2, nsa_cv_w1, nsa_cv_w2, nsa_w_out, moe_w_grp, moe_b_grp, moe_w_exp, moe_b_exp, moe_w_gate, moe_w_up, moe_w_down):
    bsz, t, d = x.shape
    depth = ada_w.shape[0]
    mod = ada_modulation(c, ada_w, ada_b).reshape(depth, bsz, 6, 1, d)
    ja = jb = jc = 0
    for i in range(depth):
        sh1, sc1, gt1, sh2, sc2, gt2 = (mod[i, :, k] for k in range(6))
        kind = i % N_MIXERS
        if kind == 0:
            p = modproj(x, sc1, sh1, _pad_cols(gdn_w_in[ja]).astype(BF16))
            o = gdn_core(p, gdn_conv_w[ja], gdn_a_log[ja], gdn_dt_bias[ja], gdn_norm_g[ja])
            w_out = gdn_w_out[ja]
            ja += 1
        elif kind == 1:
            p = modproj(x, sc1, sh1, lru_w_in[jb].astype(BF16))
            o = lru_core(p, lru_conv_w[jb], lru_conv_b[jb], lru_w_a[jb], lru_b_a[jb], lru_w_x[jb], lru_b_x[jb],
                         lru_lambda[jb])
            w_out = lru_w_out[jb]
            jb += 1
        else:
            o = nsa_mixer_core(x, sc1, sh1, nsa_w_in[jc], nsa_pe_k[jc], nsa_pe_v[jc], nsa_ck_w1[jc], nsa_ck_w2[jc],
                               nsa_cv_w1[jc], nsa_cv_w2[jc])
            w_out = nsa_w_out[jc]
            jc += 1
        w_router = _pad_cols(jnp.concatenate([moe_w_grp[i], moe_w_exp[i]], axis=1))
        b_router = _pad_cols(jnp.concatenate([moe_b_grp[i], moe_b_exp[i]])[None, :])
        x1, hf, route, counts = outproj_ln(o, w_out.astype(BF16), x, gt1, ln1_g[i], ln1_b[i], sc2, sh2,
                                           w_router, b_router)
        y = hier_moe_apply(hf, route, counts, moe_w_gate[i], moe_w_up[i], moe_w_down[i])
        x = res_ln(x1, y, gt2, ln2_g[i], ln2_b[i])
    return x
```

```python
import functools

import jax
import jax.numpy as jnp
from jax import lax
from jax.experimental import pallas as pl
from jax.experimental.pallas import tpu as pltpu

F32 = jnp.float32
BF16 = jnp.bfloat16

D_MODEL = 1024
DEPTH = 4
N_MIXERS = 3
LN_EPS = 1e-5
NORM_EPS = 1e-6
CONV_K = 4
ALPHA = (2.0 * DEPTH) ** 0.25
GDN_HEADS = 8
GDN_DK = 128
GDN_CHUNK = 64
GDN_QKV = 3 * GDN_HEADS * GDN_DK
LRU_WIDTH = 1024
LRU_BLOCKS = 16
LRU_BW = LRU_WIDTH // LRU_BLOCKS
RG_C = 8.0
NSA_GROUPS = 4
NSA_HPG = 4
NSA_DK = 64
NSA_Q = NSA_GROUPS * NSA_HPG * NSA_DK
NSA_KV = NSA_GROUPS * NSA_DK
CMP_LEN = 32
CMP_STRIDE = 16
SLC_LEN = 64
N_SEL = 16
WINDOW = 512
Q_BLOCK = 128
SEL_KEY_CHUNK = 512
N_GROUPS = 4
EXP_PER_GROUP = 8
N_EXPERTS = N_GROUPS * EXP_PER_GROUP
TOP_K = 2
MOE_BLOCK = 256
NEG = -1e30
BIG = 1e30
LOWEST = -3e38

LANES = 128
SUBLANES = 8
VMEM_LIMIT_BYTES = 48 * 1024 * 1024


def _params(*sem):
    return pltpu.CompilerParams(dimension_semantics=sem, vmem_limit_bytes=VMEM_LIMIT_BYTES)


def _sigmoid(x):
    return 1.0 / (1.0 + jnp.exp(-x))


def _silu(x):
    return x * _sigmoid(x)


def _softplus(x):
    return jnp.maximum(x, 0.0) + jnp.log(1.0 + jnp.exp(-jnp.abs(x)))


def _gelu(x):
    return 0.5 * x * (1.0 + jnp.tanh(0.7978845608028654 * (x + 0.044715 * (x * x * x))))


def _dot(a, b):
    return jnp.dot(a, b, preferred_element_type=F32)


def _dot_nt(a, b):
    return lax.dot_general(a, b, (((1,), (1,)), ((), ())), preferred_element_type=F32)


def _dot_tn(a, b):
    return lax.dot_general(a, b, (((0,), (0,)), ((), ())), preferred_element_type=F32)


def _split3(x):
    hi = x.astype(BF16)
    r1 = x - hi.astype(F32)
    mid = r1.astype(BF16)
    lo = (r1 - mid.astype(F32)).astype(BF16)
    return hi, mid, lo


def _dot_lhs_exact(m_bf16, x):
    hi, mid, lo = _split3(x)
    return _dot(m_bf16, hi) + _dot(m_bf16, mid) + _dot(m_bf16, lo)


def _dot_rhs_exact(x, m_bf16):
    hi, mid, lo = _split3(x)
    return _dot(hi, m_bf16) + _dot(mid, m_bf16) + _dot(lo, m_bf16)


def _dot_f32(a, b):
    ah, am, al = _split3(a)
    bh, bm, bl = _split3(b)
    return (_dot(ah, bh) + _dot(ah, bm) + _dot(am, bh)) + (_dot(ah, bl) + _dot(al, bh) + _dot(am, bm))


def _layer_norm(z, g, b):
    mu = jnp.mean(z, axis=-1, keepdims=True)
    zc = z - mu
    var = jnp.mean(zc * zc, axis=-1, keepdims=True)
    return zc * lax.rsqrt(var + LN_EPS) * g + b


def _causal_conv(xbuf, cw_ref, col0, width, tb):
    cols = slice(col0, col0 + width)
    acc = cw_ref[0:1, cols] * xbuf[SUBLANES - 3:SUBLANES - 3 + tb, cols]
    for k in range(1, CONV_K):
        off = SUBLANES - 3 + k
        acc = acc + cw_ref[k:k + 1, cols] * xbuf[off:off + tb, cols]
    return acc


def _ada_kernel(c_ref, w_ref, b_ref, o_ref):
    cond = _silu(c_ref[...])
    o_ref[...] = _dot_f32(cond, w_ref[...]) + b_ref[...]


def ada_modulation(c, ada_w, ada_b, col_tile=1536):
    depth, d, n6 = ada_w.shape
    bsz = c.shape[0]
    return pl.pallas_call(
        _ada_kernel,
        grid=(depth, n6 // col_tile),
        in_specs=[
            pl.BlockSpec((bsz, d), lambda i, j: (0, 0)),
            pl.BlockSpec((None, d, col_tile), lambda i, j: (i, 0, j)),
            pl.BlockSpec((None, 1, col_tile), lambda i, j: (i, 0, j)),
        ],
        out_specs=pl.BlockSpec((None, bsz, col_tile), lambda i, j: (i, 0, j)),
        out_shape=jax.ShapeDtypeStruct((depth, bsz, n6), F32),
        compiler_params=_params("arbitrary", "arbitrary"),
        name="ada_modulation",
    )(c, ada_w, ada_b.reshape(depth, 1, n6))


def _modproj_kernel(x_ref, sc_ref, sh_ref, w_ref, o_ref, *, col_chunk):
    h = (x_ref[...] * (1.0 + sc_ref[...]) + sh_ref[...]).astype(BF16)
    ncols = o_ref.shape[-1]
    for c0 in range(0, ncols, col_chunk):
        c1 = min(c0 + col_chunk, ncols)
        o_ref[:, c0:c1] = _dot(h, w_ref[:, c0:c1])


def modproj(x, sc, sh, w, tm=256, col_chunk=1024):
    bsz, t, d = x.shape
    ncols = w.shape[1]
    return pl.pallas_call(
        functools.partial(_modproj_kernel, col_chunk=col_chunk),
        grid=(bsz, t // tm),
        in_specs=[
            pl.BlockSpec((None, tm, d), lambda b, i: (b, i, 0)),
            pl.BlockSpec((None, 1, d), lambda b, i: (b, 0, 0)),
            pl.BlockSpec((None, 1, d), lambda b, i: (b, 0, 0)),
            pl.BlockSpec((d, ncols), lambda b, i: (0, 0)),
        ],
        out_specs=pl.BlockSpec((None, tm, ncols), lambda b, i: (b, i, 0)),
        out_shape=jax.ShapeDtypeStruct((bsz, t, ncols), F32),
        compiler_params=_params("arbitrary", "arbitrary"),
        name="modproj",
    )(x, sc, sh, w)


def _gdn_kernel(qkv_ref, z_ref, ab_ref, cw_ref, alog_ref, dtb_ref, ng_ref, o_ref, xbuf, s_ref, *, tb):
    nh, dk, chunk = GDN_HEADS, GDN_DK, GDN_CHUNK

    @pl.when(pl.program_id(1) == 0)
    def _():
        xbuf[0:SUBLANES, :] = jnp.zeros((SUBLANES, GDN_QKV), F32)
        s_ref[...] = jnp.zeros_like(s_ref)

    xbuf[SUBLANES:SUBLANES + tb, :] = qkv_ref[...]

    ab = ab_ref[...]
    g_all = -jnp.exp(alog_ref[...]) * _softplus(ab + dtb_ref[...])
    beta_all = _sigmoid(ab)
    ri = lax.broadcasted_iota(jnp.int32, (tb, tb), 0)
    ci = lax.broadcasted_iota(jnp.int32, (tb, tb), 1)
    same = (ri // chunk) == (ci // chunk)
    causal = same & (ci <= ri)
    strict = same & (ci < ri)
    eye = jnp.where(ri == ci, 1.0, 0.0)
    tri = jnp.where(causal, 1.0, 0.0).astype(BF16)
    ones_blk = jnp.where(same, 1.0, 0.0).astype(BF16)
    gc_all = _dot_lhs_exact(tri, g_all)
    glast_all = _dot_lhs_exact(ones_blk, g_all)
    gc_t = gc_all.T
    egc_all = jnp.exp(gc_all)
    kdec_all = jnp.exp(glast_all - gc_all)

    heads = range(nh)
    lmats, rhss, qk16s, qg16s, ks16s, gls = [], [], [], [], [], []
    for h in heads:
        q = _silu(_causal_conv(xbuf, cw_ref, h * dk, dk, tb))
        k = _silu(_causal_conv(xbuf, cw_ref, (nh + h) * dk, dk, tb))
        v = _silu(_causal_conv(xbuf, cw_ref, (2 * nh + h) * dk, dk, tb))
        q = q * lax.rsqrt(jnp.sum(q * q, axis=-1, keepdims=True) + NORM_EPS)
        k = k * lax.rsqrt(jnp.sum(k * k, axis=-1, keepdims=True) + NORM_EPS)
        beta = beta_all[:, nh + h:nh + h + 1]
        eg = egc_all[:, h:h + 1]
        decay = jnp.exp(jnp.where(causal, gc_all[:, h:h + 1] - gc_t[h:h + 1, :], NEG))
        kb = k * beta
        k16 = k.astype(BF16)
        qs = q * (dk ** -0.5)
        lmats.append(jnp.where(strict, _dot_nt(kb.astype(BF16), k16) * decay, 0.0))
        rhss.append(jnp.concatenate([v * beta, kb * eg], axis=1).astype(BF16))
        qk16s.append((_dot_nt(qs.astype(BF16), k16) * decay).astype(BF16))
        qg16s.append((qs * eg).astype(BF16))
        ks16s.append((k * kdec_all[:, h:h + 1]).astype(BF16))
        gls.append(glast_all[:, h:h + 1])
    xbuf[0:SUBLANES, :] = xbuf[tb:tb + SUBLANES, :]

    xpows = lmats
    tinvs = [eye - lm for lm in lmats]
    for _ in range(5):
        xpows = [_dot(x16, x16) for x16 in [x.astype(BF16) for x in xpows]]
        tinvs = [t + _dot(t.astype(BF16), x.astype(BF16)) for t, x in zip(tinvs, xpows)]
    uws = [_dot(t.astype(BF16), rhs) for t, rhs in zip(tinvs, rhss)]
    us = [uw[:, :dk] for uw in uws]
    w16s = [uw[:, dk:].astype(BF16) for uw in uws]

    states = [s_ref[h] for h in heads]
    o_parts = [[] for _ in heads]
    vn_parts = [[] for _ in heads]
    for c in range(tb // chunk):
        rows = slice(c * chunk, (c + 1) * chunk)
        for h in heads:
            s16 = states[h].astype(BF16)
            vn16 = (us[h][rows] - _dot(w16s[h][rows], s16)).astype(BF16)
            o_parts[h].append(_dot(qg16s[h][rows], s16))
            states[h] = (states[h] * jnp.exp(gls[h][c * chunk:c * chunk + 1, :])
                         + _dot_tn(ks16s[h][rows], vn16))
            vn_parts[h].append(vn16)
    for h in heads:
        s_ref[h] = states[h]
        o = jnp.concatenate(o_parts[h], axis=0) + _dot(qk16s[h], jnp.concatenate(vn_parts[h], axis=0))
        o = o * lax.rsqrt(jnp.mean(o * o, axis=-1, keepdims=True) + NORM_EPS) * ng_ref[...]
        o_ref[:, h * dk:(h + 1) * dk] = o * _silu(z_ref[:, h * dk:(h + 1) * dk])


def gdn_core(p, conv_w, a_log, dt_bias, norm_g, tb=256):
    bsz, t, _ = p.shape
    nh, dk = GDN_HEADS, GDN_DK
    v_w = nh * dk
    alog = jnp.zeros((1, LANES), F32).at[0, :nh].set(a_log)
    dtb = jnp.zeros((1, LANES), F32).at[0, :nh].set(dt_bias)
    ab_blk = (GDN_QKV + v_w) // LANES
    return pl.pallas_call(
        functools.partial(_gdn_kernel, tb=tb),
        grid=(bsz, t // tb),
        in_specs=[
            pl.BlockSpec((None, tb, GDN_QKV), lambda b, i: (b, i, 0)),
            pl.BlockSpec((None, tb, v_w), lambda b, i: (b, i, GDN_QKV // v_w)),
            pl.BlockSpec((None, tb, LANES), lambda b, i: (b, i, ab_blk)),
            pl.BlockSpec((CONV_K, GDN_QKV), lambda b, i: (0, 0)),
            pl.BlockSpec((1, LANES), lambda b, i: (0, 0)),
            pl.BlockSpec((1, LANES), lambda b, i: (0, 0)),
            pl.BlockSpec((1, dk), lambda b, i: (0, 0)),
        ],
        out_specs=pl.BlockSpec((None, tb, v_w), lambda b, i: (b, i, 0)),
        out_shape=jax.ShapeDtypeStruct((bsz, t, v_w), F32),
        scratch_shapes=[pltpu.VMEM((tb + SUBLANES, GDN_QKV), F32), pltpu.VMEM((nh, dk, dk), F32)],
        compiler_params=_params("arbitrary", "arbitrary"),
        name="gdn_core",
    )(p, p, p, conv_w, alog, dtb, norm_g.reshape(1, dk))


def _lru_kernel(gate_ref, xb_ref, cw_ref, cb_ref, wax_ref, ba_ref, bx_ref, lam_ref, o_ref, xbuf, h_ref, *, tb):
    width = LRU_WIDTH
    grp = 2 * LANES

    @pl.when(pl.program_id(1) == 0)
    def _():
        xbuf[0:SUBLANES, :] = jnp.zeros((SUBLANES, width), F32)
        h_ref[...] = jnp.zeros_like(h_ref)

    xbuf[SUBLANES:SUBLANES + tb, :] = xb_ref[...]
    row = lax.broadcasted_iota(jnp.int32, (tb, grp), 0)

    for j in range(width // grp):
        cols = slice(j * grp, (j + 1) * grp)
        xc = _causal_conv(xbuf, cw_ref, j * grp, grp, tb) + cb_ref[:, cols]
        res = _dot(xc.astype(BF16), wax_ref[j])
        r = _sigmoid(res[:, :grp] + ba_ref[:, cols])
        ig = _sigmoid(res[:, grp:] + bx_ref[:, cols])
        log_a = -RG_C * r * _softplus(-lam_ref[:, cols])
        a = jnp.exp(log_a)
        mult = jnp.sqrt(jnp.maximum(1.0 - jnp.exp(2.0 * log_a), 0.0))
        u = mult * ig * xc
        s = 1
        while s < tb:
            a_sh = jnp.where(row < s, 1.0, pltpu.roll(a, s, 0))
            u_sh = jnp.where(row < s, 0.0, pltpu.roll(u, s, 0))
            u = a * u_sh + u
            a = a * a_sh
            s *= 2
        hs = u + a * h_ref[:, cols]
        h_ref[:, cols] = hs[tb - 1:tb, :]
        o_ref[:, cols] = hs * _gelu(gate_ref[:, cols])

    xbuf[0:SUBLANES, :] = xbuf[tb:tb + SUBLANES, :]


def _block_diag_groups(w):
    per = (2 * LANES) // LRU_BW
    wg = w.reshape(LRU_BLOCKS // per, per, LRU_BW, LRU_BW)
    eye = jnp.eye(per, dtype=w.dtype)
    return jnp.einsum('gaij,ab->gaibj', wg, eye).reshape(LRU_BLOCKS // per, per * LRU_BW, per * LRU_BW)


def lru_core(p, conv_w, conv_b, w_a, b_a, w_x, b_x, lam, tb=256):
    bsz, t, _ = p.shape
    width = LRU_WIDTH
    grp = 2 * LANES
    wax = jnp.concatenate([_block_diag_groups(w_a), _block_diag_groups(w_x)], axis=-1).astype(BF16)
    vec = lambda a: a.reshape(1, width)
    return pl.pallas_call(
        functools.partial(_lru_kernel, tb=tb),
        grid=(bsz, t // tb),
        in_specs=[
            pl.BlockSpec((None, tb, width), lambda b, i: (b, i, 0)),
            pl.BlockSpec((None, tb, width), lambda b, i: (b, i, 1)),
            pl.BlockSpec((CONV_K, width), lambda b, i: (0, 0)),
            pl.BlockSpec((1, width), lambda b, i: (0, 0)),
            pl.BlockSpec((width // grp, grp, 2 * grp), lambda b, i: (0, 0, 0)),
            pl.BlockSpec((1, width), lambda b, i: (0, 0)),
            pl.BlockSpec((1, width), lambda b, i: (0, 0)),
            pl.BlockSpec((1, width), lambda b, i: (0, 0)),
        ],
        out_specs=pl.BlockSpec((None, tb, width), lambda b, i: (b, i, 0)),
        out_shape=jax.ShapeDtypeStruct((bsz, t, width), F32),
        scratch_shapes=[pltpu.VMEM((tb + SUBLANES, width), F32), pltpu.VMEM((1, width), F32)],
        compiler_params=_params("arbitrary", "arbitrary"),
        name="lru_core",
    )(p, p, conv_w, vec(conv_b), wax, vec(b_a), vec(b_x), vec(lam))


def _cmp_kernel(x_ref, pelo_ref, pehi_ref, w1a_ref, w1b_ref, w2_ref, o_ref):
    x = x_ref[...]
    nseg = x.shape[0]
    first = _dot((x + pelo_ref[...]).astype(BF16), w1a_ref[...])
    second = _dot((x + pehi_ref[...]).astype(BF16), w1b_ref[...])
    hid = _gelu(first + pltpu.roll(second, nseg - 1, 0))
    o_ref[...] = _dot(hid.astype(BF16), w2_ref[...]).astype(o_ref.dtype)


def nsa_compress(xseg, pe, w1, w2):
    _, bsz, g, nseg, seg_w = xseg.shape
    hid = w1.shape[-1]
    pelo = pe[:, :CMP_STRIDE].reshape(2, 1, seg_w)
    pehi = pe[:, CMP_STRIDE:].reshape(2, 1, seg_w)
    w1a = w1[:, :seg_w].astype(BF16)
    w1b = w1[:, seg_w:].astype(BF16)
    w2p = jnp.pad(w2, ((0, 0), (0, 0), (0, LANES - NSA_DK))).astype(BF16)
    return pl.pallas_call(
        _cmp_kernel,
        grid=(2, bsz, g),
        in_specs=[
            pl.BlockSpec((None, None, None, nseg, seg_w), lambda a, b, c: (a, b, c, 0, 0)),
            pl.BlockSpec((None, 1, seg_w), lambda a, b, c: (a, 0, 0)),
            pl.BlockSpec((None, 1, seg_w), lambda a, b, c: (a, 0, 0)),
            pl.BlockSpec((None, seg_w, hid), lambda a, b, c: (a, 0, 0)),
            pl.BlockSpec((None, seg_w, hid), lambda a, b, c: (a, 0, 0)),
            pl.BlockSpec((None, hid, LANES), lambda a, b, c: (a, 0, 0)),
        ],
        out_specs=pl.BlockSpec((None, None, None, nseg, LANES), lambda a, b, c: (a, b, c, 0, 0)),
        out_shape=jax.ShapeDtypeStruct((2, bsz, g, nseg, LANES), BF16),
        compiler_params=_params("arbitrary", "arbitrary", "arbitrary"),
        name="nsa_compress",
    )(xseg, pelo, pehi, w1a, w1b, w2p)


DEN_LANE = NSA_DK


def _nsa_kernel(q_ref, gate_ref, kc_ref, vc_ref, ks_ref, vs_ref, kw_ref, vw_ref, wbias_ref, o_ref, *, t_len):
    p_heads, dk, qb = NSA_HPG, NSA_DK, Q_BLOCK
    rows = p_heads * qb
    n_cmp_rows = kc_ref.shape[0]
    n_slc = t_len // SLC_LEN
    n_sel = min(N_SEL, n_slc)
    kc_len = SEL_KEY_CHUNK
    t0 = pl.program_id(2) * qb

    qg = q_ref[...]
    halves = []
    for p in range(p_heads):
        tile = qg[:, (p // 2) * LANES:(p // 2 + 1) * LANES]
        halves.append(tile if p % 2 == 0 else pltpu.roll(tile, dk, 1))
    lane = lax.broadcasted_iota(jnp.int32, (rows, LANES), 1)
    qz = jnp.where(lane < dk, jnp.concatenate(halves, axis=0) * (dk ** -0.5), 0.0)
    qz16 = qz.astype(BF16)
    tpos = t0 + (lax.broadcasted_iota(jnp.int32, (rows, 1), 0) % qb)

    w_len = qb + WINDOW
    w0 = pl.multiple_of(t0, qb)
    qw16 = jnp.where(lane == DEN_LANE, NEG, qz).astype(BF16)
    s_w = _dot_nt(qw16, kw_ref[pl.ds(w0, w_len), :]) + wbias_ref[...]
    e_w = jnp.exp(s_w - jnp.max(s_w, axis=-1, keepdims=True))
    acc_w = _dot(e_w.astype(BF16), vw_ref[pl.ds(w0, w_len), :])
    o_w = acc_w * (1.0 / acc_w[:, DEN_LANE:DEN_LANE + 1])

    s_c = _dot_nt(qz16, kc_ref[...])
    n_idx = lax.broadcasted_iota(jnp.int32, (rows, n_cmp_rows), 1)
    vis = (n_idx * CMP_STRIDE + (CMP_LEN - 1)) <= tpos
    s_c = jnp.where(vis, s_c, NEG)
    e_c = jnp.where(vis, jnp.exp(s_c - jnp.max(s_c, axis=-1, keepdims=True)), 0.0)
    l_c = jnp.sum(e_c, axis=-1, keepdims=True)
    p_c = e_c * (1.0 / jnp.where(l_c > 0.0, l_c, 1.0))
    o_c = _dot(p_c.astype(BF16), vc_ref[...])

    p_sum = p_c[0:qb]
    for p in range(1, p_heads):
        p_sum = p_sum + p_c[p * qb:(p + 1) * qb]
    sj = lax.broadcasted_iota(jnp.int32, (n_slc, n_cmp_rows), 0) * SLC_LEN
    cn = lax.broadcasted_iota(jnp.int32, (n_slc, n_cmp_rows), 1) * CMP_STRIDE
    ov = jnp.maximum(jnp.minimum(cn + CMP_LEN, sj + SLC_LEN) - jnp.maximum(cn, sj), 0)
    overlap_t = (ov.astype(F32) * (1.0 / CMP_LEN)).astype(BF16)
    imp = None
    for piece in _split3(p_sum):
        part = _dot_nt(overlap_t, piece)
        imp = part if imp is None else imp + part
    s_idx = lax.broadcasted_iota(jnp.int32, (n_slc, qb), 0)
    tq = t0 + lax.broadcasted_iota(jnp.int32, (n_slc, qb), 1)
    cur = tq // SLC_LEN
    forced = (s_idx == 0) | (s_idx == cur) | (s_idx == cur - 1)
    imp = jnp.where(forced, BIG, imp)
    imp = jnp.where(s_idx * SLC_LEN <= tq, imp, -BIG)
    s_idx_f = s_idx.astype(F32)
    sel_t = jnp.zeros((n_slc, qb), F32)
    for _ in range(n_sel):
        m = jnp.max(imp, axis=0, keepdims=True)
        first = jnp.min(jnp.where(imp == m, s_idx_f, float(n_slc)), axis=0, keepdims=True)
        pick = s_idx_f == first
        sel_t = jnp.where(pick, 1.0, sel_t)
        imp = jnp.where(pick, LOWEST, imp)
    selneg = ((sel_t.T - 1.0) * BIG).astype(BF16)

    q_aug = jnp.concatenate([jnp.concatenate([selneg] * p_heads, axis=0), qz16], axis=1)
    key_off = lax.broadcasted_iota(jnp.int32, (rows, kc_len), 1)

    def sel_chunk(c, carry, diagonal):
        m, acc = carry
        k0 = pl.multiple_of(c * kc_len, kc_len)
        s = _dot_nt(q_aug, ks_ref[pl.ds(k0, kc_len), :])
        if diagonal:
            s = jnp.where((k0 + key_off) <= tpos, s, NEG)
        m_new = jnp.maximum(m, jnp.max(s, axis=-1, keepdims=True))
        prob = jnp.exp((s - m_new).astype(BF16))
        return m_new, jnp.exp(m - m_new) * acc + _dot(prob, vs_ref[pl.ds(k0, kc_len), :])

    n_full = t0 // kc_len
    init = (jnp.full((rows, 1), NEG, F32), jnp.zeros((rows, LANES), F32))
    carry = lax.fori_loop(0, n_full, lambda c, cr: sel_chunk(c, cr, False), init)
    _, acc_s = sel_chunk(n_full, carry, True)
    o_s = acc_s * (1.0 / acc_s[:, DEN_LANE:DEN_LANE + 1])

    gts = _sigmoid(gate_ref[...])
    outs = []
    for p in range(p_heads):
        r = slice(p * qb, (p + 1) * qb)
        outs.append(gts[:, 3 * p:3 * p + 1] * o_c[r] + gts[:, 3 * p + 1:3 * p + 2] * o_s[r]
                    + gts[:, 3 * p + 2:3 * p + 3] * o_w[r])
    lane_o = lax.broadcasted_iota(jnp.int32, (qb, LANES), 1)
    for t in range(p_heads // 2):
        o_ref[:, t * LANES:(t + 1) * LANES] = jnp.where(lane_o < dk, outs[2 * t], pltpu.roll(outs[2 * t + 1], dk, 1))


def nsa_attention(p, gates, kvcmp, ks_aug, vs_aug, kw_aug, vw_aug):
    bsz, t, _ = p.shape
    g, dk, qb = NSA_GROUPS, NSA_DK, Q_BLOCK
    rows = NSA_HPG * qb
    w_len = qb + WINDOW
    qw = NSA_HPG * dk
    ri = jnp.arange(rows, dtype=jnp.int32)[:, None] % qb
    ki = jnp.arange(w_len, dtype=jnp.int32)[None, :]
    wbias = jnp.where((ki > ri) & (ki <= ri + WINDOW), 0.0, NEG).astype(F32)
    full = lambda a: pl.BlockSpec((None, None) + a.shape[2:], lambda b, c, j: (b, c, 0, 0))
    return pl.pallas_call(
        functools.partial(_nsa_kernel, t_len=t),
        grid=(bsz, g, t // qb),
        in_specs=[
            pl.BlockSpec((None, qb, qw), lambda b, c, j: (b, j, c)),
            pl.BlockSpec((None, None, qb, NSA_HPG * 3), lambda b, c, j: (b, c, j, 0)),
            full(kvcmp[0]), full(kvcmp[1]), full(ks_aug), full(vs_aug), full(kw_aug), full(vw_aug),
            pl.BlockSpec((rows, w_len), lambda b, c, j: (0, 0)),
        ],
        out_specs=pl.BlockSpec((None, qb, qw), lambda b, c, j: (b, j, c)),
        out_shape=jax.ShapeDtypeStruct((bsz, t, g * qw), F32),
        compiler_params=_params("arbitrary", "arbitrary", "arbitrary"),
        name="nsa_attention",
    )(p, gates, kvcmp[0], kvcmp[1], ks_aug, vs_aug, kw_aug, vw_aug, wbias)


def nsa_mixer_core(x, sc, sh, w_in, pe_k, pe_v, ck_w1, ck_w2, cv_w1, cv_w2):
    bsz, t, _ = x.shape
    g, dk = NSA_GROUPS, NSA_DK
    n_slc = t // SLC_LEN
    p = modproj(x, sc, sh, _pad_cols(w_in).astype(BF16))
    kv = p[:, :, NSA_Q:NSA_Q + 6 * NSA_KV].reshape(bsz, t, 6, g, dk).transpose(2, 0, 3, 1, 4)
    xseg = kv[0:2].reshape(2, bsz, g, t // CMP_STRIDE, CMP_STRIDE * dk)
    kvcmp = nsa_compress(xseg, jnp.stack([pe_k, pe_v]), jnp.stack([ck_w1, cv_w1]), jnp.stack([ck_w2, cv_w2]))
    kv16 = kv[2:6].astype(BF16)
    lead = (bsz, g, t)
    zeros = lambda n: jnp.zeros(lead + (n,), BF16)
    blk_onehot = (jnp.arange(t, dtype=jnp.int32)[:, None] // SLC_LEN
                  == jnp.arange(n_slc, dtype=jnp.int32)[None, :]).astype(BF16)
    ks_aug = jnp.concatenate([jnp.broadcast_to(blk_onehot, lead + (n_slc,)), kv16[0], zeros(LANES - dk)], axis=-1)
    den = jnp.ones(lead + (1,), BF16)
    vs_aug = jnp.concatenate([kv16[1], den, zeros(LANES - dk - 1)], axis=-1)
    pad_k = jnp.zeros((bsz, g, WINDOW, LANES), BF16).at[..., DEN_LANE].set(1.0)
    kw_aug = jnp.concatenate([pad_k, jnp.concatenate([kv16[2], zeros(LANES - dk)], axis=-1)], axis=2)
    vw_aug = jnp.concatenate([jnp.zeros((bsz, g, WINDOW, LANES), BF16),
                              jnp.concatenate([kv16[3], den, zeros(LANES - dk - 1)], axis=-1)], axis=2)
    gates = p[:, :, NSA_Q + 6 * NSA_KV:NSA_Q + 6 * NSA_KV + 3 * g * NSA_HPG]
    gates = gates.reshape(bsz, t, g, 3 * NSA_HPG).transpose(0, 2, 1, 3)
    return nsa_attention(p, gates, kvcmp, ks_aug, vs_aug, kw_aug, vw_aug)


def _outproj_ln_kernel(o_ref, w_ref, x_ref, gt_ref, g_ref, b_ref, sc_ref, sh_ref, wr_ref, br_ref,
                       x1_ref, hf_ref, rt_ref, idx_ref, cnt_ref, cnt_acc):
    @pl.when((pl.program_id(0) == 0) & (pl.program_id(1) == 0))
    def _():
        cnt_acc[...] = jnp.zeros_like(cnt_acc)

    y = _dot(o_ref[...].astype(BF16), w_ref[...])
    x1 = _layer_norm(ALPHA * x_ref[...] + (1.0 + gt_ref[...]) * y, g_ref[...], b_ref[...])
    x1_ref[...] = x1
    hf = x1 * (1.0 + sc_ref[...]) + sh_ref[...]
    hf_ref[...] = hf

    logits = _dot_f32(hf, wr_ref[...]) + br_ref[...]
    lane = lax.broadcasted_iota(jnp.int32, logits.shape, 1).astype(F32)
    is_grp = lane < N_GROUPS
    gl = jnp.where(is_grp, logits, LOWEST)
    gmax = jnp.max(gl, axis=-1, keepdims=True)
    gi = jnp.min(jnp.where(gl == gmax, lane, float(LANES)), axis=-1, keepdims=True)
    gp = 1.0 / jnp.sum(jnp.where(is_grp, jnp.exp(logits - gmax), 0.0), axis=-1, keepdims=True)
    lo = N_GROUPS + EXP_PER_GROUP * gi
    el = jnp.where((lane >= lo) & (lane < lo + EXP_PER_GROUP), logits, LOWEST)
    m1 = jnp.max(el, axis=-1, keepdims=True)
    i1 = jnp.min(jnp.where(el == m1, lane, float(LANES)), axis=-1, keepdims=True)
    el2 = jnp.where(lane == i1, LOWEST, el)
    m2 = jnp.max(el2, axis=-1, keepdims=True)
    i2 = jnp.min(jnp.where(el2 == m2, lane, float(LANES)), axis=-1, keepdims=True)
    e2 = jnp.exp(m2 - m1)
    g1 = gp / (1.0 + e2)
    g2 = gp * e2 / (1.0 + e2)

    tm = logits.shape[0]
    onehot = jnp.where((lane == i1) | (lane == i2), 1.0, 0.0)
    ri = lax.broadcasted_iota(jnp.int32, (tm, tm), 0)
    ci = lax.broadcasted_iota(jnp.int32, (tm, tm), 1)
    before = jnp.where(ci < ri, 1.0, 0.0).astype(BF16)
    seen = _dot(before, onehot.astype(BF16)) + cnt_acc[...]
    rank1 = jnp.sum(jnp.where(lane == i1, seen, 0.0), axis=-1, keepdims=True)
    rank2 = jnp.sum(jnp.where(lane == i2, seen, 0.0), axis=-1, keepdims=True)
    cnt_acc[...] = cnt_acc[...] + jnp.sum(onehot, axis=0, keepdims=True)
    cnt_ref[...] = cnt_acc[...]

    vals = (i1 - N_GROUPS, i2 - N_GROUPS, g1, g2, rank1, rank2)
    out = jnp.zeros_like(logits)
    for pos, val in enumerate(vals):
        out = jnp.where(lane == float(pos), val, out)
    rt_ref[...] = out
    idx_ref[...] = out.T[0:SUBLANES, :].astype(jnp.int32)


def outproj_ln(o, w_out, x, gt, ln_g, ln_b, sc2, sh2, w_router, b_router, tm=256):
    bsz, t, d = x.shape
    din = o.shape[-1]
    row = lambda: pl.BlockSpec((None, tm, d), lambda b, i: (b, i, 0))
    per_b = lambda: pl.BlockSpec((None, 1, d), lambda b, i: (b, 0, 0))
    const = lambda r, c: pl.BlockSpec((r, c), lambda b, i: (0, 0))
    return pl.pallas_call(
        _outproj_ln_kernel,
        grid=(bsz, t // tm),
        in_specs=[
            pl.BlockSpec((None, tm, din), lambda b, i: (b, i, 0)),
            const(din, d), row(), per_b(), const(1, d), const(1, d), per_b(), per_b(),
            const(d, LANES), const(1, LANES),
        ],
        out_specs=[row(), row(), pl.BlockSpec((None, tm, LANES), lambda b, i: (b, i, 0)),
                   pl.BlockSpec((None, None, SUBLANES, tm), lambda b, i: (b, i, 0, 0)), const(1, LANES)],
        out_shape=[jax.ShapeDtypeStruct((bsz, t, d), F32), jax.ShapeDtypeStruct((bsz, t, d), F32),
                   jax.ShapeDtypeStruct((bsz, t, LANES), F32),
                   jax.ShapeDtypeStruct((bsz, t // tm, SUBLANES, tm), jnp.int32),
                   jax.ShapeDtypeStruct((1, LANES), F32)],
        scratch_shapes=[pltpu.VMEM((1, LANES), F32)],
        compiler_params=_params("arbitrary", "arbitrary"),
        name="outproj_ln",
    )(o, w_out, x, gt, ln_g.reshape(1, d), ln_b.reshape(1, d), sc2, sh2, w_router, b_router)


def _moe_kernel(blk_e_ref, nused_ref, x_ref, wg_ref, wu_ref, wd_ref, o_ref, wg16, wu16, wd16):
    i = pl.program_id(0)
    e = blk_e_ref[i]
    e_prev = blk_e_ref[jnp.maximum(i - 1, 0)]

    @pl.when((i == 0) | (e != e_prev))
    def _():
        wg16[...] = wg_ref[...].astype(BF16)
        wu16[...] = wu_ref[...].astype(BF16)
        wd16[...] = wd_ref[...].astype(BF16)

    @pl.when(i < nused_ref[0])
    def _():
        x = x_ref[...].astype(BF16)
        gate = _dot(x, wg16[...])
        up = _dot(x, wu16[...])
        o_ref[...] = _dot((_silu(gate) * up).astype(BF16), wd16[...])

    @pl.when(i >= nused_ref[0])
    def _():
        o_ref[...] = jnp.zeros_like(o_ref)


def moe_ffn(xg, blk_e, n_used, w_gate, w_up, w_down):
    rows, d = xg.shape
    de = w_gate.shape[-1]
    n_blk = rows // MOE_BLOCK
    grid_spec = pltpu.PrefetchScalarGridSpec(
        num_scalar_prefetch=2,
        grid=(n_blk,),
        in_specs=[
            pl.BlockSpec((MOE_BLOCK, d), lambda i, be, nu: (i, 0)),
            pl.BlockSpec((None, d, de), lambda i, be, nu: (be[i], 0, 0)),
            pl.BlockSpec((None, d, de), lambda i, be, nu: (be[i], 0, 0)),
            pl.BlockSpec((None, de, d), lambda i, be, nu: (be[i], 0, 0)),
        ],
        out_specs=pl.BlockSpec((MOE_BLOCK, d), lambda i, be, nu: (i, 0)),
        scratch_shapes=[pltpu.VMEM((d, de), BF16), pltpu.VMEM((d, de), BF16), pltpu.VMEM((de, d), BF16)],
    )
    return pl.pallas_call(
        _moe_kernel,
        grid_spec=grid_spec,
        out_shape=jax.ShapeDtypeStruct((rows, d), F32),
        compiler_params=_params("arbitrary"),
        name="moe_ffn",
    )(blk_e, n_used, xg, w_gate, w_up, w_down)


def _dispatch_kernel(ps_ref, idx_ref, x_ref, xg_in, xg_out, dest_ref, sem, *, tile):
    del xg_in

    def row_copy(r, d):
        return pltpu.make_async_copy(x_ref.at[pl.ds(r, 1)], xg_out.at[pl.ds(d, 1)], sem)

    copies = []
    for r in range(tile):
        for k in range(TOP_K):
            d = ps_ref[idx_ref[k, r]] + idx_ref[2 * TOP_K + k, r]
            dest_ref[k, r] = d
            copies.append(row_copy(r, d))
            copies[-1].start()
    for cp in copies:
        cp.wait()


def moe_dispatch(hf, idx, pad_start, rows):
    n, d = hf.shape
    n_tiles, _, tile = idx.shape
    grid_spec = pltpu.PrefetchScalarGridSpec(
        num_scalar_prefetch=1,
        grid=(n_tiles,),
        in_specs=[
            pl.BlockSpec((None, SUBLANES, tile), lambda i, ps: (i, 0, 0), memory_space=pltpu.SMEM),
            pl.BlockSpec((tile, d), lambda i, ps: (i, 0)),
            pl.BlockSpec(memory_space=pl.ANY),
        ],
        out_specs=[
            pl.BlockSpec(memory_space=pl.ANY),
            pl.BlockSpec((None, TOP_K, tile), lambda i, ps: (i, 0, 0), memory_space=pltpu.SMEM),
        ],
        scratch_shapes=[pltpu.SemaphoreType.DMA(())],
    )
    return pl.pallas_call(
        functools.partial(_dispatch_kernel, tile=tile),
        grid_spec=grid_spec,
        out_shape=[jax.ShapeDtypeStruct((rows, d), hf.dtype),
                   jax.ShapeDtypeStruct((n_tiles, TOP_K, tile), jnp.int32)],
        input_output_aliases={3: 0},
        compiler_params=_params("arbitrary"),
        name="moe_dispatch",
    )(pad_start, idx, hf, jnp.zeros((rows, d), hf.dtype))


def _combine_ln_kernel(dest_ref, x_ref, rt_ref, gt_ref, g_ref, b_ref, yb_ref, o_ref, buf, sem, *, tile):
    copies = []
    for r in range(tile):
        for k in range(TOP_K):
            copies.append(pltpu.make_async_copy(yb_ref.at[pl.ds(dest_ref[k, r], 1)], buf.at[k, pl.ds(r, 1)], sem))
            copies[-1].start()
    for cp in copies:
        cp.wait()
    rt = rt_ref[...]
    y = rt[:, TOP_K:TOP_K + 1] * buf[0]
    for k in range(1, TOP_K):
        y = y + rt[:, TOP_K + k:TOP_K + k + 1] * buf[k]
    o_ref[...] = _layer_norm(ALPHA * x_ref[...] + (1.0 + gt_ref[...]) * y, g_ref[...], b_ref[...])


def moe_combine_ln(dest, yb, x1, route, gt, ln_g, ln_b):
    bsz, t, d = x1.shape
    n = bsz * t
    n_tiles, _, tile = dest.shape
    per_b = t // tile
    row = lambda w: pl.BlockSpec((tile, w), lambda i: (i, 0))
    out = pl.pallas_call(
        functools.partial(_combine_ln_kernel, tile=tile),
        grid=(n_tiles,),
        in_specs=[
            pl.BlockSpec((None, TOP_K, tile), lambda i: (i, 0, 0), memory_space=pltpu.SMEM),
            row(d), row(LANES),
            pl.BlockSpec((None, 1, d), lambda i: (i // per_b, 0, 0)),
            pl.BlockSpec((1, d), lambda i: (0, 0)), pl.BlockSpec((1, d), lambda i: (0, 0)),
            pl.BlockSpec(memory_space=pl.ANY),
        ],
        out_specs=row(d),
        out_shape=jax.ShapeDtypeStruct((n, d), F32),
        scratch_shapes=[pltpu.VMEM((TOP_K, tile, d), F32), pltpu.SemaphoreType.DMA(())],
        compiler_params=_params("arbitrary"),
        name="moe_combine_ln",
    )(dest, x1.reshape(n, d), route.reshape(n, LANES), gt, ln_g.reshape(1, d), ln_b.reshape(1, d), yb)
    return out.reshape(bsz, t, d)


def hier_moe_ln(x1, hf, route, idx, counts, gt, ln_g, ln_b, w_gate, w_up, w_down):
    bsz, t, d = hf.shape
    n = bsz * t
    n_exp = w_gate.shape[0]
    counts = counts[0, N_GROUPS:N_GROUPS + n_exp].astype(jnp.int32)
    padded = (counts + MOE_BLOCK - 1) // MOE_BLOCK * MOE_BLOCK
    pad_end = jnp.cumsum(padded)
    pad_start = (pad_end - padded).astype(jnp.int32)
    n_blk = -(-(n * TOP_K) // MOE_BLOCK) + n_exp
    blk_first = jnp.arange(n_blk, dtype=jnp.int32) * MOE_BLOCK
    blk_e = jnp.minimum(jnp.sum((pad_end[None, :] <= blk_first[:, None]).astype(jnp.int32), axis=1), n_exp - 1)
    n_used = (pad_end[-1:] // MOE_BLOCK).astype(jnp.int32)
    idx = idx.reshape((-1,) + idx.shape[2:])
    xg, dest = moe_dispatch(hf.reshape(n, d), idx, pad_start, n_blk * MOE_BLOCK)
    yb = moe_ffn(xg, blk_e.astype(jnp.int32), n_used, w_gate, w_up, w_down)
    return moe_combine_ln(dest, yb, x1, route, gt, ln_g, ln_b)


def _pad_cols(w, mult=LANES):
    pad = (-w.shape[-1]) % mult
    return jnp.pad(w, ((0, 0), (0, pad))) if pad else w


def kernel(x, c, ada_w, ada_b, ln1_g, ln1_b, ln2_g, ln2_b, gdn_w_in, gdn_conv_w, gdn_a_log, gdn_dt_bias, gdn_norm_g, gdn_w_out, lru_w_in, lru_conv_w, lru_conv_b, lru_w_a, lru_b_a, lru_w_x, lru_b_x, lru_lambda, lru_w_out, nsa_w_in, nsa_pe_k, nsa_pe_v, nsa_ck_w1, nsa_ck_w2, nsa_cv_w1, nsa_cv_w2, nsa_w_out, moe_w_grp, moe_b_grp, moe_w_exp, moe_b_exp, moe_w_gate, moe_w_up, moe_w_down):
    bsz, t, d = x.shape
    depth = ada_w.shape[0]
    mod = ada_modulation(c, ada_w, ada_b).reshape(depth, bsz, 6, 1, d)
    ja = jb = jc = 0
    for i in range(depth):
        sh1, sc1, gt1, sh2, sc2, gt2 = (mod[i, :, k] for k in range(6))
        kind = i % N_MIXERS
        if kind == 0:
            p = modproj(x, sc1, sh1, _pad_cols(gdn_w_in[ja]).astype(BF16))
            o = gdn_core(p, gdn_conv_w[ja], gdn_a_log[ja], gdn_dt_bias[ja], gdn_norm_g[ja])
            w_out = gdn_w_out[ja]
            ja += 1
        elif kind == 1:
            p = modproj(x, sc1, sh1, lru_w_in[jb].astype(BF16))
            o = lru_core(p, lru_conv_w[jb], lru_conv_b[jb], lru_w_a[jb], lru_b_a[jb], lru_w_x[jb], lru_b_x[jb],
                         lru_lambda[jb])
            w_out = lru_w_out[jb]
            jb += 1
        else:
            o = nsa_mixer_core(x, sc1, sh1, nsa_w_in[jc], nsa_pe_k[jc], nsa_pe_v[jc], nsa_ck_w1[jc], nsa_ck_w2[jc],
                               nsa_cv_w1[jc], nsa_cv_w2[jc])
            w_out = nsa_w_out[jc]
            jc += 1
        w_router = _pad_cols(jnp.concatenate([moe_w_grp[i], moe_w_exp[i]], axis=1))
        b_router = _pad_cols(jnp.concatenate([moe_b_grp[i], moe_b_exp[i]])[None, :])
        x1, hf, route, idx, counts = outproj_ln(o, w_out.astype(BF16), x, gt1, ln1_g[i], ln1_b[i], sc2, sh2,
                                                w_router, b_router)
        x = hier_moe_ln(x1, hf, route, idx, counts, gt2, ln2_g[i], ln2_b[i],
                        moe_w_gate[i], moe_w_up[i], moe_w_down[i])
    return x
```

```python
import functools

import jax
import jax.numpy as jnp
from jax import lax
from jax.experimental import pallas as pl
from jax.experimental.pallas import tpu as pltpu

F32 = jnp.float32
BF16 = jnp.bfloat16

D_MODEL = 1024
DEPTH = 4
N_MIXERS = 3
LN_EPS = 1e-5
NORM_EPS = 1e-6
CONV_K = 4
ALPHA = (2.0 * DEPTH) ** 0.25
GDN_HEADS = 8
GDN_DK = 128
GDN_CHUNK = 64
GDN_QKV = 3 * GDN_HEADS * GDN_DK
LRU_WIDTH = 1024
LRU_BLOCKS = 16
LRU_BW = LRU_WIDTH // LRU_BLOCKS
RG_C = 8.0
NSA_GROUPS = 4
NSA_HPG = 4
NSA_DK = 64
NSA_Q = NSA_GROUPS * NSA_HPG * NSA_DK
NSA_KV = NSA_GROUPS * NSA_DK
CMP_LEN = 32
CMP_STRIDE = 16
SLC_LEN = 64
N_SEL = 16
WINDOW = 512
Q_BLOCK = 128
SEL_KEY_CHUNK = 512
N_GROUPS = 4
EXP_PER_GROUP = 8
N_EXPERTS = N_GROUPS * EXP_PER_GROUP
TOP_K = 2
MOE_BLOCK = 256
NEG = -1e30
BIG = 1e30
LOWEST = -3e38

LANES = 128
SUBLANES = 8
VMEM_LIMIT_BYTES = 48 * 1024 * 1024


def _params(*sem):
    return pltpu.CompilerParams(dimension_semantics=sem, vmem_limit_bytes=VMEM_LIMIT_BYTES)


def _sigmoid(x):
    return 1.0 / (1.0 + jnp.exp(-x))


def _silu(x):
    return x * _sigmoid(x)


def _softplus(x):
    return jnp.maximum(x, 0.0) + jnp.log(1.0 + jnp.exp(-jnp.abs(x)))


def _gelu(x):
    return 0.5 * x * (1.0 + jnp.tanh(0.7978845608028654 * (x + 0.044715 * (x * x * x))))


def _dot(a, b):
    return jnp.dot(a, b, preferred_element_type=F32)


def _dot_nt(a, b):
    return lax.dot_general(a, b, (((1,), (1,)), ((), ())), preferred_element_type=F32)


def _dot_tn(a, b):
    return lax.dot_general(a, b, (((0,), (0,)), ((), ())), preferred_element_type=F32)


def _split3(x):
    hi = x.astype(BF16)
    r1 = x - hi.astype(F32)
    mid = r1.astype(BF16)
    lo = (r1 - mid.astype(F32)).astype(BF16)
    return hi, mid, lo


def _dot_lhs_exact(m_bf16, x):
    hi, mid, lo = _split3(x)
    return _dot(m_bf16, hi) + _dot(m_bf16, mid) + _dot(m_bf16, lo)


def _dot_rhs_exact(x, m_bf16):
    hi, mid, lo = _split3(x)
    return _dot(hi, m_bf16) + _dot(mid, m_bf16) + _dot(lo, m_bf16)


def _dot_f32(a, b):
    ah, am, al = _split3(a)
    bh, bm, bl = _split3(b)
    return (_dot(ah, bh) + _dot(ah, bm) + _dot(am, bh)) + (_dot(ah, bl) + _dot(al, bh) + _dot(am, bm))


def _layer_norm(z, g, b):
    mu = jnp.mean(z, axis=-1, keepdims=True)
    zc = z - mu
    var = jnp.mean(zc * zc, axis=-1, keepdims=True)
    return zc * lax.rsqrt(var + LN_EPS) * g + b


def _causal_conv(xbuf, cw_ref, col0, width, tb):
    cols = slice(col0, col0 + width)
    acc = cw_ref[0:1, cols] * xbuf[SUBLANES - 3:SUBLANES - 3 + tb, cols]
    for k in range(1, CONV_K):
        off = SUBLANES - 3 + k
        acc = acc + cw_ref[k:k + 1, cols] * xbuf[off:off + tb, cols]
    return acc


def _ada_kernel(c_ref, w_ref, b_ref, o_ref):
    cond = _silu(c_ref[...])
    o_ref[...] = _dot_f32(cond, w_ref[...]) + b_ref[...]


def ada_modulation(c, ada_w, ada_b, col_tile=1536):
    depth, d, n6 = ada_w.shape
    bsz = c.shape[0]
    return pl.pallas_call(
        _ada_kernel,
        grid=(depth, n6 // col_tile),
        in_specs=[
            pl.BlockSpec((bsz, d), lambda i, j: (0, 0)),
            pl.BlockSpec((None, d, col_tile), lambda i, j: (i, 0, j)),
            pl.BlockSpec((None, 1, col_tile), lambda i, j: (i, 0, j)),
        ],
        out_specs=pl.BlockSpec((None, bsz, col_tile), lambda i, j: (i, 0, j)),
        out_shape=jax.ShapeDtypeStruct((depth, bsz, n6), F32),
        compiler_params=_params("arbitrary", "arbitrary"),
        name="ada_modulation",
    )(c, ada_w, ada_b.reshape(depth, 1, n6))


def _modproj_kernel(x_ref, sc_ref, sh_ref, w_ref, o_ref, *, col_chunk):
    h = (x_ref[...] * (1.0 + sc_ref[...]) + sh_ref[...]).astype(BF16)
    ncols = o_ref.shape[-1]
    for c0 in range(0, ncols, col_chunk):
        c1 = min(c0 + col_chunk, ncols)
        o_ref[:, c0:c1] = _dot(h, w_ref[:, c0:c1])


def modproj(x, sc, sh, w, tm=256, col_chunk=1024):
    bsz, t, d = x.shape
    ncols = w.shape[1]
    return pl.pallas_call(
        functools.partial(_modproj_kernel, col_chunk=col_chunk),
        grid=(bsz, t // tm),
        in_specs=[
            pl.BlockSpec((None, tm, d), lambda b, i: (b, i, 0)),
            pl.BlockSpec((None, 1, d), lambda b, i: (b, 0, 0)),
            pl.BlockSpec((None, 1, d), lambda b, i: (b, 0, 0)),
            pl.BlockSpec((d, ncols), lambda b, i: (0, 0)),
        ],
        out_specs=pl.BlockSpec((None, tm, ncols), lambda b, i: (b, i, 0)),
        out_shape=jax.ShapeDtypeStruct((bsz, t, ncols), F32),
        compiler_params=_params("arbitrary", "arbitrary"),
        name="modproj",
    )(x, sc, sh, w)


def _gdn_kernel(qkv_ref, z_ref, ab_ref, cw_ref, alog_ref, dtb_ref, ng_ref, o_ref, xbuf, s_ref, *, tb):
    nh, dk, chunk = GDN_HEADS, GDN_DK, GDN_CHUNK

    @pl.when(pl.program_id(1) == 0)
    def _():
        xbuf[0:SUBLANES, :] = jnp.zeros((SUBLANES, GDN_QKV), F32)
        s_ref[...] = jnp.zeros_like(s_ref)

    xbuf[SUBLANES:SUBLANES + tb, :] = qkv_ref[...]

    ab = ab_ref[...]
    g_all = -jnp.exp(alog_ref[...]) * _softplus(ab + dtb_ref[...])
    beta_all = _sigmoid(ab)
    ri = lax.broadcasted_iota(jnp.int32, (tb, tb), 0)
    ci = lax.broadcasted_iota(jnp.int32, (tb, tb), 1)
    same = (ri // chunk) == (ci // chunk)
    causal = same & (ci <= ri)
    strict = same & (ci < ri)
    eye = jnp.where(ri == ci, 1.0, 0.0)
    tri = jnp.where(causal, 1.0, 0.0).astype(BF16)
    ones_blk = jnp.where(same, 1.0, 0.0).astype(BF16)
    gc_all = _dot_lhs_exact(tri, g_all)
    glast_all = _dot_lhs_exact(ones_blk, g_all)
    gc_t = gc_all.T
    egc_all = jnp.exp(gc_all)
    kdec_all = jnp.exp(glast_all - gc_all)

    heads = range(nh)
    lmats, rhss, qk16s, qg16s, ks16s, gls = [], [], [], [], [], []
    for h in heads:
        q = _silu(_causal_conv(xbuf, cw_ref, h * dk, dk, tb))
        k = _silu(_causal_conv(xbuf, cw_ref, (nh + h) * dk, dk, tb))
        v = _silu(_causal_conv(xbuf, cw_ref, (2 * nh + h) * dk, dk, tb))
        q = q * lax.rsqrt(jnp.sum(q * q, axis=-1, keepdims=True) + NORM_EPS)
        k = k * lax.rsqrt(jnp.sum(k * k, axis=-1, keepdims=True) + NORM_EPS)
        beta = beta_all[:, nh + h:nh + h + 1]
        eg = egc_all[:, h:h + 1]
        decay = jnp.exp(jnp.where(causal, gc_all[:, h:h + 1] - gc_t[h:h + 1, :], NEG))
        kb = k * beta
        k16 = k.astype(BF16)
        qs = q * (dk ** -0.5)
        lmats.append(jnp.where(strict, _dot_nt(kb.astype(BF16), k16) * decay, 0.0))
        rhss.append(jnp.concatenate([v * beta, kb * eg], axis=1).astype(BF16))
        qk16s.append((_dot_nt(qs.astype(BF16), k16) * decay).astype(BF16))
        qg16s.append((qs * eg).astype(BF16))
        ks16s.append((k * kdec_all[:, h:h + 1]).astype(BF16))
        gls.append(glast_all[:, h:h + 1])
    xbuf[0:SUBLANES, :] = xbuf[tb:tb + SUBLANES, :]

    xpows = lmats
    tinvs = [eye - lm for lm in lmats]
    for _ in range(5):
        xpows = [_dot(x16, x16) for x16 in [x.astype(BF16) for x in xpows]]
        tinvs = [t + _dot(t.astype(BF16), x.astype(BF16)) for t, x in zip(tinvs, xpows)]
    uws = [_dot(t.astype(BF16), rhs) for t, rhs in zip(tinvs, rhss)]
    us = [uw[:, :dk] for uw in uws]
    w16s = [uw[:, dk:].astype(BF16) for uw in uws]

    states = [s_ref[h] for h in heads]
    o_parts = [[] for _ in heads]
    vn_parts = [[] for _ in heads]
    for c in range(tb // chunk):
        rows = slice(c * chunk, (c + 1) * chunk)
        for h in heads:
            s16 = states[h].astype(BF16)
            vn16 = (us[h][rows] - _dot(w16s[h][rows], s16)).astype(BF16)
            o_parts[h].append(_dot(qg16s[h][rows], s16))
            states[h] = (states[h] * jnp.exp(gls[h][c * chunk:c * chunk + 1, :])
                         + _dot_tn(ks16s[h][rows], vn16))
            vn_parts[h].append(vn16)
    for h in heads:
        s_ref[h] = states[h]
        o = jnp.concatenate(o_parts[h], axis=0) + _dot(qk16s[h], jnp.concatenate(vn_parts[h], axis=0))
        o = o * lax.rsqrt(jnp.mean(o * o, axis=-1, keepdims=True) + NORM_EPS) * ng_ref[...]
        o_ref[:, h * dk:(h + 1) * dk] = o * _silu(z_ref[:, h * dk:(h + 1) * dk])


def gdn_core(p, conv_w, a_log, dt_bias, norm_g, tb=256):
    bsz, t, _ = p.shape
    nh, dk = GDN_HEADS, GDN_DK
    v_w = nh * dk
    alog = jnp.zeros((1, LANES), F32).at[0, :nh].set(a_log)
    dtb = jnp.zeros((1, LANES), F32).at[0, :nh].set(dt_bias)
    ab_blk = (GDN_QKV + v_w) // LANES
    return pl.pallas_call(
        functools.partial(_gdn_kernel, tb=tb),
        grid=(bsz, t // tb),
        in_specs=[
            pl.BlockSpec((None, tb, GDN_QKV), lambda b, i: (b, i, 0)),
            pl.BlockSpec((None, tb, v_w), lambda b, i: (b, i, GDN_QKV // v_w)),
            pl.BlockSpec((None, tb, LANES), lambda b, i: (b, i, ab_blk)),
            pl.BlockSpec((CONV_K, GDN_QKV), lambda b, i: (0, 0)),
            pl.BlockSpec((1, LANES), lambda b, i: (0, 0)),
            pl.BlockSpec((1, LANES), lambda b, i: (0, 0)),
            pl.BlockSpec((1, dk), lambda b, i: (0, 0)),
        ],
        out_specs=pl.BlockSpec((None, tb, v_w), lambda b, i: (b, i, 0)),
        out_shape=jax.ShapeDtypeStruct((bsz, t, v_w), F32),
        scratch_shapes=[pltpu.VMEM((tb + SUBLANES, GDN_QKV), F32), pltpu.VMEM((nh, dk, dk), F32)],
        compiler_params=_params("arbitrary", "arbitrary"),
        name="gdn_core",
    )(p, p, p, conv_w, alog, dtb, norm_g.reshape(1, dk))


def _lru_kernel(gate_ref, xb_ref, cw_ref, cb_ref, wax_ref, ba_ref, bx_ref, lam_ref, o_ref, xbuf, h_ref, *, tb):
    width = LRU_WIDTH
    grp = 2 * LANES

    @pl.when(pl.program_id(1) == 0)
    def _():
        xbuf[0:SUBLANES, :] = jnp.zeros((SUBLANES, width), F32)
        h_ref[...] = jnp.zeros_like(h_ref)

    xbuf[SUBLANES:SUBLANES + tb, :] = xb_ref[...]
    row = lax.broadcasted_iota(jnp.int32, (tb, grp), 0)

    for j in range(width // grp):
        cols = slice(j * grp, (j + 1) * grp)
        xc = _causal_conv(xbuf, cw_ref, j * grp, grp, tb) + cb_ref[:, cols]
        res = _dot(xc.astype(BF16), wax_ref[j])
        r = _sigmoid(res[:, :grp] + ba_ref[:, cols])
        ig = _sigmoid(res[:, grp:] + bx_ref[:, cols])
        log_a = -RG_C * r * _softplus(-lam_ref[:, cols])
        a = jnp.exp(log_a)
        mult = jnp.sqrt(jnp.maximum(1.0 - jnp.exp(2.0 * log_a), 0.0))
        u = mult * ig * xc
        s = 1
        while s < tb:
            a_sh = jnp.where(row < s, 1.0, pltpu.roll(a, s, 0))
            u_sh = jnp.where(row < s, 0.0, pltpu.roll(u, s, 0))
            u = a * u_sh + u
            a = a * a_sh
            s *= 2
        hs = u + a * h_ref[:, cols]
        h_ref[:, cols] = hs[tb - 1:tb, :]
        o_ref[:, cols] = hs * _gelu(gate_ref[:, cols])

    xbuf[0:SUBLANES, :] = xbuf[tb:tb + SUBLANES, :]


def _block_diag_groups(w):
    per = (2 * LANES) // LRU_BW
    wg = w.reshape(LRU_BLOCKS // per, per, LRU_BW, LRU_BW)
    eye = jnp.eye(per, dtype=w.dtype)
    return jnp.einsum('gaij,ab->gaibj', wg, eye).reshape(LRU_BLOCKS // per, per * LRU_BW, per * LRU_BW)


def lru_core(p, conv_w, conv_b, w_a, b_a, w_x, b_x, lam, tb=256):
    bsz, t, _ = p.shape
    width = LRU_WIDTH
    grp = 2 * LANES
    wax = jnp.concatenate([_block_diag_groups(w_a), _block_diag_groups(w_x)], axis=-1).astype(BF16)
    vec = lambda a: a.reshape(1, width)
    return pl.pallas_call(
        functools.partial(_lru_kernel, tb=tb),
        grid=(bsz, t // tb),
        in_specs=[
            pl.BlockSpec((None, tb, width), lambda b, i: (b, i, 0)),
            pl.BlockSpec((None, tb, width), lambda b, i: (b, i, 1)),
            pl.BlockSpec((CONV_K, width), lambda b, i: (0, 0)),
            pl.BlockSpec((1, width), lambda b, i: (0, 0)),
            pl.BlockSpec((width // grp, grp, 2 * grp), lambda b, i: (0, 0, 0)),
            pl.BlockSpec((1, width), lambda b, i: (0, 0)),
            pl.BlockSpec((1, width), lambda b, i: (0, 0)),
            pl.BlockSpec((1, width), lambda b, i: (0, 0)),
        ],
        out_specs=pl.BlockSpec((None, tb, width), lambda b, i: (b, i, 0)),
        out_shape=jax.ShapeDtypeStruct((bsz, t, width), F32),
        scratch_shapes=[pltpu.VMEM((tb + SUBLANES, width), F32), pltpu.VMEM((1, width), F32)],
        compiler_params=_params("arbitrary", "arbitrary"),
        name="lru_core",
    )(p, p, conv_w, vec(conv_b), wax, vec(b_a), vec(b_x), vec(lam))


def _cmp_kernel(x_ref, pelo_ref, pehi_ref, w1a_ref, w1b_ref, w2_ref, o_ref):
    x = x_ref[...]
    nseg = x.shape[0]
    first = _dot((x + pelo_ref[...]).astype(BF16), w1a_ref[...])
    second = _dot((x + pehi_ref[...]).astype(BF16), w1b_ref[...])
    hid = _gelu(first + pltpu.roll(second, nseg - 1, 0))
    o_ref[...] = _dot(hid.astype(BF16), w2_ref[...]).astype(o_ref.dtype)


def nsa_compress(xseg, pe, w1, w2):
    _, bsz, g, nseg, seg_w = xseg.shape
    hid = w1.shape[-1]
    pelo = pe[:, :CMP_STRIDE].reshape(2, 1, seg_w)
    pehi = pe[:, CMP_STRIDE:].reshape(2, 1, seg_w)
    w1a = w1[:, :seg_w].astype(BF16)
    w1b = w1[:, seg_w:].astype(BF16)
    w2p = jnp.pad(w2, ((0, 0), (0, 0), (0, LANES - NSA_DK))).astype(BF16)
    return pl.pallas_call(
        _cmp_kernel,
        grid=(2, bsz, g),
        in_specs=[
            pl.BlockSpec((None, None, None, nseg, seg_w), lambda a, b, c: (a, b, c, 0, 0)),
            pl.BlockSpec((None, 1, seg_w), lambda a, b, c: (a, 0, 0)),
            pl.BlockSpec((None, 1, seg_w), lambda a, b, c: (a, 0, 0)),
            pl.BlockSpec((None, seg_w, hid), lambda a, b, c: (a, 0, 0)),
            pl.BlockSpec((None, seg_w, hid), lambda a, b, c: (a, 0, 0)),
            pl.BlockSpec((None, hid, LANES), lambda a, b, c: (a, 0, 0)),
        ],
        out_specs=pl.BlockSpec((None, None, None, nseg, LANES), lambda a, b, c: (a, b, c, 0, 0)),
        out_shape=jax.ShapeDtypeStruct((2, bsz, g, nseg, LANES), BF16),
        compiler_params=_params("arbitrary", "arbitrary", "arbitrary"),
        name="nsa_compress",
    )(xseg, pelo, pehi, w1a, w1b, w2p)


DEN_LANE = NSA_DK


def _nsa_kernel(q_ref, gate_ref, kc_ref, vc_ref, ks_ref, vs_ref, kw_ref, vw_ref, wbias_ref, o_ref, *, t_len):
    p_heads, dk, qb = NSA_HPG, NSA_DK, Q_BLOCK
    rows = p_heads * qb
    n_cmp_rows = kc_ref.shape[0]
    n_slc = t_len // SLC_LEN
    n_sel = min(N_SEL, n_slc)
    kc_len = SEL_KEY_CHUNK
    t0 = pl.program_id(2) * qb

    qg = q_ref[...]
    halves = []
    for p in range(p_heads):
        tile = qg[:, (p // 2) * LANES:(p // 2 + 1) * LANES]
        halves.append(tile if p % 2 == 0 else pltpu.roll(tile, dk, 1))
    lane = lax.broadcasted_iota(jnp.int32, (rows, LANES), 1)
    qz = jnp.where(lane < dk, jnp.concatenate(halves, axis=0) * (dk ** -0.5), 0.0)
    qz16 = qz.astype(BF16)
    tpos = t0 + (lax.broadcasted_iota(jnp.int32, (rows, 1), 0) % qb)

    w_len = qb + WINDOW
    w0 = pl.multiple_of(t0, qb)
    qw16 = jnp.where(lane == DEN_LANE, NEG, qz).astype(BF16)
    s_w = _dot_nt(qw16, kw_ref[pl.ds(w0, w_len), :]) + wbias_ref[...]
    e_w = jnp.exp(s_w - jnp.max(s_w, axis=-1, keepdims=True))
    acc_w = _dot(e_w.astype(BF16), vw_ref[pl.ds(w0, w_len), :])
    o_w = acc_w * (1.0 / acc_w[:, DEN_LANE:DEN_LANE + 1])

    s_c = _dot_nt(qz16, kc_ref[...])
    n_idx = lax.broadcasted_iota(jnp.int32, (rows, n_cmp_rows), 1)
    vis = (n_idx * CMP_STRIDE + (CMP_LEN - 1)) <= tpos
    s_c = jnp.where(vis, s_c, NEG)
    e_c = jnp.where(vis, jnp.exp(s_c - jnp.max(s_c, axis=-1, keepdims=True)), 0.0)
    l_c = jnp.sum(e_c, axis=-1, keepdims=True)
    p_c = e_c * (1.0 / jnp.where(l_c > 0.0, l_c, 1.0))
    o_c = _dot(p_c.astype(BF16), vc_ref[...])

    p_sum = p_c[0:qb]
    for p in range(1, p_heads):
        p_sum = p_sum + p_c[p * qb:(p + 1) * qb]
    sj = lax.broadcasted_iota(jnp.int32, (n_slc, n_cmp_rows), 0) * SLC_LEN
    cn = lax.broadcasted_iota(jnp.int32, (n_slc, n_cmp_rows), 1) * CMP_STRIDE
    ov = jnp.maximum(jnp.minimum(cn + CMP_LEN, sj + SLC_LEN) - jnp.maximum(cn, sj), 0)
    overlap_t = (ov.astype(F32) * (1.0 / CMP_LEN)).astype(BF16)
    imp = None
    for piece in _split3(p_sum):
        part = _dot_nt(overlap_t, piece)
        imp = part if imp is None else imp + part
    s_idx = lax.broadcasted_iota(jnp.int32, (n_slc, qb), 0)
    tq = t0 + lax.broadcasted_iota(jnp.int32, (n_slc, qb), 1)
    cur = tq // SLC_LEN
    forced = (s_idx == 0) | (s_idx == cur) | (s_idx == cur - 1)
    imp = jnp.where(forced, BIG, imp)
    imp = jnp.where(s_idx * SLC_LEN <= tq, imp, -BIG)
    s_idx_f = s_idx.astype(F32)
    sel_t = jnp.zeros((n_slc, qb), F32)
    for _ in range(n_sel):
        m = jnp.max(imp, axis=0, keepdims=True)
        first = jnp.min(jnp.where(imp == m, s_idx_f, float(n_slc)), axis=0, keepdims=True)
        pick = s_idx_f == first
        sel_t = jnp.where(pick, 1.0, sel_t)
        imp = jnp.where(pick, LOWEST, imp)
    own_blk = t0 // SLC_LEN
    sel_t = jnp.where((s_idx == own_blk) | (s_idx == own_blk + 1) | (s_idx * SLC_LEN > tq), 0.0, sel_t)
    selneg = ((sel_t.T - 1.0) * BIG).astype(BF16)
    q_aug = jnp.concatenate([jnp.concatenate([selneg] * p_heads, axis=0), qz16], axis=1)

    own0 = pl.multiple_of(t0, qb)
    s_own = _dot_nt(qz16, ks_ref[pl.ds(own0, qb), n_slc:n_slc + LANES])
    s_own = jnp.where((t0 + lax.broadcasted_iota(jnp.int32, (rows, qb), 1)) <= tpos, s_own, NEG)
    m_own = jnp.max(s_own, axis=-1, keepdims=True)
    acc_own = _dot(jnp.exp((s_own - m_own).astype(BF16)), vs_ref[pl.ds(own0, qb), :])

    def sel_update(c, carry, scores):
        m, acc = carry
        m_new = jnp.maximum(m, jnp.max(scores, axis=-1, keepdims=True))
        prob = jnp.exp((scores - m_new).astype(BF16))
        pv = _dot(prob, vs_ref[pl.ds(pl.multiple_of(c * kc_len, kc_len), kc_len), :])
        return m_new, jnp.exp(m - m_new) * acc + pv

    def sel_scores(c):
        return _dot_nt(q_aug, ks_ref[pl.ds(pl.multiple_of(c * kc_len, kc_len), kc_len), :])

    def sel_pair(i, carry):
        s_a = sel_scores(2 * i)
        s_b = sel_scores(2 * i + 1)
        return sel_update(2 * i + 1, sel_update(2 * i, carry, s_a), s_b)

    n_chunks = t0 // kc_len + 1
    carry = lax.fori_loop(0, n_chunks // 2, sel_pair, (m_own, acc_own))
    _, acc_s = lax.cond(n_chunks % 2 == 1,
                        lambda cr: sel_update(n_chunks - 1, cr, sel_scores(n_chunks - 1)),
                        lambda cr: cr, carry)
    o_s = acc_s * (1.0 / acc_s[:, DEN_LANE:DEN_LANE + 1])

    gts = _sigmoid(gate_ref[...])
    outs = []
    for p in range(p_heads):
        r = slice(p * qb, (p + 1) * qb)
        outs.append(gts[:, 3 * p:3 * p + 1] * o_c[r] + gts[:, 3 * p + 1:3 * p + 2] * o_s[r]
                    + gts[:, 3 * p + 2:3 * p + 3] * o_w[r])
    lane_o = lax.broadcasted_iota(jnp.int32, (qb, LANES), 1)
    for t in range(p_heads // 2):
        o_ref[:, t * LANES:(t + 1) * LANES] = jnp.where(lane_o < dk, outs[2 * t], pltpu.roll(outs[2 * t + 1], dk, 1))


def nsa_attention(p, gates, kvcmp, ks_aug, vs_aug, kw_aug, vw_aug):
    bsz, t, _ = p.shape
    g, dk, qb = NSA_GROUPS, NSA_DK, Q_BLOCK
    rows = NSA_HPG * qb
    w_len = qb + WINDOW
    qw = NSA_HPG * dk
    ri = jnp.arange(rows, dtype=jnp.int32)[:, None] % qb
    ki = jnp.arange(w_len, dtype=jnp.int32)[None, :]
    wbias = jnp.where((ki > ri) & (ki <= ri + WINDOW), 0.0, NEG).astype(F32)
    full = lambda a: pl.BlockSpec((None, None) + a.shape[2:], lambda b, c, j: (b, c, 0, 0))
    return pl.pallas_call(
        functools.partial(_nsa_kernel, t_len=t),
        grid=(bsz, g, t // qb),
        in_specs=[
            pl.BlockSpec((None, qb, qw), lambda b, c, j: (b, j, c)),
            pl.BlockSpec((None, None, qb, NSA_HPG * 3), lambda b, c, j: (b, c, j, 0)),
            full(kvcmp[0]), full(kvcmp[1]), full(ks_aug), full(vs_aug), full(kw_aug), full(vw_aug),
            pl.BlockSpec((rows, w_len), lambda b, c, j: (0, 0)),
        ],
        out_specs=pl.BlockSpec((None, qb, qw), lambda b, c, j: (b, j, c)),
        out_shape=jax.ShapeDtypeStruct((bsz, t, g * qw), F32),
        compiler_params=_params("arbitrary", "arbitrary", "arbitrary"),
        name="nsa_attention",
    )(p, gates, kvcmp[0], kvcmp[1], ks_aug, vs_aug, kw_aug, vw_aug, wbias)


def nsa_mixer_core(x, sc, sh, w_in, pe_k, pe_v, ck_w1, ck_w2, cv_w1, cv_w2):
    bsz, t, _ = x.shape
    g, dk = NSA_GROUPS, NSA_DK
    n_slc = t // SLC_LEN
    p = modproj(x, sc, sh, _pad_cols(w_in).astype(BF16))
    kv = p[:, :, NSA_Q:NSA_Q + 6 * NSA_KV].reshape(bsz, t, 6, g, dk).transpose(2, 0, 3, 1, 4)
    xseg = kv[0:2].reshape(2, bsz, g, t // CMP_STRIDE, CMP_STRIDE * dk)
    kvcmp = nsa_compress(xseg, jnp.stack([pe_k, pe_v]), jnp.stack([ck_w1, cv_w1]), jnp.stack([ck_w2, cv_w2]))
    kv16 = kv[2:6].astype(BF16)
    lead = (bsz, g, t)
    zeros = lambda n: jnp.zeros(lead + (n,), BF16)
    blk_onehot = (jnp.arange(t, dtype=jnp.int32)[:, None] // SLC_LEN
                  == jnp.arange(n_slc, dtype=jnp.int32)[None, :]).astype(BF16)
    ks_aug = jnp.concatenate([jnp.broadcast_to(blk_onehot, lead + (n_slc,)), kv16[0], zeros(LANES - dk)], axis=-1)
    den = jnp.ones(lead + (1,), BF16)
    vs_aug = jnp.concatenate([kv16[1], den, zeros(LANES - dk - 1)], axis=-1)
    pad_k = jnp.zeros((bsz, g, WINDOW, LANES), BF16).at[..., DEN_LANE].set(1.0)
    kw_aug = jnp.concatenate([pad_k, jnp.concatenate([kv16[2], zeros(LANES - dk)], axis=-1)], axis=2)
    vw_aug = jnp.concatenate([jnp.zeros((bsz, g, WINDOW, LANES), BF16),
                              jnp.concatenate([kv16[3], den, zeros(LANES - dk - 1)], axis=-1)], axis=2)
    gates = p[:, :, NSA_Q + 6 * NSA_KV:NSA_Q + 6 * NSA_KV + 3 * g * NSA_HPG]
    gates = gates.reshape(bsz, t, g, 3 * NSA_HPG).transpose(0, 2, 1, 3)
    return nsa_attention(p, gates, kvcmp, ks_aug, vs_aug, kw_aug, vw_aug)


def _outproj_ln_kernel(o_ref, w_ref, x_ref, gt_ref, g_ref, b_ref, sc_ref, sh_ref, wr_ref, br_ref,
                       x1_ref, hf_ref, rt_ref, idx_ref, cnt_ref, cnt_acc):
    @pl.when((pl.program_id(0) == 0) & (pl.program_id(1) == 0))
    def _():
        cnt_acc[...] = jnp.zeros_like(cnt_acc)

    y = _dot(o_ref[...].astype(BF16), w_ref[...])
    x1 = _layer_norm(ALPHA * x_ref[...] + (1.0 + gt_ref[...]) * y, g_ref[...], b_ref[...])
    x1_ref[...] = x1
    hf = x1 * (1.0 + sc_ref[...]) + sh_ref[...]
    hf_ref[...] = hf

    logits = _dot_f32(hf, wr_ref[...]) + br_ref[...]
    lane = lax.broadcasted_iota(jnp.int32, logits.shape, 1).astype(F32)
    is_grp = lane < N_GROUPS
    gl = jnp.where(is_grp, logits, LOWEST)
    gmax = jnp.max(gl, axis=-1, keepdims=True)
    gi = jnp.min(jnp.where(gl == gmax, lane, float(LANES)), axis=-1, keepdims=True)
    gp = 1.0 / jnp.sum(jnp.where(is_grp, jnp.exp(logits - gmax), 0.0), axis=-1, keepdims=True)
    lo = N_GROUPS + EXP_PER_GROUP * gi
    el = jnp.where((lane >= lo) & (lane < lo + EXP_PER_GROUP), logits, LOWEST)
    m1 = jnp.max(el, axis=-1, keepdims=True)
    i1 = jnp.min(jnp.where(el == m1, lane, float(LANES)), axis=-1, keepdims=True)
    el2 = jnp.where(lane == i1, LOWEST, el)
    m2 = jnp.max(el2, axis=-1, keepdims=True)
    i2 = jnp.min(jnp.where(el2 == m2, lane, float(LANES)), axis=-1, keepdims=True)
    e2 = jnp.exp(m2 - m1)
    g1 = gp / (1.0 + e2)
    g2 = gp * e2 / (1.0 + e2)

    tm = logits.shape[0]
    onehot = jnp.where((lane == i1) | (lane == i2), 1.0, 0.0)
    ri = lax.broadcasted_iota(jnp.int32, (tm, tm), 0)
    ci = lax.broadcasted_iota(jnp.int32, (tm, tm), 1)
    before = jnp.where(ci < ri, 1.0, 0.0).astype(BF16)
    seen = _dot(before, onehot.astype(BF16)) + cnt_acc[...]
    rank1 = jnp.sum(jnp.where(lane == i1, seen, 0.0), axis=-1, keepdims=True)
    rank2 = jnp.sum(jnp.where(lane == i2, seen, 0.0), axis=-1, keepdims=True)
    cnt_acc[...] = cnt_acc[...] + jnp.sum(onehot, axis=0, keepdims=True)
    cnt_ref[...] = cnt_acc[...]

    vals = (i1 - N_GROUPS, i2 - N_GROUPS, g1, g2, rank1, rank2)
    out = jnp.zeros_like(logits)
    for pos, val in enumerate(vals):
        out = jnp.where(lane == float(pos), val, out)
    rt_ref[...] = out
    idx_ref[...] = out.T[0:SUBLANES, :].astype(jnp.int32)


def outproj_ln(o, w_out, x, gt, ln_g, ln_b, sc2, sh2, w_router, b_router, tm=256):
    bsz, t, d = x.shape
    din = o.shape[-1]
    row = lambda: pl.BlockSpec((None, tm, d), lambda b, i: (b, i, 0))
    per_b = lambda: pl.BlockSpec((None, 1, d), lambda b, i: (b, 0, 0))
    const = lambda r, c: pl.BlockSpec((r, c), lambda b, i: (0, 0))
    return pl.pallas_call(
        _outproj_ln_kernel,
        grid=(bsz, t // tm),
        in_specs=[
            pl.BlockSpec((None, tm, din), lambda b, i: (b, i, 0)),
            const(din, d), row(), per_b(), const(1, d), const(1, d), per_b(), per_b(),
            const(d, LANES), const(1, LANES),
        ],
        out_specs=[row(), row(), pl.BlockSpec((None, tm, LANES), lambda b, i: (b, i, 0)),
                   pl.BlockSpec((None, None, SUBLANES, tm), lambda b, i: (b, i, 0, 0)), const(1, LANES)],
        out_shape=[jax.ShapeDtypeStruct((bsz, t, d), F32), jax.ShapeDtypeStruct((bsz, t, d), F32),
                   jax.ShapeDtypeStruct((bsz, t, LANES), F32),
                   jax.ShapeDtypeStruct((bsz, t // tm, SUBLANES, tm), jnp.int32),
                   jax.ShapeDtypeStruct((1, LANES), F32)],
        scratch_shapes=[pltpu.VMEM((1, LANES), F32)],
        compiler_params=_params("arbitrary", "arbitrary"),
        name="outproj_ln",
    )(o, w_out, x, gt, ln_g.reshape(1, d), ln_b.reshape(1, d), sc2, sh2, w_router, b_router)


def _moe_kernel(blk_e_ref, nused_ref, x_ref, wg_ref, wu_ref, wd_ref, o_ref, wg16, wu16, wd16):
    i = pl.program_id(0)
    e = blk_e_ref[i]
    e_prev = blk_e_ref[jnp.maximum(i - 1, 0)]

    @pl.when((i == 0) | (e != e_prev))
    def _():
        wg16[...] = wg_ref[...].astype(BF16)
        wu16[...] = wu_ref[...].astype(BF16)
        wd16[...] = wd_ref[...].astype(BF16)

    @pl.when(i < nused_ref[0])
    def _():
        x = x_ref[...].astype(BF16)
        gate = _dot(x, wg16[...])
        up = _dot(x, wu16[...])
        o_ref[...] = _dot((_silu(gate) * up).astype(BF16), wd16[...])

    @pl.when(i >= nused_ref[0])
    def _():
        o_ref[...] = jnp.zeros_like(o_ref)


def moe_ffn(xg, blk_e, n_used, w_gate, w_up, w_down):
    rows, d = xg.shape
    de = w_gate.shape[-1]
    n_blk = rows // MOE_BLOCK
    grid_spec = pltpu.PrefetchScalarGridSpec(
        num_scalar_prefetch=2,
        grid=(n_blk,),
        in_specs=[
            pl.BlockSpec((MOE_BLOCK, d), lambda i, be, nu: (jnp.minimum(i, nu[0] - 1), 0)),
            pl.BlockSpec((None, d, de), lambda i, be, nu: (be[i], 0, 0)),
            pl.BlockSpec((None, d, de), lambda i, be, nu: (be[i], 0, 0)),
            pl.BlockSpec((None, de, d), lambda i, be, nu: (be[i], 0, 0)),
        ],
        out_specs=pl.BlockSpec((MOE_BLOCK, d), lambda i, be, nu: (i, 0)),
        scratch_shapes=[pltpu.VMEM((d, de), BF16), pltpu.VMEM((d, de), BF16), pltpu.VMEM((de, d), BF16)],
    )
    return pl.pallas_call(
        _moe_kernel,
        grid_spec=grid_spec,
        out_shape=jax.ShapeDtypeStruct((rows, d), F32),
        compiler_params=_params("arbitrary"),
        name="moe_ffn",
    )(blk_e, n_used, xg, w_gate, w_up, w_down)


def _dispatch_kernel(ps_ref, fill_ref, idx_ref, x_ref, xg_out, dest_ref, zbuf, sem, *, tile, n_exp):
    def row_copy(r, d):
        return pltpu.make_async_copy(x_ref.at[pl.ds(r, 1)], xg_out.at[pl.ds(d, 1)], sem)

    @pl.when(pl.program_id(0) == 0)
    def _():
        zbuf[...] = jnp.zeros_like(zbuf)
        pieces = [SUBLANES << b for b in range((MOE_BLOCK // SUBLANES).bit_length() - 1)]

        def fill_copies(e, wait):
            start = fill_ref[e]
            head = (-start) & (SUBLANES - 1)
            rest = fill_ref[n_exp + e] - head
            for j in range(SUBLANES - 1):
                @pl.when(j < head)
                def _(j=j):
                    cp = pltpu.make_async_copy(zbuf.at[pl.ds(0, 1)], xg_out.at[pl.ds(start + j, 1)], sem)
                    cp.wait() if wait else cp.start()
            cur = start + head
            for size in pieces:
                @pl.when((rest & size) != 0)
                def _(cur=cur, size=size):
                    dst = xg_out.at[pl.ds(pl.multiple_of(cur, SUBLANES), size)]
                    cp = pltpu.make_async_copy(zbuf.at[pl.ds(0, size)], dst, sem)
                    cp.wait() if wait else cp.start()
                cur = cur + (rest & size)

        for e in range(n_exp):
            fill_copies(e, wait=False)
        for e in range(n_exp):
            fill_copies(e, wait=True)

        n_rows = xg_out.shape[0]
        half = zbuf.shape[0]

        def tail_copy(j, wait):
            dst = xg_out.at[pl.ds(pl.multiple_of(j * half, half), half)]
            cp = pltpu.make_async_copy(zbuf, dst, sem)
            cp.wait() if wait else cp.start()

        first_tail = fill_ref[2 * n_exp]
        lax.fori_loop(first_tail, n_rows // half, lambda j, c: (tail_copy(j, False), c)[1], 0)
        lax.fori_loop(first_tail, n_rows // half, lambda j, c: (tail_copy(j, True), c)[1], 0)

    copies = []
    for r in range(tile):
        for k in range(TOP_K):
            d = ps_ref[idx_ref[k, r]] + idx_ref[2 * TOP_K + k, r]
            dest_ref[k, r] = d
            copies.append(row_copy(r, d))
            copies[-1].start()
    for cp in copies:
        cp.wait()


def moe_dispatch(hf, idx, pad_start, fill, rows):
    n, d = hf.shape
    n_tiles, _, tile = idx.shape
    grid_spec = pltpu.PrefetchScalarGridSpec(
        num_scalar_prefetch=2,
        grid=(n_tiles,),
        in_specs=[
            pl.BlockSpec((None, SUBLANES, tile), lambda i, ps, fl: (i, 0, 0), memory_space=pltpu.SMEM),
            pl.BlockSpec((tile, d), lambda i, ps, fl: (i, 0)),
        ],
        out_specs=[
            pl.BlockSpec(memory_space=pl.ANY),
            pl.BlockSpec((None, TOP_K, tile), lambda i, ps, fl: (i, 0, 0), memory_space=pltpu.SMEM),
        ],
        scratch_shapes=[pltpu.VMEM((MOE_BLOCK // 2, d), hf.dtype), pltpu.SemaphoreType.DMA(())],
    )
    return pl.pallas_call(
        functools.partial(_dispatch_kernel, tile=tile, n_exp=fill.shape[0] // 2),
        grid_spec=grid_spec,
        out_shape=[jax.ShapeDtypeStruct((rows, d), hf.dtype),
                   jax.ShapeDtypeStruct((n_tiles, TOP_K, tile), jnp.int32)],
        compiler_params=_params("arbitrary"),
        name="moe_dispatch",
    )(pad_start, fill, idx, hf)


def _combine_ln_kernel(dest_ref, dest_next_ref, x_ref, rt_ref, gt_ref, g_ref, b_ref, yb_ref, o_ref, buf, sems,
                       *, tile):
    i = pl.program_id(0)
    slot = i % 2

    def row_copies(dests, s):
        return [pltpu.make_async_copy(yb_ref.at[pl.ds(dests[k, r], 1)], buf.at[s, k, pl.ds(r, 1)], sems.at[s])
                for r in range(tile) for k in range(TOP_K)]

    @pl.when(i == 0)
    def _():
        for cp in row_copies(dest_ref, slot):
            cp.start()

    @pl.when(i + 1 < pl.num_programs(0))
    def _():
        for cp in row_copies(dest_next_ref, 1 - slot):
            cp.start()

    for r in range(tile):
        for k in range(TOP_K):
            pltpu.make_async_copy(yb_ref.at[pl.ds(0, 1)], buf.at[slot, k, pl.ds(r, 1)], sems.at[slot]).wait()
    rt = rt_ref[...]
    y = rt[:, TOP_K:TOP_K + 1] * buf[slot, 0]
    for k in range(1, TOP_K):
        y = y + rt[:, TOP_K + k:TOP_K + k + 1] * buf[slot, k]
    o_ref[...] = _layer_norm(ALPHA * x_ref[...] + (1.0 + gt_ref[...]) * y, g_ref[...], b_ref[...])


def moe_combine_ln(dest, yb, x1, route, gt, ln_g, ln_b):
    bsz, t, d = x1.shape
    n = bsz * t
    n_tiles, _, tile = dest.shape
    per_b = t // tile
    row = lambda w: pl.BlockSpec((tile, w), lambda i: (i, 0))
    out = pl.pallas_call(
        functools.partial(_combine_ln_kernel, tile=tile),
        grid=(n_tiles,),
        in_specs=[
            pl.BlockSpec((None, TOP_K, tile), lambda i: (i, 0, 0), memory_space=pltpu.SMEM),
            pl.BlockSpec((None, TOP_K, tile), lambda i: (jnp.minimum(i + 1, n_tiles - 1), 0, 0),
                         memory_space=pltpu.SMEM),
            row(d), row(LANES),
            pl.BlockSpec((None, 1, d), lambda i: (i // per_b, 0, 0)),
            pl.BlockSpec((1, d), lambda i: (0, 0)), pl.BlockSpec((1, d), lambda i: (0, 0)),
            pl.BlockSpec(memory_space=pl.ANY),
        ],
        out_specs=row(d),
        out_shape=jax.ShapeDtypeStruct((n, d), F32),
        scratch_shapes=[pltpu.VMEM((2, TOP_K, tile, d), F32), pltpu.SemaphoreType.DMA((2,))],
        compiler_params=_params("arbitrary"),
        name="moe_combine_ln",
    )(dest, dest, x1.reshape(n, d), route.reshape(n, LANES), gt, ln_g.reshape(1, d), ln_b.reshape(1, d), yb)
    return out.reshape(bsz, t, d)


def hier_moe_ln(x1, hf, route, idx, counts, gt, ln_g, ln_b, w_gate, w_up, w_down):
    bsz, t, d = hf.shape
    n = bsz * t
    n_exp = w_gate.shape[0]
    counts = counts[0, N_GROUPS:N_GROUPS + n_exp].astype(jnp.int32)
    padded = (counts + MOE_BLOCK - 1) // MOE_BLOCK * MOE_BLOCK
    pad_end = jnp.cumsum(padded)
    pad_start = (pad_end - padded).astype(jnp.int32)
    n_blk = -(-(n * TOP_K) // MOE_BLOCK) + n_exp
    blk_first = jnp.arange(n_blk, dtype=jnp.int32) * MOE_BLOCK
    blk_e = jnp.minimum(jnp.sum((pad_end[None, :] <= blk_first[:, None]).astype(jnp.int32), axis=1), n_exp - 1)
    n_used = (pad_end[-1:] // MOE_BLOCK).astype(jnp.int32)
    idx = idx.reshape((-1,) + idx.shape[2:])
    first_tail_half = pad_end[-1:] // (MOE_BLOCK // 2)
    fill = jnp.concatenate([pad_start + counts, padded - counts, first_tail_half]).astype(jnp.int32)
    xg, dest = moe_dispatch(hf.reshape(n, d), idx, pad_start, fill, n_blk * MOE_BLOCK)
    yb = moe_ffn(xg, blk_e.astype(jnp.int32), n_used, w_gate, w_up, w_down)
    return moe_combine_ln(dest, yb, x1, route, gt, ln_g, ln_b)


def _pad_cols(w, mult=LANES):
    pad = (-w.shape[-1]) % mult
    return jnp.pad(w, ((0, 0), (0, pad))) if pad else w


def kernel(x, c, ada_w, ada_b, ln1_g, ln1_b, ln2_g, ln2_b, gdn_w_in, gdn_conv_w, gdn_a_log, gdn_dt_bias, gdn_norm_g, gdn_w_out, lru_w_in, lru_conv_w, lru_conv_b, lru_w_a, lru_b_a, lru_w_x, lru_b_x, lru_lambda, lru_w_out, nsa_w_in, nsa_pe_k, nsa_pe_v, nsa_ck_w1, nsa_ck_w2, nsa_cv_w1, nsa_cv_w2, nsa_w_out, moe_w_grp, moe_b_grp, moe_w_exp, moe_b_exp, moe_w_gate, moe_w_up, moe_w_down):
    bsz, t, d = x.shape
    depth = ada_w.shape[0]
    mod = ada_modulation(c, ada_w, ada_b).reshape(depth, bsz, 6, 1, d)
    ja = jb = jc = 0
    for i in range(depth):
        sh1, sc1, gt1, sh2, sc2, gt2 = (mod[i, :, k] for k in range(6))
        kind = i % N_MIXERS
        if kind == 0:
            p = modproj(x, sc1, sh1, _pad_cols(gdn_w_in[ja]).astype(BF16))
            o = gdn_core(p, gdn_conv_w[ja], gdn_a_log[ja], gdn_dt_bias[ja], gdn_norm_g[ja])
            w_out = gdn_w_out[ja]
            ja += 1
        elif kind == 1:
            p = modproj(x, sc1, sh1, lru_w_in[jb].astype(BF16))
            o = lru_core(p, lru_conv_w[jb], lru_conv_b[jb], lru_w_a[jb], lru_b_a[jb], lru_w_x[jb], lru_b_x[jb],
                         lru_lambda[jb])
            w_out = lru_w_out[jb]
            jb += 1
        else:
            o = nsa_mixer_core(x, sc1, sh1, nsa_w_in[jc], nsa_pe_k[jc], nsa_pe_v[jc], nsa_ck_w1[jc], nsa_ck_w2[jc],
                               nsa_cv_w1[jc], nsa_cv_w2[jc])
            w_out = nsa_w_out[jc]
            jc += 1
        w_router = _pad_cols(jnp.concatenate([moe_w_grp[i], moe_w_exp[i]], axis=1))
        b_router = _pad_cols(jnp.concatenate([moe_b_grp[i], moe_b_exp[i]])[None, :])
        x1, hf, route, idx, counts = outproj_ln(o, w_out.astype(BF16), x, gt1, ln1_g[i], ln1_b[i], sc2, sh2,
                                                w_router, b_router)
        x = hier_moe_ln(x1, hf, route, idx, counts, gt2, ln2_g[i], ln2_b[i],
                        moe_w_gate[i], moe_w_up[i], moe_w_down[i])
    return x
```

```python
import functools

import jax
import jax.numpy as jnp
from jax import lax
from jax.experimental import pallas as pl
from jax.experimental.pallas import tpu as pltpu

F32 = jnp.float32
BF16 = jnp.bfloat16

D_MODEL = 1024
DEPTH = 4
N_MIXERS = 3
LN_EPS = 1e-5
NORM_EPS = 1e-6
CONV_K = 4
ALPHA = (2.0 * DEPTH) ** 0.25
GDN_HEADS = 8
GDN_DK = 128
GDN_CHUNK = 64
GDN_QKV = 3 * GDN_HEADS * GDN_DK
LRU_WIDTH = 1024
LRU_BLOCKS = 16
LRU_BW = LRU_WIDTH // LRU_BLOCKS
RG_C = 8.0
NSA_GROUPS = 4
NSA_HPG = 4
NSA_DK = 64
NSA_Q = NSA_GROUPS * NSA_HPG * NSA_DK
NSA_KV = NSA_GROUPS * NSA_DK
CMP_LEN = 32
CMP_STRIDE = 16
SLC_LEN = 64
N_SEL = 16
WINDOW = 512
Q_BLOCK = 128
SEL_KEY_CHUNK = 512
N_GROUPS = 4
EXP_PER_GROUP = 8
N_EXPERTS = N_GROUPS * EXP_PER_GROUP
TOP_K = 2
MOE_BLOCK = 256
NEG = -1e30
BIG = 1e30
LOWEST = -3e38

LANES = 128
SUBLANES = 8
VMEM_LIMIT_BYTES = 48 * 1024 * 1024


def _params(*sem):
    return pltpu.CompilerParams(dimension_semantics=sem, vmem_limit_bytes=VMEM_LIMIT_BYTES)


def _sigmoid(x):
    return 1.0 / (1.0 + jnp.exp(-x))


def _silu(x):
    return x * _sigmoid(x)


def _softplus(x):
    return jnp.maximum(x, 0.0) + jnp.log(1.0 + jnp.exp(-jnp.abs(x)))


def _gelu(x):
    return 0.5 * x * (1.0 + jnp.tanh(0.7978845608028654 * (x + 0.044715 * (x * x * x))))


def _dot(a, b):
    return jnp.dot(a, b, preferred_element_type=F32)


def _dot_nt(a, b):
    return lax.dot_general(a, b, (((1,), (1,)), ((), ())), preferred_element_type=F32)


def _dot_tn(a, b):
    return lax.dot_general(a, b, (((0,), (0,)), ((), ())), preferred_element_type=F32)


def _split3(x):
    hi = x.astype(BF16)
    r1 = x - hi.astype(F32)
    mid = r1.astype(BF16)
    lo = (r1 - mid.astype(F32)).astype(BF16)
    return hi, mid, lo


def _dot_lhs_exact(m_bf16, x):
    hi, mid, lo = _split3(x)
    return _dot(m_bf16, hi) + _dot(m_bf16, mid) + _dot(m_bf16, lo)


def _dot_rhs_exact(x, m_bf16):
    hi, mid, lo = _split3(x)
    return _dot(hi, m_bf16) + _dot(mid, m_bf16) + _dot(lo, m_bf16)


def _dot_f32(a, b):
    ah, am, al = _split3(a)
    bh, bm, bl = _split3(b)
    return (_dot(ah, bh) + _dot(ah, bm) + _dot(am, bh)) + (_dot(ah, bl) + _dot(al, bh) + _dot(am, bm))


def _layer_norm(z, g, b):
    mu = jnp.mean(z, axis=-1, keepdims=True)
    zc = z - mu
    var = jnp.mean(zc * zc, axis=-1, keepdims=True)
    return zc * lax.rsqrt(var + LN_EPS) * g + b


def _causal_conv(xbuf, cw_ref, col0, width, tb):
    cols = slice(col0, col0 + width)
    acc = cw_ref[0:1, cols] * xbuf[SUBLANES - 3:SUBLANES - 3 + tb, cols]
    for k in range(1, CONV_K):
        off = SUBLANES - 3 + k
        acc = acc + cw_ref[k:k + 1, cols] * xbuf[off:off + tb, cols]
    return acc


def _ada_kernel(c_ref, w_ref, b_ref, o_ref):
    cond = _silu(c_ref[...])
    o_ref[...] = _dot_f32(cond, w_ref[...]) + b_ref[...]


def ada_modulation(c, ada_w, ada_b, col_tile=1536):
    depth, d, n6 = ada_w.shape
    bsz = c.shape[0]
    return pl.pallas_call(
        _ada_kernel,
        grid=(depth, n6 // col_tile),
        in_specs=[
            pl.BlockSpec((bsz, d), lambda i, j: (0, 0)),
            pl.BlockSpec((None, d, col_tile), lambda i, j: (i, 0, j)),
            pl.BlockSpec((None, 1, col_tile), lambda i, j: (i, 0, j)),
        ],
        out_specs=pl.BlockSpec((None, bsz, col_tile), lambda i, j: (i, 0, j)),
        out_shape=jax.ShapeDtypeStruct((depth, bsz, n6), F32),
        compiler_params=_params("arbitrary", "arbitrary"),
        name="ada_modulation",
    )(c, ada_w, ada_b.reshape(depth, 1, n6))


def _modproj_kernel(x_ref, sc_ref, sh_ref, w_ref, o_ref, *, col_chunk):
    h = (x_ref[...] * (1.0 + sc_ref[...]) + sh_ref[...]).astype(BF16)
    ncols = o_ref.shape[-1]
    for c0 in range(0, ncols, col_chunk):
        c1 = min(c0 + col_chunk, ncols)
        o_ref[:, c0:c1] = _dot(h, w_ref[:, c0:c1])


def modproj(x, sc, sh, w, tm=256, col_chunk=1024):
    bsz, t, d = x.shape
    ncols = w.shape[1]
    return pl.pallas_call(
        functools.partial(_modproj_kernel, col_chunk=col_chunk),
        grid=(bsz, t // tm),
        in_specs=[
            pl.BlockSpec((None, tm, d), lambda b, i: (b, i, 0)),
            pl.BlockSpec((None, 1, d), lambda b, i: (b, 0, 0)),
            pl.BlockSpec((None, 1, d), lambda b, i: (b, 0, 0)),
            pl.BlockSpec((d, ncols), lambda b, i: (0, 0)),
        ],
        out_specs=pl.BlockSpec((None, tm, ncols), lambda b, i: (b, i, 0)),
        out_shape=jax.ShapeDtypeStruct((bsz, t, ncols), F32),
        compiler_params=_params("arbitrary", "arbitrary"),
        name="modproj",
    )(x, sc, sh, w)


def _gdn_kernel(qkv_ref, z_ref, ab_ref, cw_ref, alog_ref, dtb_ref, ng_ref, o_ref, xbuf, s_ref, *, tb):
    nh, dk, chunk = GDN_HEADS, GDN_DK, GDN_CHUNK

    @pl.when(pl.program_id(1) == 0)
    def _():
        xbuf[0:SUBLANES, :] = jnp.zeros((SUBLANES, GDN_QKV), F32)
        s_ref[...] = jnp.zeros_like(s_ref)

    xbuf[SUBLANES:SUBLANES + tb, :] = qkv_ref[...]

    ab = ab_ref[...]
    g_all = -jnp.exp(alog_ref[...]) * _softplus(ab + dtb_ref[...])
    beta_all = _sigmoid(ab)
    ri = lax.broadcasted_iota(jnp.int32, (tb, tb), 0)
    ci = lax.broadcasted_iota(jnp.int32, (tb, tb), 1)
    same = (ri // chunk) == (ci // chunk)
    causal = same & (ci <= ri)
    strict = same & (ci < ri)
    eye = jnp.where(ri == ci, 1.0, 0.0)
    tri = jnp.where(causal, 1.0, 0.0).astype(BF16)
    ones_blk = jnp.where(same, 1.0, 0.0).astype(BF16)
    gc_all = _dot_lhs_exact(tri, g_all)
    glast_all = _dot_lhs_exact(ones_blk, g_all)
    gc_t = gc_all.T
    egc_all = jnp.exp(gc_all)
    kdec_all = jnp.exp(glast_all - gc_all)

    heads = range(nh)
    lmats, rhss, qk16s, qg16s, ks16s, gls = [], [], [], [], [], []
    for h in heads:
        q = _silu(_causal_conv(xbuf, cw_ref, h * dk, dk, tb))
        k = _silu(_causal_conv(xbuf, cw_ref, (nh + h) * dk, dk, tb))
        v = _silu(_causal_conv(xbuf, cw_ref, (2 * nh + h) * dk, dk, tb))
        q = q * lax.rsqrt(jnp.sum(q * q, axis=-1, keepdims=True) + NORM_EPS)
        k = k * lax.rsqrt(jnp.sum(k * k, axis=-1, keepdims=True) + NORM_EPS)
        beta = beta_all[:, nh + h:nh + h + 1]
        eg = egc_all[:, h:h + 1]
        decay = jnp.exp(jnp.where(causal, gc_all[:, h:h + 1] - gc_t[h:h + 1, :], NEG))
        kb = k * beta
        k16 = k.astype(BF16)
        qs = q * (dk ** -0.5)
        lmats.append(jnp.where(strict, _dot_nt(kb.astype(BF16), k16) * decay, 0.0))
        rhss.append(jnp.concatenate([v * beta, kb * eg], axis=1).astype(BF16))
        qk16s.append((_dot_nt(qs.astype(BF16), k16) * decay).astype(BF16))
        qg16s.append((qs * eg).astype(BF16))
        ks16s.append((k * kdec_all[:, h:h + 1]).astype(BF16))
        gls.append(glast_all[:, h:h + 1])
    xbuf[0:SUBLANES, :] = xbuf[tb:tb + SUBLANES, :]

    xpows = lmats
    tinvs = [eye - lm for lm in lmats]
    for _ in range(5):
        xpows = [_dot(x16, x16) for x16 in [x.astype(BF16) for x in xpows]]
        tinvs = [t + _dot(t.astype(BF16), x.astype(BF16)) for t, x in zip(tinvs, xpows)]
    uws = [_dot(t.astype(BF16), rhs) for t, rhs in zip(tinvs, rhss)]
    us = [uw[:, :dk] for uw in uws]
    w16s = [uw[:, dk:].astype(BF16) for uw in uws]

    states = [s_ref[h] for h in heads]
    o_parts = [[] for _ in heads]
    vn_parts = [[] for _ in heads]
    for c in range(tb // chunk):
        rows = slice(c * chunk, (c + 1) * chunk)
        for h in heads:
            s16 = states[h].astype(BF16)
            vn16 = (us[h][rows] - _dot(w16s[h][rows], s16)).astype(BF16)
            o_parts[h].append(_dot(qg16s[h][rows], s16))
            states[h] = (states[h] * jnp.exp(gls[h][c * chunk:c * chunk + 1, :])
                         + _dot_tn(ks16s[h][rows], vn16))
            vn_parts[h].append(vn16)
    for h in heads:
        s_ref[h] = states[h]
        o = jnp.concatenate(o_parts[h], axis=0) + _dot(qk16s[h], jnp.concatenate(vn_parts[h], axis=0))
        o = o * lax.rsqrt(jnp.mean(o * o, axis=-1, keepdims=True) + NORM_EPS) * ng_ref[...]
        o_ref[:, h * dk:(h + 1) * dk] = o * _silu(z_ref[:, h * dk:(h + 1) * dk])


def gdn_core(p, conv_w, a_log, dt_bias, norm_g, tb=256):
    bsz, t, _ = p.shape
    nh, dk = GDN_HEADS, GDN_DK
    v_w = nh * dk
    alog = jnp.zeros((1, LANES), F32).at[0, :nh].set(a_log)
    dtb = jnp.zeros((1, LANES), F32).at[0, :nh].set(dt_bias)
    ab_blk = (GDN_QKV + v_w) // LANES
    return pl.pallas_call(
        functools.partial(_gdn_kernel, tb=tb),
        grid=(bsz, t // tb),
        in_specs=[
            pl.BlockSpec((None, tb, GDN_QKV), lambda b, i: (b, i, 0)),
            pl.BlockSpec((None, tb, v_w), lambda b, i: (b, i, GDN_QKV // v_w)),
            pl.BlockSpec((None, tb, LANES), lambda b, i: (b, i, ab_blk)),
            pl.BlockSpec((CONV_K, GDN_QKV), lambda b, i: (0, 0)),
            pl.BlockSpec((1, LANES), lambda b, i: (0, 0)),
            pl.BlockSpec((1, LANES), lambda b, i: (0, 0)),
            pl.BlockSpec((1, dk), lambda b, i: (0, 0)),
        ],
        out_specs=pl.BlockSpec((None, tb, v_w), lambda b, i: (b, i, 0)),
        out_shape=jax.ShapeDtypeStruct((bsz, t, v_w), F32),
        scratch_shapes=[pltpu.VMEM((tb + SUBLANES, GDN_QKV), F32), pltpu.VMEM((nh, dk, dk), F32)],
        compiler_params=_params("arbitrary", "arbitrary"),
        name="gdn_core",
    )(p, p, p, conv_w, alog, dtb, norm_g.reshape(1, dk))


def _lru_kernel(gate_ref, xb_ref, cw_ref, cb_ref, wax_ref, ba_ref, bx_ref, lam_ref, o_ref, xbuf, h_ref, *, tb):
    width = LRU_WIDTH
    grp = 2 * LANES

    @pl.when(pl.program_id(1) == 0)
    def _():
        xbuf[0:SUBLANES, :] = jnp.zeros((SUBLANES, width), F32)
        h_ref[...] = jnp.zeros_like(h_ref)

    xbuf[SUBLANES:SUBLANES + tb, :] = xb_ref[...]
    row = lax.broadcasted_iota(jnp.int32, (tb, grp), 0)

    for j in range(width // grp):
        cols = slice(j * grp, (j + 1) * grp)
        xc = _causal_conv(xbuf, cw_ref, j * grp, grp, tb) + cb_ref[:, cols]
        res = _dot(xc.astype(BF16), wax_ref[j])
        r = _sigmoid(res[:, :grp] + ba_ref[:, cols])
        ig = _sigmoid(res[:, grp:] + bx_ref[:, cols])
        log_a = -RG_C * r * _softplus(-lam_ref[:, cols])
        a = jnp.exp(log_a)
        mult = jnp.sqrt(jnp.maximum(1.0 - jnp.exp(2.0 * log_a), 0.0))
        u = mult * ig * xc
        s = 1
        while s < tb:
            a_sh = jnp.where(row < s, 1.0, pltpu.roll(a, s, 0))
            u_sh = jnp.where(row < s, 0.0, pltpu.roll(u, s, 0))
            u = a * u_sh + u
            a = a * a_sh
            s *= 2
        hs = u + a * h_ref[:, cols]
        h_ref[:, cols] = hs[tb - 1:tb, :]
        o_ref[:, cols] = hs * _gelu(gate_ref[:, cols])

    xbuf[0:SUBLANES, :] = xbuf[tb:tb + SUBLANES, :]


def _block_diag_groups(w):
    per = (2 * LANES) // LRU_BW
    wg = w.reshape(LRU_BLOCKS // per, per, LRU_BW, LRU_BW)
    eye = jnp.eye(per, dtype=w.dtype)
    return jnp.einsum('gaij,ab->gaibj', wg, eye).reshape(LRU_BLOCKS // per, per * LRU_BW, per * LRU_BW)


def lru_core(p, conv_w, conv_b, w_a, b_a, w_x, b_x, lam, tb=256):
    bsz, t, _ = p.shape
    width = LRU_WIDTH
    grp = 2 * LANES
    wax = jnp.concatenate([_block_diag_groups(w_a), _block_diag_groups(w_x)], axis=-1).astype(BF16)
    vec = lambda a: a.reshape(1, width)
    return pl.pallas_call(
        functools.partial(_lru_kernel, tb=tb),
        grid=(bsz, t // tb),
        in_specs=[
            pl.BlockSpec((None, tb, width), lambda b, i: (b, i, 0)),
            pl.BlockSpec((None, tb, width), lambda b, i: (b, i, 1)),
            pl.BlockSpec((CONV_K, width), lambda b, i: (0, 0)),
            pl.BlockSpec((1, width), lambda b, i: (0, 0)),
            pl.BlockSpec((width // grp, grp, 2 * grp), lambda b, i: (0, 0, 0)),
            pl.BlockSpec((1, width), lambda b, i: (0, 0)),
            pl.BlockSpec((1, width), lambda b, i: (0, 0)),
            pl.BlockSpec((1, width), lambda b, i: (0, 0)),
        ],
        out_specs=pl.BlockSpec((None, tb, width), lambda b, i: (b, i, 0)),
        out_shape=jax.ShapeDtypeStruct((bsz, t, width), F32),
        scratch_shapes=[pltpu.VMEM((tb + SUBLANES, width), F32), pltpu.VMEM((1, width), F32)],
        compiler_params=_params("arbitrary", "arbitrary"),
        name="lru_core",
    )(p, p, conv_w, vec(conv_b), wax, vec(b_a), vec(b_x), vec(lam))


DEN_LANE = NSA_DK


def _nsa_proj_kernel(x_ref, sc_ref, sh_ref, wf_ref, wkv_ref, bkv_ref, pf_ref, pkv_ref, *, tm, n_slc, pad_tiles):
    g = NSA_GROUPS
    slab = n_slc + LANES
    i = pl.program_id(1)

    @pl.when(i < pad_tiles)
    def _():
        lane = lax.broadcasted_iota(jnp.int32, pkv_ref.shape, 1)
        kw0 = g * slab + g * LANES
        flag = (lane >= kw0) & (lane < kw0 + g * LANES) & ((lane - kw0) % LANES == DEN_LANE)
        pkv_ref[...] = jnp.where(flag, 1.0, 0.0).astype(pkv_ref.dtype)

    @pl.when(i >= pad_tiles)
    def _():
        h = (x_ref[...] * (1.0 + sc_ref[...]) + sh_ref[...]).astype(BF16)
        nf = pf_ref.shape[-1]
        for c0 in range(0, nf, 1024):
            c1 = min(c0 + 1024, nf)
            pf_ref[:, c0:c1] = _dot(h, wf_ref[:, c0:c1])
        tok = (i - pad_tiles) * tm + lax.broadcasted_iota(jnp.int32, (tm, n_slc), 0)
        onehot = jnp.where(tok // SLC_LEN == lax.broadcasted_iota(jnp.int32, (tm, n_slc), 1), 1.0, 0.0)
        for c in range(g):
            pkv_ref[:, c * slab:c * slab + n_slc] = onehot.astype(pkv_ref.dtype)
            pkv_ref[:, c * slab + n_slc:(c + 1) * slab] = _dot(h, wkv_ref[:, c * LANES:(c + 1) * LANES]).astype(
                pkv_ref.dtype)
        rest = _dot(h, wkv_ref[:, g * LANES:]) + bkv_ref[...]
        pkv_ref[:, g * slab:] = rest.astype(pkv_ref.dtype)


def nsa_proj(x, sc, sh, w_in, tm=256):
    bsz, t, d = x.shape
    g, dk = NSA_GROUPS, NSA_DK
    n_slc = t // SLC_LEN
    pad_tiles = WINDOW // tm
    pad_lanes = lambda w: jnp.pad(w, ((0, 0),) * (w.ndim - 1) + ((0, LANES - w.shape[-1]),))
    streams = pad_lanes(w_in[:, NSA_Q:NSA_Q + 6 * NSA_KV].reshape(d, 6, g, dk))
    gate_w = pad_lanes(w_in[:, NSA_Q + 6 * NSA_KV:].reshape(d, g, 3 * NSA_HPG))
    wf = jnp.concatenate([w_in[:, :NSA_Q], streams[:, 0:2].reshape(d, -1), gate_w.reshape(d, -1)], axis=1)
    wkv = streams[:, 2:6].reshape(d, -1)
    den = jnp.zeros((LANES,), F32).at[DEN_LANE].set(1.0)
    bkv = jnp.concatenate([jnp.tile(den, g), jnp.zeros((g * LANES,), F32), jnp.tile(den, g)])[None, :]
    nf, nkv = wf.shape[1], g * (n_slc + LANES) + 3 * g * LANES
    tok_tile = lambda b, i: (b, jnp.maximum(i - pad_tiles, 0), 0)
    return pl.pallas_call(
        functools.partial(_nsa_proj_kernel, tm=tm, n_slc=n_slc, pad_tiles=pad_tiles),
        grid=(bsz, t // tm + pad_tiles),
        in_specs=[
            pl.BlockSpec((None, tm, d), tok_tile),
            pl.BlockSpec((None, 1, d), lambda b, i: (b, 0, 0)),
            pl.BlockSpec((None, 1, d), lambda b, i: (b, 0, 0)),
            pl.BlockSpec(wf.shape, lambda b, i: (0, 0)),
            pl.BlockSpec(wkv.shape, lambda b, i: (0, 0)),
            pl.BlockSpec(bkv.shape, lambda b, i: (0, 0)),
        ],
        out_specs=[pl.BlockSpec((None, tm, nf), tok_tile),
                   pl.BlockSpec((None, tm, nkv), lambda b, i: (b, i, 0))],
        out_shape=[jax.ShapeDtypeStruct((bsz, t, nf), F32),
                   jax.ShapeDtypeStruct((bsz, t + WINDOW, nkv), BF16)],
        compiler_params=_params("arbitrary", "arbitrary"),
        name="nsa_proj",
    )(x, sc, sh, wf.astype(BF16), wkv.astype(BF16), bkv)


def _cmp_kernel(x_ref, pe_ref, w1a_ref, w1b_ref, w2_ref, o_ref):
    nseg = o_ref.shape[0]
    first = second = None
    for l in range(CMP_STRIDE):
        xl = x_ref[pl.ds(l, nseg, stride=CMP_STRIDE), :]
        a = _dot((xl + pe_ref[l:l + 1, :]).astype(BF16), w1a_ref[l])
        b = _dot((xl + pe_ref[CMP_STRIDE + l:CMP_STRIDE + l + 1, :]).astype(BF16), w1b_ref[l])
        first = a if first is None else first + a
        second = b if second is None else second + b
    hid = _gelu(first + pltpu.roll(second, nseg - 1, 0))
    o_ref[...] = _dot(hid.astype(BF16), w2_ref[...]).astype(o_ref.dtype)


def nsa_compress(pf, pe, w1, w2):
    bsz, t, _ = pf.shape
    g, dk = NSA_GROUPS, NSA_DK
    nseg = t // CMP_STRIDE
    hid = w1.shape[-1]
    pad_mid = lambda w: jnp.pad(w, ((0, 0), (0, 0), (0, LANES - dk), (0, 0)))
    w1r = w1.reshape(2, CMP_LEN, dk, hid)
    w1a = pad_mid(w1r[:, :CMP_STRIDE]).astype(BF16)
    w1b = pad_mid(w1r[:, CMP_STRIDE:]).astype(BF16)
    pe_p = jnp.pad(pe, ((0, 0), (0, 0), (0, LANES - dk)))
    w2p = jnp.pad(w2, ((0, 0), (0, 0), (0, LANES - dk))).astype(BF16)
    first_slab = NSA_Q // LANES
    return pl.pallas_call(
        _cmp_kernel,
        grid=(2, bsz, g),
        in_specs=[
            pl.BlockSpec((None, t, LANES), lambda a, b, c: (b, 0, first_slab + a * g + c)),
            pl.BlockSpec((None, CMP_LEN, LANES), lambda a, b, c: (a, 0, 0)),
            pl.BlockSpec((None, CMP_STRIDE, LANES, hid), lambda a, b, c: (a, 0, 0, 0)),
            pl.BlockSpec((None, CMP_STRIDE, LANES, hid), lambda a, b, c: (a, 0, 0, 0)),
            pl.BlockSpec((None, hid, LANES), lambda a, b, c: (a, 0, 0)),
        ],
        out_specs=pl.BlockSpec((None, None, None, nseg, LANES), lambda a, b, c: (a, b, c, 0, 0)),
        out_shape=jax.ShapeDtypeStruct((2, bsz, g, nseg, LANES), BF16),
        compiler_params=_params("arbitrary", "arbitrary", "arbitrary"),
        name="nsa_compress",
    )(pf, pe_p, w1a, w1b, w2p)


def _nsa_kernel(q_ref, gate_ref, kc_ref, vc_ref, ks_ref, vs_ref, kw_ref, vw_ref, wbias_ref, o_ref, *, t_len):
    p_heads, dk, qb = NSA_HPG, NSA_DK, Q_BLOCK
    rows = p_heads * qb
    n_cmp_rows = kc_ref.shape[0]
    n_slc = t_len // SLC_LEN
    n_sel = min(N_SEL, n_slc)
    kc_len = SEL_KEY_CHUNK
    t0 = pl.program_id(2) * qb

    qg = q_ref[...]
    halves = []
    for p in range(p_heads):
        tile = qg[:, (p // 2) * LANES:(p // 2 + 1) * LANES]
        halves.append(tile if p % 2 == 0 else pltpu.roll(tile, dk, 1))
    lane = lax.broadcasted_iota(jnp.int32, (rows, LANES), 1)
    qz = jnp.where(lane < dk, jnp.concatenate(halves, axis=0) * (dk ** -0.5), 0.0)
    qz16 = qz.astype(BF16)
    tpos = t0 + (lax.broadcasted_iota(jnp.int32, (rows, 1), 0) % qb)

    w_len = qb + WINDOW
    w0 = pl.multiple_of(t0, qb)
    qw16 = jnp.where(lane == DEN_LANE, NEG, qz).astype(BF16)
    s_w = _dot_nt(qw16, kw_ref[pl.ds(w0, w_len), :]) + wbias_ref[...]
    e_w = jnp.exp(s_w - jnp.max(s_w, axis=-1, keepdims=True))
    acc_w = _dot(e_w.astype(BF16), vw_ref[pl.ds(w0, w_len), :])
    o_w = acc_w * (1.0 / acc_w[:, DEN_LANE:DEN_LANE + 1])

    s_c = _dot_nt(qz16, kc_ref[...])
    n_idx = lax.broadcasted_iota(jnp.int32, (rows, n_cmp_rows), 1)
    vis = (n_idx * CMP_STRIDE + (CMP_LEN - 1)) <= tpos
    s_c = jnp.where(vis, s_c, NEG)
    e_c = jnp.where(vis, jnp.exp(s_c - jnp.max(s_c, axis=-1, keepdims=True)), 0.0)
    l_c = jnp.sum(e_c, axis=-1, keepdims=True)
    p_c = e_c * (1.0 / jnp.where(l_c > 0.0, l_c, 1.0))
    o_c = _dot(p_c.astype(BF16), vc_ref[...])

    p_sum = p_c[0:qb]
    for p in range(1, p_heads):
        p_sum = p_sum + p_c[p * qb:(p + 1) * qb]
    sj = lax.broadcasted_iota(jnp.int32, (n_slc, n_cmp_rows), 0) * SLC_LEN
    cn = lax.broadcasted_iota(jnp.int32, (n_slc, n_cmp_rows), 1) * CMP_STRIDE
    ov = jnp.maximum(jnp.minimum(cn + CMP_LEN, sj + SLC_LEN) - jnp.maximum(cn, sj), 0)
    overlap_t = (ov.astype(F32) * (1.0 / CMP_LEN)).astype(BF16)
    imp = None
    for piece in _split3(p_sum):
        part = _dot_nt(overlap_t, piece)
        imp = part if imp is None else imp + part
    s_idx = lax.broadcasted_iota(jnp.int32, (n_slc, qb), 0)
    tq = t0 + lax.broadcasted_iota(jnp.int32, (n_slc, qb), 1)
    cur = tq // SLC_LEN
    forced = (s_idx == 0) | (s_idx == cur) | (s_idx == cur - 1)
    imp = jnp.where(forced, BIG, imp)
    imp = jnp.where(s_idx * SLC_LEN <= tq, imp, -BIG)
    s_idx_f = s_idx.astype(F32)
    sel_t = jnp.zeros((n_slc, qb), F32)
    for _ in range(n_sel):
        m = jnp.max(imp, axis=0, keepdims=True)
        first = jnp.min(jnp.where(imp == m, s_idx_f, float(n_slc)), axis=0, keepdims=True)
        pick = s_idx_f == first
        sel_t = jnp.where(pick, 1.0, sel_t)
        imp = jnp.where(pick, LOWEST, imp)
    own_blk = t0 // SLC_LEN
    sel_t = jnp.where((s_idx == own_blk) | (s_idx == own_blk + 1) | (s_idx * SLC_LEN > tq), 0.0, sel_t)
    selneg = ((sel_t.T - 1.0) * BIG).astype(BF16)
    q_aug = jnp.concatenate([jnp.concatenate([selneg] * p_heads, axis=0), qz16], axis=1)

    own0 = pl.multiple_of(WINDOW + t0, qb)
    s_own = _dot_nt(qz16, ks_ref[pl.ds(own0, qb), n_slc:n_slc + LANES])
    s_own = jnp.where((t0 + lax.broadcasted_iota(jnp.int32, (rows, qb), 1)) <= tpos, s_own, NEG)
    m_own = jnp.max(s_own, axis=-1, keepdims=True)
    acc_own = _dot(jnp.exp((s_own - m_own).astype(BF16)), vs_ref[pl.ds(own0, qb), :])

    def chunk_rows(c):
        return pl.ds(pl.multiple_of(WINDOW + c * kc_len, kc_len), kc_len)

    def sel_update(c, carry, scores):
        m, acc = carry
        m_new = jnp.maximum(m, jnp.max(scores, axis=-1, keepdims=True))
        prob = jnp.exp((scores - m_new).astype(BF16))
        return m_new, jnp.exp(m - m_new) * acc + _dot(prob, vs_ref[chunk_rows(c), :])

    def sel_scores(c):
        return _dot_nt(q_aug, ks_ref[chunk_rows(c), :])

    def sel_pair(i, carry):
        s_a = sel_scores(2 * i)
        s_b = sel_scores(2 * i + 1)
        return sel_update(2 * i + 1, sel_update(2 * i, carry, s_a), s_b)

    n_chunks = t0 // kc_len + 1
    carry = lax.fori_loop(0, n_chunks // 2, sel_pair, (m_own, acc_own))
    _, acc_s = lax.cond(n_chunks % 2 == 1,
                        lambda cr: sel_update(n_chunks - 1, cr, sel_scores(n_chunks - 1)),
                        lambda cr: cr, carry)
    o_s = acc_s * (1.0 / acc_s[:, DEN_LANE:DEN_LANE + 1])

    gts = _sigmoid(gate_ref[...])
    outs = []
    for p in range(p_heads):
        r = slice(p * qb, (p + 1) * qb)
        outs.append(gts[:, 3 * p:3 * p + 1] * o_c[r] + gts[:, 3 * p + 1:3 * p + 2] * o_s[r]
                    + gts[:, 3 * p + 2:3 * p + 3] * o_w[r])
    lane_o = lax.broadcasted_iota(jnp.int32, (qb, LANES), 1)
    for t in range(p_heads // 2):
        o_ref[:, t * LANES:(t + 1) * LANES] = jnp.where(lane_o < dk, outs[2 * t], pltpu.roll(outs[2 * t + 1], dk, 1))


def nsa_attention(pf, pkv, kvcmp):
    bsz, t, _ = pf.shape
    g, dk, qb = NSA_GROUPS, NSA_DK, Q_BLOCK
    rows = NSA_HPG * qb
    w_len = qb + WINDOW
    qw = NSA_HPG * dk
    slab = t // SLC_LEN + LANES
    gate_slab = NSA_Q // LANES + 2 * g
    kv_base = g * slab // LANES
    ri = jnp.arange(rows, dtype=jnp.int32)[:, None] % qb
    ki = jnp.arange(w_len, dtype=jnp.int32)[None, :]
    wbias = jnp.where((ki > ri) & (ki <= ri + WINDOW), 0.0, NEG).astype(F32)
    cmp_spec = pl.BlockSpec((None, None) + kvcmp.shape[3:], lambda b, c, j: (b, c, 0, 0))
    kv_spec = lambda first: pl.BlockSpec((None, t + WINDOW, LANES), lambda b, c, j: (b, 0, first + c))
    return pl.pallas_call(
        functools.partial(_nsa_kernel, t_len=t),
        grid=(bsz, g, t // qb),
        in_specs=[
            pl.BlockSpec((None, qb, qw), lambda b, c, j: (b, j, c)),
            pl.BlockSpec((None, qb, LANES), lambda b, c, j: (b, j, gate_slab + c)),
            cmp_spec, cmp_spec,
            pl.BlockSpec((None, t + WINDOW, slab), lambda b, c, j: (b, 0, c)),
            kv_spec(kv_base), kv_spec(kv_base + g), kv_spec(kv_base + 2 * g),
            pl.BlockSpec((rows, w_len), lambda b, c, j: (0, 0)),
        ],
        out_specs=pl.BlockSpec((None, qb, qw), lambda b, c, j: (b, j, c)),
        out_shape=jax.ShapeDtypeStruct((bsz, t, g * qw), F32),
        compiler_params=_params("arbitrary", "arbitrary", "arbitrary"),
        name="nsa_attention",
    )(pf, pf, kvcmp[0], kvcmp[1], pkv, pkv, pkv, pkv, wbias)


def nsa_mixer_core(x, sc, sh, w_in, pe_k, pe_v, ck_w1, ck_w2, cv_w1, cv_w2):
    pf, pkv = nsa_proj(x, sc, sh, w_in)
    kvcmp = nsa_compress(pf, jnp.stack([pe_k, pe_v]), jnp.stack([ck_w1, cv_w1]), jnp.stack([ck_w2, cv_w2]))
    return nsa_attention(pf, pkv, kvcmp)


def _outproj_ln_kernel(o_ref, w_ref, x_ref, gt_ref, g_ref, b_ref, sc_ref, sh_ref, wr_ref, br_ref,
                       x1_ref, hf_ref, rt_ref, idx_ref, cnt_ref, cnt_acc):
    @pl.when((pl.program_id(0) == 0) & (pl.program_id(1) == 0))
    def _():
        cnt_acc[...] = jnp.zeros_like(cnt_acc)

    y = _dot(o_ref[...].astype(BF16), w_ref[...])
    x1 = _layer_norm(ALPHA * x_ref[...] + (1.0 + gt_ref[...]) * y, g_ref[...], b_ref[...])
    x1_ref[...] = x1
    hf = x1 * (1.0 + sc_ref[...]) + sh_ref[...]
    hf_ref[...] = hf

    logits = _dot_f32(hf, wr_ref[...]) + br_ref[...]
    lane = lax.broadcasted_iota(jnp.int32, logits.shape, 1).astype(F32)
    is_grp = lane < N_GROUPS
    gl = jnp.where(is_grp, logits, LOWEST)
    gmax = jnp.max(gl, axis=-1, keepdims=True)
    gi = jnp.min(jnp.where(gl == gmax, lane, float(LANES)), axis=-1, keepdims=True)
    gp = 1.0 / jnp.sum(jnp.where(is_grp, jnp.exp(logits - gmax), 0.0), axis=-1, keepdims=True)
    lo = N_GROUPS + EXP_PER_GROUP * gi
    el = jnp.where((lane >= lo) & (lane < lo + EXP_PER_GROUP), logits, LOWEST)
    m1 = jnp.max(el, axis=-1, keepdims=True)
    i1 = jnp.min(jnp.where(el == m1, lane, float(LANES)), axis=-1, keepdims=True)
    el2 = jnp.where(lane == i1, LOWEST, el)
    m2 = jnp.max(el2, axis=-1, keepdims=True)
    i2 = jnp.min(jnp.where(el2 == m2, lane, float(LANES)), axis=-1, keepdims=True)
    e2 = jnp.exp(m2 - m1)
    g1 = gp / (1.0 + e2)
    g2 = gp * e2 / (1.0 + e2)

    tm = logits.shape[0]
    onehot = jnp.where((lane == i1) | (lane == i2), 1.0, 0.0)
    ri = lax.broadcasted_iota(jnp.int32, (tm, tm), 0)
    ci = lax.broadcasted_iota(jnp.int32, (tm, tm), 1)
    before = jnp.where(ci < ri, 1.0, 0.0).astype(BF16)
    seen = _dot(before, onehot.astype(BF16)) + cnt_acc[...]
    rank1 = jnp.sum(jnp.where(lane == i1, seen, 0.0), axis=-1, keepdims=True)
    rank2 = jnp.sum(jnp.where(lane == i2, seen, 0.0), axis=-1, keepdims=True)
    cnt_acc[...] = cnt_acc[...] + jnp.sum(onehot, axis=0, keepdims=True)
    cnt_ref[...] = cnt_acc[...]

    vals = (i1 - N_GROUPS, i2 - N_GROUPS, g1, g2, rank1, rank2)
    out = jnp.zeros_like(logits)
    for pos, val in enumerate(vals):
        out = jnp.where(lane == float(pos), val, out)
    rt_ref[...] = out
    idx_ref[...] = out.T[0:SUBLANES, :].astype(jnp.int32)


def outproj_ln(o, w_out, x, gt, ln_g, ln_b, sc2, sh2, w_router, b_router, tm=256):
    bsz, t, d = x.shape
    din = o.shape[-1]
    row = lambda: pl.BlockSpec((None, tm, d), lambda b, i: (b, i, 0))
    per_b = lambda: pl.BlockSpec((None, 1, d), lambda b, i: (b, 0, 0))
    const = lambda r, c: pl.BlockSpec((r, c), lambda b, i: (0, 0))
    return pl.pallas_call(
        _outproj_ln_kernel,
        grid=(bsz, t // tm),
        in_specs=[
            pl.BlockSpec((None, tm, din), lambda b, i: (b, i, 0)),
            const(din, d), row(), per_b(), const(1, d), const(1, d), per_b(), per_b(),
            const(d, LANES), const(1, LANES),
        ],
        out_specs=[row(), row(), pl.BlockSpec((None, tm, LANES), lambda b, i: (b, i, 0)),
                   pl.BlockSpec((None, None, SUBLANES, tm), lambda b, i: (b, i, 0, 0)), const(1, LANES)],
        out_shape=[jax.ShapeDtypeStruct((bsz, t, d), F32), jax.ShapeDtypeStruct((bsz, t, d), F32),
                   jax.ShapeDtypeStruct((bsz, t, LANES), F32),
                   jax.ShapeDtypeStruct((bsz, t // tm, SUBLANES, tm), jnp.int32),
                   jax.ShapeDtypeStruct((1, LANES), F32)],
        scratch_shapes=[pltpu.VMEM((1, LANES), F32)],
        compiler_params=_params("arbitrary", "arbitrary"),
        name="outproj_ln",
    )(o, w_out, x, gt, ln_g.reshape(1, d), ln_b.reshape(1, d), sc2, sh2, w_router, b_router)


def _moe_kernel(blk_e_ref, nused_ref, x_ref, wg_ref, wu_ref, wd_ref, o_ref, wg16, wu16, wd16):
    i = pl.program_id(0)
    e = blk_e_ref[i]
    e_prev = blk_e_ref[jnp.maximum(i - 1, 0)]

    @pl.when((i == 0) | (e != e_prev))
    def _():
        wg16[...] = wg_ref[...].astype(BF16)
        wu16[...] = wu_ref[...].astype(BF16)
        wd16[...] = wd_ref[...].astype(BF16)

    @pl.when(i < nused_ref[0])
    def _():
        x = x_ref[...].astype(BF16)
        gate = _dot(x, wg16[...])
        up = _dot(x, wu16[...])
        o_ref[...] = _dot((_silu(gate) * up).astype(BF16), wd16[...])

    @pl.when(i >= nused_ref[0])
    def _():
        o_ref[...] = jnp.zeros_like(o_ref)


def moe_ffn(xg, blk_e, n_used, w_gate, w_up, w_down):
    rows, d = xg.shape
    de = w_gate.shape[-1]
    n_blk = rows // MOE_BLOCK
    grid_spec = pltpu.PrefetchScalarGridSpec(
        num_scalar_prefetch=2,
        grid=(n_blk,),
        in_specs=[
            pl.BlockSpec((MOE_BLOCK, d), lambda i, be, nu: (jnp.minimum(i, nu[0] - 1), 0)),
            pl.BlockSpec((None, d, de), lambda i, be, nu: (be[i], 0, 0)),
            pl.BlockSpec((None, d, de), lambda i, be, nu: (be[i], 0, 0)),
            pl.BlockSpec((None, de, d), lambda i, be, nu: (be[i], 0, 0)),
        ],
        out_specs=pl.BlockSpec((MOE_BLOCK, d), lambda i, be, nu: (i, 0)),
        scratch_shapes=[pltpu.VMEM((d, de), BF16), pltpu.VMEM((d, de), BF16), pltpu.VMEM((de, d), BF16)],
    )
    return pl.pallas_call(
        _moe_kernel,
        grid_spec=grid_spec,
        out_shape=jax.ShapeDtypeStruct((rows, d), F32),
        compiler_params=_params("arbitrary"),
        name="moe_ffn",
    )(blk_e, n_used, xg, w_gate, w_up, w_down)


def _dispatch_kernel(ps_ref, fill_ref, idx_ref, x_ref, xg_out, dest_ref, zbuf, sem, *, tile, n_exp):
    def row_copy(r, d):
        return pltpu.make_async_copy(x_ref.at[pl.ds(r, 1)], xg_out.at[pl.ds(d, 1)], sem)

    @pl.when(pl.program_id(0) == 0)
    def _():
        zbuf[...] = jnp.zeros_like(zbuf)
        pieces = [SUBLANES << b for b in range((MOE_BLOCK // SUBLANES).bit_length() - 1)]

        def fill_copies(e, wait):
            start = fill_ref[e]
            head = (-start) & (SUBLANES - 1)
            rest = fill_ref[n_exp + e] - head
            for j in range(SUBLANES - 1):
                @pl.when(j < head)
                def _(j=j):
                    cp = pltpu.make_async_copy(zbuf.at[pl.ds(0, 1)], xg_out.at[pl.ds(start + j, 1)], sem)
                    cp.wait() if wait else cp.start()
            cur = start + head
            for size in pieces:
                @pl.when((rest & size) != 0)
                def _(cur=cur, size=size):
                    dst = xg_out.at[pl.ds(pl.multiple_of(cur, SUBLANES), size)]
                    cp = pltpu.make_async_copy(zbuf.at[pl.ds(0, size)], dst, sem)
                    cp.wait() if wait else cp.start()
                cur = cur + (rest & size)

        for e in range(n_exp):
            fill_copies(e, wait=False)
        for e in range(n_exp):
            fill_copies(e, wait=True)

        n_rows = xg_out.shape[0]
        half = zbuf.shape[0]

        def tail_copy(j, wait):
            dst = xg_out.at[pl.ds(pl.multiple_of(j * half, half), half)]
            cp = pltpu.make_async_copy(zbuf, dst, sem)
            cp.wait() if wait else cp.start()

        first_tail = fill_ref[2 * n_exp]
        lax.fori_loop(first_tail, n_rows // half, lambda j, c: (tail_copy(j, False), c)[1], 0)
        lax.fori_loop(first_tail, n_rows // half, lambda j, c: (tail_copy(j, True), c)[1], 0)

    copies = []
    for r in range(tile):
        for k in range(TOP_K):
            d = ps_ref[idx_ref[k, r]] + idx_ref[2 * TOP_K + k, r]
            dest_ref[k, r] = d
            copies.append(row_copy(r, d))
            copies[-1].start()
    for cp in copies:
        cp.wait()


def moe_dispatch(hf, idx, pad_start, fill, rows):
    n, d = hf.shape
    n_tiles, _, tile = idx.shape
    grid_spec = pltpu.PrefetchScalarGridSpec(
        num_scalar_prefetch=2,
        grid=(n_tiles,),
        in_specs=[
            pl.BlockSpec((None, SUBLANES, tile), lambda i, ps, fl: (i, 0, 0), memory_space=pltpu.SMEM),
            pl.BlockSpec((tile, d), lambda i, ps, fl: (i, 0)),
        ],
        out_specs=[
            pl.BlockSpec(memory_space=pl.ANY),
            pl.BlockSpec((None, TOP_K, tile), lambda i, ps, fl: (i, 0, 0), memory_space=pltpu.SMEM),
        ],
        scratch_shapes=[pltpu.VMEM((MOE_BLOCK // 2, d), hf.dtype), pltpu.SemaphoreType.DMA(())],
    )
    return pl.pallas_call(
        functools.partial(_dispatch_kernel, tile=tile, n_exp=fill.shape[0] // 2),
        grid_spec=grid_spec,
        out_shape=[jax.ShapeDtypeStruct((rows, d), hf.dtype),
                   jax.ShapeDtypeStruct((n_tiles, TOP_K, tile), jnp.int32)],
        compiler_params=_params("arbitrary"),
        name="moe_dispatch",
    )(pad_start, fill, idx, hf)


def _combine_ln_kernel(dest_ref, dest_next_ref, x_ref, rt_ref, gt_ref, g_ref, b_ref, yb_ref, o_ref, buf, sems,
                       *, tile):
    i = pl.program_id(0)
    slot = i % 2

    def row_copies(dests, s):
        return [pltpu.make_async_copy(yb_ref.at[pl.ds(dests[k, r], 1)], buf.at[s, k, pl.ds(r, 1)], sems.at[s])
                for r in range(tile) for k in range(TOP_K)]

    @pl.when(i == 0)
    def _():
        for cp in row_copies(dest_ref, slot):
            cp.start()

    @pl.when(i + 1 < pl.num_programs(0))
    def _():
        for cp in row_copies(dest_next_ref, 1 - slot):
            cp.start()

    for r in range(tile):
        for k in range(TOP_K):
            pltpu.make_async_copy(yb_ref.at[pl.ds(0, 1)], buf.at[slot, k, pl.ds(r, 1)], sems.at[slot]).wait()
    rt = rt_ref[...]
    y = rt[:, TOP_K:TOP_K + 1] * buf[slot, 0]
    for k in range(1, TOP_K):
        y = y + rt[:, TOP_K + k:TOP_K + k + 1] * buf[slot, k]
    o_ref[...] = _layer_norm(ALPHA * x_ref[...] + (1.0 + gt_ref[...]) * y, g_ref[...], b_ref[...])


def moe_combine_ln(dest, yb, x1, route, gt, ln_g, ln_b):
    bsz, t, d = x1.shape
    n = bsz * t
    n_tiles, _, tile = dest.shape
    per_b = t // tile
    row = lambda w: pl.BlockSpec((tile, w), lambda i: (i, 0))
    out = pl.pallas_call(
        functools.partial(_combine_ln_kernel, tile=tile),
        grid=(n_tiles,),
        in_specs=[
            pl.BlockSpec((None, TOP_K, tile), lambda i: (i, 0, 0), memory_space=pltpu.SMEM),
            pl.BlockSpec((None, TOP_K, tile), lambda i: (jnp.minimum(i + 1, n_tiles - 1), 0, 0),
                         memory_space=pltpu.SMEM),
            row(d), row(LANES),
            pl.BlockSpec((None, 1, d), lambda i: (i // per_b, 0, 0)),
            pl.BlockSpec((1, d), lambda i: (0, 0)), pl.BlockSpec((1, d), lambda i: (0, 0)),
            pl.BlockSpec(memory_space=pl.ANY),
        ],
        out_specs=row(d),
        out_shape=jax.ShapeDtypeStruct((n, d), F32),
        scratch_shapes=[pltpu.VMEM((2, TOP_K, tile, d), F32), pltpu.SemaphoreType.DMA((2,))],
        compiler_params=_params("arbitrary"),
        name="moe_combine_ln",
    )(dest, dest, x1.reshape(n, d), route.reshape(n, LANES), gt, ln_g.reshape(1, d), ln_b.reshape(1, d), yb)
    return out.reshape(bsz, t, d)


def hier_moe_ln(x1, hf, route, idx, counts, gt, ln_g, ln_b, w_gate, w_up, w_down):
    bsz, t, d = hf.shape
    n = bsz * t
    n_exp = w_gate.shape[0]
    counts = counts[0, N_GROUPS:N_GROUPS + n_exp].astype(jnp.int32)
    padded = (counts + MOE_BLOCK - 1) // MOE_BLOCK * MOE_BLOCK
    pad_end = jnp.cumsum(padded)
    pad_start = (pad_end - padded).astype(jnp.int32)
    n_blk = -(-(n * TOP_K) // MOE_BLOCK) + n_exp
    blk_first = jnp.arange(n_blk, dtype=jnp.int32) * MOE_BLOCK
    blk_e = jnp.minimum(jnp.sum((pad_end[None, :] <= blk_first[:, None]).astype(jnp.int32), axis=1), n_exp - 1)
    n_used = (pad_end[-1:] // MOE_BLOCK).astype(jnp.int32)
    idx = idx.reshape((-1,) + idx.shape[2:])
    first_tail_half = pad_end[-1:] // (MOE_BLOCK // 2)
    fill = jnp.concatenate([pad_start + counts, padded - counts, first_tail_half]).astype(jnp.int32)
    xg, dest = moe_dispatch(hf.reshape(n, d), idx, pad_start, fill, n_blk * MOE_BLOCK)
    yb = moe_ffn(xg, blk_e.astype(jnp.int32), n_used, w_gate, w_up, w_down)
    return moe_combine_ln(dest, yb, x1, route, gt, ln_g, ln_b)


def _pad_cols(w, mult=LANES):
    pad = (-w.shape[-1]) % mult
    return jnp.pad(w, ((0, 0), (0, pad))) if pad else w


def kernel(x, c, ada_w, ada_b, ln1_g, ln1_b, ln2_g, ln2_b, gdn_w_in, gdn_conv_w, gdn_a_log, gdn_dt_bias, gdn_norm_g, gdn_w_out, lru_w_in, lru_conv_w, lru_conv_b, lru_w_a, lru_b_a, lru_w_x, lru_b_x, lru_lambda, lru_w_out, nsa_w_in, nsa_pe_k, nsa_pe_v, nsa_ck_w1, nsa_ck_w2, nsa_cv_w1, nsa_cv_w2, nsa_w_out, moe_w_grp, moe_b_grp, moe_w_exp, moe_b_exp, moe_w_gate, moe_w_up, moe_w_down):
    bsz, t, d = x.shape
    depth = ada_w.shape[0]
    mod = ada_modulation(c, ada_w, ada_b).reshape(depth, bsz, 6, 1, d)
    ja = jb = jc = 0
    for i in range(depth):
        sh1, sc1, gt1, sh2, sc2, gt2 = (mod[i, :, k] for k in range(6))
        kind = i % N_MIXERS
        if kind == 0:
            p = modproj(x, sc1, sh1, _pad_cols(gdn_w_in[ja]).astype(BF16))
            o = gdn_core(p, gdn_conv_w[ja], gdn_a_log[ja], gdn_dt_bias[ja], gdn_norm_g[ja])
            w_out = gdn_w_out[ja]
            ja += 1
        elif kind == 1:
            p = modproj(x, sc1, sh1, lru_w_in[jb].astype(BF16))
            o = lru_core(p, lru_conv_w[jb], lru_conv_b[jb], lru_w_a[jb], lru_b_a[jb], lru_w_x[jb], lru_b_x[jb],
                         lru_lambda[jb])
            w_out = lru_w_out[jb]
            jb += 1
        else:
            o = nsa_mixer_core(x, sc1, sh1, nsa_w_in[jc], nsa_pe_k[jc], nsa_pe_v[jc], nsa_ck_w1[jc], nsa_ck_w2[jc],
                               nsa_cv_w1[jc], nsa_cv_w2[jc])
            w_out = nsa_w_out[jc]
            jc += 1
        w_router = _pad_cols(jnp.concatenate([moe_w_grp[i], moe_w_exp[i]], axis=1))
        b_router = _pad_cols(jnp.concatenate([moe_b_grp[i], moe_b_exp[i]])[None, :])
        x1, hf, route, idx, counts = outproj_ln(o, w_out.astype(BF16), x, gt1, ln1_g[i], ln1_b[i], sc2, sh2,
                                                w_router, b_router)
        x = hier_moe_ln(x1, hf, route, idx, counts, gt2, ln2_g[i], ln2_b[i],
                        moe_w_gate[i], moe_w_up[i], moe_w_down[i])
    return x
```

```python
import functools

import jax
import jax.numpy as jnp
from jax import lax
from jax.experimental import pallas as pl
from jax.experimental.pallas import tpu as pltpu

F32 = jnp.float32
BF16 = jnp.bfloat16

D_MODEL = 1024
DEPTH = 4
N_MIXERS = 3
LN_EPS = 1e-5
NORM_EPS = 1e-6
CONV_K = 4
ALPHA = (2.0 * DEPTH) ** 0.25
GDN_HEADS = 8
GDN_DK = 128
GDN_CHUNK = 64
GDN_QKV = 3 * GDN_HEADS * GDN_DK
LRU_WIDTH = 1024
LRU_BLOCKS = 16
LRU_BW = LRU_WIDTH // LRU_BLOCKS
RG_C = 8.0
NSA_GROUPS = 4
NSA_HPG = 4
NSA_DK = 64
NSA_Q = NSA_GROUPS * NSA_HPG * NSA_DK
NSA_KV = NSA_GROUPS * NSA_DK
CMP_LEN = 32
CMP_STRIDE = 16
SLC_LEN = 64
N_SEL = 16
WINDOW = 512
Q_BLOCK = 128
SEL_KEY_CHUNK = 512
N_GROUPS = 4
EXP_PER_GROUP = 8
N_EXPERTS = N_GROUPS * EXP_PER_GROUP
TOP_K = 2
MOE_BLOCK = 256
NEG = -1e30
BIG = 1e30
LOWEST = -3e38

LANES = 128
SUBLANES = 8
VMEM_LIMIT_BYTES = 48 * 1024 * 1024


def _params(*sem):
    return pltpu.CompilerParams(dimension_semantics=sem, vmem_limit_bytes=VMEM_LIMIT_BYTES)


def _sigmoid(x):
    return 1.0 / (1.0 + jnp.exp(-x))


def _silu(x):
    return x * _sigmoid(x)


def _softplus(x):
    return jnp.maximum(x, 0.0) + jnp.log(1.0 + jnp.exp(-jnp.abs(x)))


def _gelu(x):
    return 0.5 * x * (1.0 + jnp.tanh(0.7978845608028654 * (x + 0.044715 * (x * x * x))))


def _dot(a, b):
    return jnp.dot(a, b, preferred_element_type=F32)


def _dot_nt(a, b):
    return lax.dot_general(a, b, (((1,), (1,)), ((), ())), preferred_element_type=F32)


def _dot_tn(a, b):
    return lax.dot_general(a, b, (((0,), (0,)), ((), ())), preferred_element_type=F32)


def _split3(x):
    hi = x.astype(BF16)
    r1 = x - hi.astype(F32)
    mid = r1.astype(BF16)
    lo = (r1 - mid.astype(F32)).astype(BF16)
    return hi, mid, lo


def _dot_lhs_exact(m_bf16, x):
    hi, mid, lo = _split3(x)
    return _dot(m_bf16, hi) + _dot(m_bf16, mid) + _dot(m_bf16, lo)


def _dot_f32(a, b):
    ah, am, al = _split3(a)
    bh, bm, bl = _split3(b)
    return (_dot(ah, bh) + _dot(ah, bm) + _dot(am, bh)) + (_dot(ah, bl) + _dot(al, bh) + _dot(am, bm))


def _dot_hi(a, b):
    ah = a.astype(BF16)
    al = (a - ah.astype(F32)).astype(BF16)
    bh = b.astype(BF16)
    bl = (b - bh.astype(F32)).astype(BF16)
    return _dot(ah, bh) + (_dot(ah, bl) + _dot(al, bh))


def _layer_norm(z, g, b):
    mu = jnp.mean(z, axis=-1, keepdims=True)
    zc = z - mu
    var = jnp.mean(zc * zc, axis=-1, keepdims=True)
    return zc * lax.rsqrt(var + LN_EPS) * g + b


def _causal_conv(xbuf, cw_ref, col0, width, tb):
    cols = slice(col0, col0 + width)
    acc = cw_ref[0:1, cols] * xbuf[SUBLANES - 3:SUBLANES - 3 + tb, cols]
    for k in range(1, CONV_K):
        off = SUBLANES - 3 + k
        acc = acc + cw_ref[k:k + 1, cols] * xbuf[off:off + tb, cols]
    return acc


def _ada_kernel(c_ref, w_ref, b_ref, o_ref):
    cond = _silu(c_ref[...])
    o_ref[...] = _dot_f32(cond, w_ref[...]) + b_ref[...]


def ada_modulation(c, ada_w, ada_b, col_tile=1536):
    depth, d, n6 = ada_w.shape
    bsz = c.shape[0]
    return pl.pallas_call(
        _ada_kernel,
        grid=(depth, n6 // col_tile),
        in_specs=[
            pl.BlockSpec((bsz, d), lambda i, j: (0, 0)),
            pl.BlockSpec((None, d, col_tile), lambda i, j: (i, 0, j)),
            pl.BlockSpec((None, 1, col_tile), lambda i, j: (i, 0, j)),
        ],
        out_specs=pl.BlockSpec((None, bsz, col_tile), lambda i, j: (i, 0, j)),
        out_shape=jax.ShapeDtypeStruct((depth, bsz, n6), F32),
        compiler_params=_params("arbitrary", "arbitrary"),
        name="ada_modulation",
    )(c, ada_w, ada_b.reshape(depth, 1, n6))


def _modproj_kernel(x_ref, sc_ref, sh_ref, w_ref, o_ref, *, col_chunk):
    h = (x_ref[...] * (1.0 + sc_ref[...]) + sh_ref[...]).astype(BF16)
    ncols = o_ref.shape[-1]
    for c0 in range(0, ncols, col_chunk):
        c1 = min(c0 + col_chunk, ncols)
        o_ref[:, c0:c1] = _dot(h, w_ref[:, c0:c1])


def modproj(x, sc, sh, w, tm=256, col_chunk=1024):
    bsz, t, d = x.shape
    ncols = w.shape[1]
    return pl.pallas_call(
        functools.partial(_modproj_kernel, col_chunk=col_chunk),
        grid=(bsz, t // tm),
        in_specs=[
            pl.BlockSpec((None, tm, d), lambda b, i: (b, i, 0)),
            pl.BlockSpec((None, 1, d), lambda b, i: (b, 0, 0)),
            pl.BlockSpec((None, 1, d), lambda b, i: (b, 0, 0)),
            pl.BlockSpec((d, ncols), lambda b, i: (0, 0)),
        ],
        out_specs=pl.BlockSpec((None, tm, ncols), lambda b, i: (b, i, 0)),
        out_shape=jax.ShapeDtypeStruct((bsz, t, ncols), F32),
        compiler_params=_params("arbitrary", "arbitrary"),
        name="modproj",
    )(x, sc, sh, w)


def _gdn_kernel(qkv_ref, z_ref, ab_ref, cw_ref, alog_ref, dtb_ref, ng_ref, o_ref, xbuf, s_ref, *, tb):
    nh, dk, chunk = GDN_HEADS, GDN_DK, GDN_CHUNK

    @pl.when(pl.program_id(1) == 0)
    def _():
        xbuf[0:SUBLANES, :] = jnp.zeros((SUBLANES, GDN_QKV), F32)
        s_ref[...] = jnp.zeros_like(s_ref)

    xbuf[SUBLANES:SUBLANES + tb, :] = qkv_ref[...]

    ab = ab_ref[...]
    g_all = -jnp.exp(alog_ref[...]) * _softplus(ab + dtb_ref[...])
    beta_all = _sigmoid(ab)
    ri = lax.broadcasted_iota(jnp.int32, (tb, tb), 0)
    ci = lax.broadcasted_iota(jnp.int32, (tb, tb), 1)
    same = (ri // chunk) == (ci // chunk)
    causal = same & (ci <= ri)
    strict = same & (ci < ri)
    eye = jnp.where(ri == ci, 1.0, 0.0)
    tri = jnp.where(causal, 1.0, 0.0).astype(BF16)
    ones_blk = jnp.where(same, 1.0, 0.0).astype(BF16)
    gc_all = _dot_lhs_exact(tri, g_all)
    glast_all = _dot_lhs_exact(ones_blk, g_all)
    gc_t = gc_all.T
    egc_all = jnp.exp(gc_all)
    kdec_all = jnp.exp(glast_all - gc_all)

    heads = range(nh)
    lmats, rhss, qk16s, qg16s, ks16s, gls = [], [], [], [], [], []
    for h in heads:
        q = _silu(_causal_conv(xbuf, cw_ref, h * dk, dk, tb))
        k = _silu(_causal_conv(xbuf, cw_ref, (nh + h) * dk, dk, tb))
        v = _silu(_causal_conv(xbuf, cw_ref, (2 * nh + h) * dk, dk, tb))
        q = q * lax.rsqrt(jnp.sum(q * q, axis=-1, keepdims=True) + NORM_EPS)
        k = k * lax.rsqrt(jnp.sum(k * k, axis=-1, keepdims=True) + NORM_EPS)
        beta = beta_all[:, nh + h:nh + h + 1]
        eg = egc_all[:, h:h + 1]
        decay = jnp.exp(jnp.where(causal, gc_all[:, h:h + 1] - gc_t[h:h + 1, :], NEG))
        kb = k * beta
        k16 = k.astype(BF16)
        qs = q * (dk ** -0.5)
        lmats.append(jnp.where(strict, _dot_nt(kb.astype(BF16), k16) * decay, 0.0))
        rhss.append(jnp.concatenate([v * beta, kb * eg], axis=1).astype(BF16))
        qk16s.append((_dot_nt(qs.astype(BF16), k16) * decay).astype(BF16))
        qg16s.append((qs * eg).astype(BF16))
        ks16s.append((k * kdec_all[:, h:h + 1]).astype(BF16))
        gls.append(glast_all[:, h:h + 1])
    xbuf[0:SUBLANES, :] = xbuf[tb:tb + SUBLANES, :]

    xpows = lmats
    tinvs = [eye - lm for lm in lmats]
    for _ in range(5):
        xpows = [_dot(x16, x16) for x16 in [x.astype(BF16) for x in xpows]]
        tinvs = [t + _dot(t.astype(BF16), x.astype(BF16)) for t, x in zip(tinvs, xpows)]
    uws = [_dot(t.astype(BF16), rhs) for t, rhs in zip(tinvs, rhss)]
    us = [uw[:, :dk] for uw in uws]
    w16s = [uw[:, dk:].astype(BF16) for uw in uws]

    states = [s_ref[h] for h in heads]
    o_parts = [[] for _ in heads]
    vn_parts = [[] for _ in heads]
    for c in range(tb // chunk):
        rows = slice(c * chunk, (c + 1) * chunk)
        for h in heads:
            s16 = states[h].astype(BF16)
            vn16 = (us[h][rows] - _dot(w16s[h][rows], s16)).astype(BF16)
            o_parts[h].append(_dot(qg16s[h][rows], s16))
            states[h] = (states[h] * jnp.exp(gls[h][c * chunk:c * chunk + 1, :])
                         + _dot_tn(ks16s[h][rows], vn16))
            vn_parts[h].append(vn16)
    for h in heads:
        s_ref[h] = states[h]
        o = jnp.concatenate(o_parts[h], axis=0) + _dot(qk16s[h], jnp.concatenate(vn_parts[h], axis=0))
        o = o * lax.rsqrt(jnp.mean(o * o, axis=-1, keepdims=True) + NORM_EPS) * ng_ref[...]
        o_ref[:, h * dk:(h + 1) * dk] = o * _silu(z_ref[:, h * dk:(h + 1) * dk])


def gdn_core(p, conv_w, a_log, dt_bias, norm_g, tb=256):
    bsz, t, _ = p.shape
    nh, dk = GDN_HEADS, GDN_DK
    v_w = nh * dk
    alog = jnp.zeros((1, LANES), F32).at[0, :nh].set(a_log)
    dtb = jnp.zeros((1, LANES), F32).at[0, :nh].set(dt_bias)
    ab_blk = (GDN_QKV + v_w) // LANES
    return pl.pallas_call(
        functools.partial(_gdn_kernel, tb=tb),
        grid=(bsz, t // tb),
        in_specs=[
            pl.BlockSpec((None, tb, GDN_QKV), lambda b, i: (b, i, 0)),
            pl.BlockSpec((None, tb, v_w), lambda b, i: (b, i, GDN_QKV // v_w)),
            pl.BlockSpec((None, tb, LANES), lambda b, i: (b, i, ab_blk)),
            pl.BlockSpec((CONV_K, GDN_QKV), lambda b, i: (0, 0)),
            pl.BlockSpec((1, LANES), lambda b, i: (0, 0)),
            pl.BlockSpec((1, LANES), lambda b, i: (0, 0)),
            pl.BlockSpec((1, dk), lambda b, i: (0, 0)),
        ],
        out_specs=pl.BlockSpec((None, tb, v_w), lambda b, i: (b, i, 0)),
        out_shape=jax.ShapeDtypeStruct((bsz, t, v_w), F32),
        scratch_shapes=[pltpu.VMEM((tb + SUBLANES, GDN_QKV), F32), pltpu.VMEM((nh, dk, dk), F32)],
        compiler_params=_params("arbitrary", "arbitrary"),
        name="gdn_core",
    )(p, p, p, conv_w, alog, dtb, norm_g.reshape(1, dk))


def _lru_kernel(gate_ref, xb_ref, cw_ref, cb_ref, wax_ref, ba_ref, bx_ref, lam_ref, o_ref, xbuf, h_ref, *, tb):
    width = LRU_WIDTH
    grp = 2 * LANES

    @pl.when(pl.program_id(1) == 0)
    def _():
        xbuf[0:SUBLANES, :] = jnp.zeros((SUBLANES, width), F32)
        h_ref[...] = jnp.zeros_like(h_ref)

    xbuf[SUBLANES:SUBLANES + tb, :] = xb_ref[...]
    row = lax.broadcasted_iota(jnp.int32, (tb, grp), 0)

    for j in range(width // grp):
        cols = slice(j * grp, (j + 1) * grp)
        xc = _causal_conv(xbuf, cw_ref, j * grp, grp, tb) + cb_ref[:, cols]
        res = _dot(xc.astype(BF16), wax_ref[j])
        r = _sigmoid(res[:, :grp] + ba_ref[:, cols])
        ig = _sigmoid(res[:, grp:] + bx_ref[:, cols])
        log_a = -RG_C * r * _softplus(-lam_ref[:, cols])
        a = jnp.exp(log_a)
        mult = jnp.sqrt(jnp.maximum(1.0 - jnp.exp(2.0 * log_a), 0.0))
        u = mult * ig * xc
        s = 1
        while s < tb:
            a_sh = jnp.where(row < s, 1.0, pltpu.roll(a, s, 0))
            u_sh = jnp.where(row < s, 0.0, pltpu.roll(u, s, 0))
            u = a * u_sh + u
            a = a * a_sh
            s *= 2
        hs = u + a * h_ref[:, cols]
        h_ref[:, cols] = hs[tb - 1:tb, :]
        o_ref[:, cols] = hs * _gelu(gate_ref[:, cols])

    xbuf[0:SUBLANES, :] = xbuf[tb:tb + SUBLANES, :]


def _block_diag_groups(w):
    per = (2 * LANES) // LRU_BW
    wg = w.reshape(LRU_BLOCKS // per, per, LRU_BW, LRU_BW)
    eye = jnp.eye(per, dtype=w.dtype)
    return jnp.einsum('gaij,ab->gaibj', wg, eye).reshape(LRU_BLOCKS // per, per * LRU_BW, per * LRU_BW)


def lru_core(p, conv_w, conv_b, w_a, b_a, w_x, b_x, lam, tb=256):
    bsz, t, _ = p.shape
    width = LRU_WIDTH
    grp = 2 * LANES
    wax = jnp.concatenate([_block_diag_groups(w_a), _block_diag_groups(w_x)], axis=-1).astype(BF16)
    vec = lambda a: a.reshape(1, width)
    return pl.pallas_call(
        functools.partial(_lru_kernel, tb=tb),
        grid=(bsz, t // tb),
        in_specs=[
            pl.BlockSpec((None, tb, width), lambda b, i: (b, i, 0)),
            pl.BlockSpec((None, tb, width), lambda b, i: (b, i, 1)),
            pl.BlockSpec((CONV_K, width), lambda b, i: (0, 0)),
            pl.BlockSpec((1, width), lambda b, i: (0, 0)),
            pl.BlockSpec((width // grp, grp, 2 * grp), lambda b, i: (0, 0, 0)),
            pl.BlockSpec((1, width), lambda b, i: (0, 0)),
            pl.BlockSpec((1, width), lambda b, i: (0, 0)),
            pl.BlockSpec((1, width), lambda b, i: (0, 0)),
        ],
        out_specs=pl.BlockSpec((None, tb, width), lambda b, i: (b, i, 0)),
        out_shape=jax.ShapeDtypeStruct((bsz, t, width), F32),
        scratch_shapes=[pltpu.VMEM((tb + SUBLANES, width), F32), pltpu.VMEM((1, width), F32)],
        compiler_params=_params("arbitrary", "arbitrary"),
        name="lru_core",
    )(p, p, conv_w, vec(conv_b), wax, vec(b_a), vec(b_x), vec(lam))


DEN_LANE = NSA_DK


def _nsa_proj_kernel(x_ref, sc_ref, sh_ref, wf_ref, wkv_ref, bkv_ref, pf_ref, pkv_ref, *, tm, n_slc, pad_tiles):
    g = NSA_GROUPS
    slab = n_slc + LANES
    i = pl.program_id(1)

    @pl.when(i < pad_tiles)
    def _():
        lane = lax.broadcasted_iota(jnp.int32, pkv_ref.shape, 1)
        kw0 = g * slab + g * LANES
        flag = (lane >= kw0) & (lane < kw0 + g * LANES) & ((lane - kw0) % LANES == DEN_LANE)
        pkv_ref[...] = jnp.where(flag, 1.0, 0.0).astype(pkv_ref.dtype)

    @pl.when(i >= pad_tiles)
    def _():
        h = (x_ref[...] * (1.0 + sc_ref[...]) + sh_ref[...]).astype(BF16)
        nf = pf_ref.shape[-1]
        for c0 in range(0, nf, 1024):
            c1 = min(c0 + 1024, nf)
            pf_ref[:, c0:c1] = _dot(h, wf_ref[:, c0:c1])
        tok = (i - pad_tiles) * tm + lax.broadcasted_iota(jnp.int32, (tm, n_slc), 0)
        onehot = jnp.where(tok // SLC_LEN == lax.broadcasted_iota(jnp.int32, (tm, n_slc), 1), 1.0, 0.0)
        for c in range(g):
            pkv_ref[:, c * slab:c * slab + n_slc] = onehot.astype(pkv_ref.dtype)
            pkv_ref[:, c * slab + n_slc:(c + 1) * slab] = _dot(h, wkv_ref[:, c * LANES:(c + 1) * LANES]).astype(
                pkv_ref.dtype)
        rest = _dot(h, wkv_ref[:, g * LANES:]) + bkv_ref[...]
        pkv_ref[:, g * slab:] = rest.astype(pkv_ref.dtype)


def nsa_proj(x, sc, sh, w_in, tm=256):
    bsz, t, d = x.shape
    g, dk = NSA_GROUPS, NSA_DK
    n_slc = t // SLC_LEN
    pad_tiles = WINDOW // tm
    pad_lanes = lambda w: jnp.pad(w, ((0, 0),) * (w.ndim - 1) + ((0, LANES - w.shape[-1]),))
    streams = pad_lanes(w_in[:, NSA_Q:NSA_Q + 6 * NSA_KV].reshape(d, 6, g, dk))
    gate_w = pad_lanes(w_in[:, NSA_Q + 6 * NSA_KV:].reshape(d, g, 3 * NSA_HPG))
    wf = jnp.concatenate([w_in[:, :NSA_Q], streams[:, 0:2].reshape(d, -1), gate_w.reshape(d, -1)], axis=1)
    wkv = streams[:, 2:6].reshape(d, -1)
    den = jnp.zeros((LANES,), F32).at[DEN_LANE].set(1.0)
    bkv = jnp.concatenate([jnp.tile(den, g), jnp.zeros((g * LANES,), F32), jnp.tile(den, g)])[None, :]
    nf, nkv = wf.shape[1], g * (n_slc + LANES) + 3 * g * LANES
    tok_tile = lambda b, i: (b, jnp.maximum(i - pad_tiles, 0), 0)
    return pl.pallas_call(
        functools.partial(_nsa_proj_kernel, tm=tm, n_slc=n_slc, pad_tiles=pad_tiles),
        grid=(bsz, t // tm + pad_tiles),
        in_specs=[
            pl.BlockSpec((None, tm, d), tok_tile),
            pl.BlockSpec((None, 1, d), lambda b, i: (b, 0, 0)),
            pl.BlockSpec((None, 1, d), lambda b, i: (b, 0, 0)),
            pl.BlockSpec(wf.shape, lambda b, i: (0, 0)),
            pl.BlockSpec(wkv.shape, lambda b, i: (0, 0)),
            pl.BlockSpec(bkv.shape, lambda b, i: (0, 0)),
        ],
        out_specs=[pl.BlockSpec((None, tm, nf), tok_tile),
                   pl.BlockSpec((None, tm, nkv), lambda b, i: (b, i, 0))],
        out_shape=[jax.ShapeDtypeStruct((bsz, t, nf), F32),
                   jax.ShapeDtypeStruct((bsz, t + WINDOW, nkv), BF16)],
        compiler_params=_params("arbitrary", "arbitrary"),
        name="nsa_proj",
    )(x, sc, sh, wf.astype(BF16), wkv.astype(BF16), bkv)


def _cmp_kernel(x_ref, pe_ref, w1a_ref, w1b_ref, w2_ref, o_ref):
    nseg = o_ref.shape[0]
    first = second = None
    for l in range(CMP_STRIDE):
        xl = x_ref[pl.ds(l, nseg, stride=CMP_STRIDE), :]
        a = _dot((xl + pe_ref[l:l + 1, :]).astype(BF16), w1a_ref[l])
        b = _dot((xl + pe_ref[CMP_STRIDE + l:CMP_STRIDE + l + 1, :]).astype(BF16), w1b_ref[l])
        first = a if first is None else first + a
        second = b if second is None else second + b
    hid = _gelu(first + pltpu.roll(second, nseg - 1, 0))
    o_ref[...] = _dot(hid.astype(BF16), w2_ref[...]).astype(o_ref.dtype)


def nsa_compress(pf, pe, w1, w2):
    bsz, t, _ = pf.shape
    g, dk = NSA_GROUPS, NSA_DK
    nseg = t // CMP_STRIDE
    hid = w1.shape[-1]
    pad_mid = lambda w: jnp.pad(w, ((0, 0), (0, 0), (0, LANES - dk), (0, 0)))
    w1r = w1.reshape(2, CMP_LEN, dk, hid)
    w1a = pad_mid(w1r[:, :CMP_STRIDE]).astype(BF16)
    w1b = pad_mid(w1r[:, CMP_STRIDE:]).astype(BF16)
    pe_p = jnp.pad(pe, ((0, 0), (0, 0), (0, LANES - dk)))
    w2p = jnp.pad(w2, ((0, 0), (0, 0), (0, LANES - dk))).astype(BF16)
    first_slab = NSA_Q // LANES
    return pl.pallas_call(
        _cmp_kernel,
        grid=(2, bsz, g),
        in_specs=[
            pl.BlockSpec((None, t, LANES), lambda a, b, c: (b, 0, first_slab + a * g + c)),
            pl.BlockSpec((None, CMP_LEN, LANES), lambda a, b, c: (a, 0, 0)),
            pl.BlockSpec((None, CMP_STRIDE, LANES, hid), lambda a, b, c: (a, 0, 0, 0)),
            pl.BlockSpec((None, CMP_STRIDE, LANES, hid), lambda a, b, c: (a, 0, 0, 0)),
            pl.BlockSpec((None, hid, LANES), lambda a, b, c: (a, 0, 0)),
        ],
        out_specs=pl.BlockSpec((None, None, None, nseg, LANES), lambda a, b, c: (a, b, c, 0, 0)),
        out_shape=jax.ShapeDtypeStruct((2, bsz, g, nseg, LANES), BF16),
        compiler_params=_params("arbitrary", "arbitrary", "arbitrary"),
        name="nsa_compress",
    )(pf, pe_p, w1a, w1b, w2p)


def _nsa_kernel(q_ref, gate_ref, kc_ref, vc_ref, ks_ref, vs_ref, kw_ref, vw_ref, wbias_ref, o_ref, *, t_len):
    p_heads, dk, qb = NSA_HPG, NSA_DK, Q_BLOCK
    rows = p_heads * qb
    n_cmp_rows = kc_ref.shape[0]
    n_slc = t_len // SLC_LEN
    n_sel = min(N_SEL, n_slc)
    kc_len = SEL_KEY_CHUNK
    t0 = pl.program_id(2) * qb

    qg = q_ref[...]
    halves = []
    for p in range(p_heads):
        tile = qg[:, (p // 2) * LANES:(p // 2 + 1) * LANES]
        halves.append(tile if p % 2 == 0 else pltpu.roll(tile, dk, 1))
    lane = lax.broadcasted_iota(jnp.int32, (rows, LANES), 1)
    qz = jnp.where(lane < dk, jnp.concatenate(halves, axis=0) * (dk ** -0.5), 0.0)
    qz16 = qz.astype(BF16)
    tpos = t0 + (lax.broadcasted_iota(jnp.int32, (rows, 1), 0) % qb)

    w_len = qb + WINDOW
    w0 = pl.multiple_of(t0, qb)
    qw16 = jnp.where(lane == DEN_LANE, NEG, qz).astype(BF16)
    s_w = _dot_nt(qw16, kw_ref[pl.ds(w0, w_len), :]) + wbias_ref[...]
    e_w = jnp.exp(s_w - jnp.max(s_w, axis=-1, keepdims=True))
    acc_w = _dot(e_w.astype(BF16), vw_ref[pl.ds(w0, w_len), :])
    o_w = acc_w * (1.0 / acc_w[:, DEN_LANE:DEN_LANE + 1])

    s_c = _dot_nt(qz16, kc_ref[...])
    n_idx = lax.broadcasted_iota(jnp.int32, (rows, n_cmp_rows), 1)
    vis = (n_idx * CMP_STRIDE + (CMP_LEN - 1)) <= tpos
    s_c = jnp.where(vis, s_c, NEG)
    e_c = jnp.where(vis, jnp.exp(s_c - jnp.max(s_c, axis=-1, keepdims=True)), 0.0)
    l_c = jnp.sum(e_c, axis=-1, keepdims=True)
    p_c = e_c * (1.0 / jnp.where(l_c > 0.0, l_c, 1.0))
    o_c = _dot(p_c.astype(BF16), vc_ref[...])

    p_sum = p_c[0:qb]
    for p in range(1, p_heads):
        p_sum = p_sum + p_c[p * qb:(p + 1) * qb]
    sj = lax.broadcasted_iota(jnp.int32, (n_slc, n_cmp_rows), 0) * SLC_LEN
    cn = lax.broadcasted_iota(jnp.int32, (n_slc, n_cmp_rows), 1) * CMP_STRIDE
    ov = jnp.maximum(jnp.minimum(cn + CMP_LEN, sj + SLC_LEN) - jnp.maximum(cn, sj), 0)
    overlap_t = (ov.astype(F32) * (1.0 / CMP_LEN)).astype(BF16)
    imp = None
    for piece in _split3(p_sum):
        part = _dot_nt(overlap_t, piece)
        imp = part if imp is None else imp + part
    s_idx = lax.broadcasted_iota(jnp.int32, (n_slc, qb), 0)
    tq = t0 + lax.broadcasted_iota(jnp.int32, (n_slc, qb), 1)
    cur = tq // SLC_LEN
    forced = (s_idx == 0) | (s_idx == cur) | (s_idx == cur - 1)
    imp = jnp.where(forced, BIG, imp)
    imp = jnp.where(s_idx * SLC_LEN <= tq, imp, -BIG)
    s_idx_f = s_idx.astype(F32)
    sel_t = jnp.zeros((n_slc, qb), F32)
    for _ in range(n_sel):
        m = jnp.max(imp, axis=0, keepdims=True)
        first = jnp.min(jnp.where(imp == m, s_idx_f, float(n_slc)), axis=0, keepdims=True)
        pick = s_idx_f == first
        sel_t = jnp.where(pick, 1.0, sel_t)
        imp = jnp.where(pick, LOWEST, imp)
    own_blk = t0 // SLC_LEN
    sel_t = jnp.where((s_idx == own_blk) | (s_idx == own_blk + 1) | (s_idx * SLC_LEN > tq), 0.0, sel_t)
    selneg = ((sel_t.T - 1.0) * BIG).astype(BF16)
    q_aug = jnp.concatenate([jnp.concatenate([selneg] * p_heads, axis=0), qz16], axis=1)

    own0 = pl.multiple_of(WINDOW + t0, qb)
    s_own = _dot_nt(qz16, ks_ref[pl.ds(own0, qb), n_slc:n_slc + LANES])
    s_own = jnp.where((t0 + lax.broadcasted_iota(jnp.int32, (rows, qb), 1)) <= tpos, s_own, NEG)
    m_own = jnp.max(s_own, axis=-1, keepdims=True)
    acc_own = _dot(jnp.exp((s_own - m_own).astype(BF16)), vs_ref[pl.ds(own0, qb), :])

    def chunk_rows(c):
        return pl.ds(pl.multiple_of(WINDOW + c * kc_len, kc_len), kc_len)

    def sel_update(c, carry, scores):
        m, acc = carry
        m_new = jnp.maximum(m, jnp.max(scores, axis=-1, keepdims=True))
        prob = jnp.exp((scores - m_new).astype(BF16))
        return m_new, jnp.exp(m - m_new) * acc + _dot(prob, vs_ref[chunk_rows(c), :])

    def sel_scores(c):
        return _dot_nt(q_aug, ks_ref[chunk_rows(c), :])

    def sel_pair(i, carry):
        s_a = sel_scores(2 * i)
        s_b = sel_scores(2 * i + 1)
        return sel_update(2 * i + 1, sel_update(2 * i, carry, s_a), s_b)

    n_chunks = t0 // kc_len + 1
    carry = lax.fori_loop(0, n_chunks // 2, sel_pair, (m_own, acc_own))
    _, acc_s = lax.cond(n_chunks % 2 == 1,
                        lambda cr: sel_update(n_chunks - 1, cr, sel_scores(n_chunks - 1)),
                        lambda cr: cr, carry)
    o_s = acc_s * (1.0 / acc_s[:, DEN_LANE:DEN_LANE + 1])

    gts = _sigmoid(gate_ref[...])
    outs = []
    for p in range(p_heads):
        r = slice(p * qb, (p + 1) * qb)
        outs.append(gts[:, 3 * p:3 * p + 1] * o_c[r] + gts[:, 3 * p + 1:3 * p + 2] * o_s[r]
                    + gts[:, 3 * p + 2:3 * p + 3] * o_w[r])
    lane_o = lax.broadcasted_iota(jnp.int32, (qb, LANES), 1)
    for t in range(p_heads // 2):
        o_ref[:, t * LANES:(t + 1) * LANES] = jnp.where(lane_o < dk, outs[2 * t], pltpu.roll(outs[2 * t + 1], dk, 1))


def nsa_attention(pf, pkv, kvcmp):
    bsz, t, _ = pf.shape
    g, dk, qb = NSA_GROUPS, NSA_DK, Q_BLOCK
    rows = NSA_HPG * qb
    w_len = qb + WINDOW
    qw = NSA_HPG * dk
    slab = t // SLC_LEN + LANES
    gate_slab = NSA_Q // LANES + 2 * g
    kv_base = g * slab // LANES
    ri = jnp.arange(rows, dtype=jnp.int32)[:, None] % qb
    ki = jnp.arange(w_len, dtype=jnp.int32)[None, :]
    wbias = jnp.where((ki > ri) & (ki <= ri + WINDOW), 0.0, NEG).astype(F32)
    cmp_spec = pl.BlockSpec((None, None) + kvcmp.shape[3:], lambda b, c, j: (b, c, 0, 0))
    kv_spec = lambda first: pl.BlockSpec((None, t + WINDOW, LANES), lambda b, c, j: (b, 0, first + c))
    return pl.pallas_call(
        functools.partial(_nsa_kernel, t_len=t),
        grid=(bsz, g, t // qb),
        in_specs=[
            pl.BlockSpec((None, qb, qw), lambda b, c, j: (b, j, c)),
            pl.BlockSpec((None, qb, LANES), lambda b, c, j: (b, j, gate_slab + c)),
            cmp_spec, cmp_spec,
            pl.BlockSpec((None, t + WINDOW, slab), lambda b, c, j: (b, 0, c)),
            kv_spec(kv_base), kv_spec(kv_base + g), kv_spec(kv_base + 2 * g),
            pl.BlockSpec((rows, w_len), lambda b, c, j: (0, 0)),
        ],
        out_specs=pl.BlockSpec((None, qb, qw), lambda b, c, j: (b, j, c)),
        out_shape=jax.ShapeDtypeStruct((bsz, t, g * qw), F32),
        compiler_params=_params("arbitrary", "arbitrary", "arbitrary"),
        name="nsa_attention",
    )(pf, pf, kvcmp[0], kvcmp[1], pkv, pkv, pkv, pkv, wbias)


def nsa_mixer_core(x, sc, sh, w_in, pe_k, pe_v, ck_w1, ck_w2, cv_w1, cv_w2):
    pf, pkv = nsa_proj(x, sc, sh, w_in)
    kvcmp = nsa_compress(pf, jnp.stack([pe_k, pe_v]), jnp.stack([ck_w1, cv_w1]), jnp.stack([ck_w2, cv_w2]))
    return nsa_attention(pf, pkv, kvcmp)


def _outproj_ln_kernel(o_ref, w_ref, x_ref, gt_ref, g_ref, b_ref, sc_ref, sh_ref, wr_ref, br_ref,
                       x1_ref, hf_ref, rt_ref, idx_ref, cnt_ref, cnt_acc):
    @pl.when((pl.program_id(0) == 0) & (pl.program_id(1) == 0))
    def _():
        cnt_acc[...] = jnp.zeros_like(cnt_acc)

    y = _dot(o_ref[...].astype(BF16), w_ref[...])
    x1 = _layer_norm(ALPHA * x_ref[...] + (1.0 + gt_ref[...]) * y, g_ref[...], b_ref[...])
    x1_ref[...] = x1
    hf = x1 * (1.0 + sc_ref[...]) + sh_ref[...]
    hf_ref[...] = hf

    logits = _dot_hi(hf, wr_ref[...]) + br_ref[...]
    lane = lax.broadcasted_iota(jnp.int32, logits.shape, 1).astype(F32)
    is_grp = lane < N_GROUPS
    gl = jnp.where(is_grp, logits, LOWEST)
    gmax = jnp.max(gl, axis=-1, keepdims=True)
    gi = jnp.min(jnp.where(gl == gmax, lane, float(LANES)), axis=-1, keepdims=True)
    gp = 1.0 / jnp.sum(jnp.where(is_grp, jnp.exp(logits - gmax), 0.0), axis=-1, keepdims=True)
    lo = N_GROUPS + EXP_PER_GROUP * gi
    el = jnp.where((lane >= lo) & (lane < lo + EXP_PER_GROUP), logits, LOWEST)
    m1 = jnp.max(el, axis=-1, keepdims=True)
    i1 = jnp.min(jnp.where(el == m1, lane, float(LANES)), axis=-1, keepdims=True)
    el2 = jnp.where(lane == i1, LOWEST, el)
    m2 = jnp.max(el2, axis=-1, keepdims=True)
    i2 = jnp.min(jnp.where(el2 == m2, lane, float(LANES)), axis=-1, keepdims=True)
    e2 = jnp.exp(m2 - m1)
    g1 = gp / (1.0 + e2)
    g2 = gp * e2 / (1.0 + e2)

    tm = logits.shape[0]
    onehot = jnp.where((lane == i1) | (lane == i2), 1.0, 0.0)
    ri = lax.broadcasted_iota(jnp.int32, (tm, tm), 0)
    ci = lax.broadcasted_iota(jnp.int32, (tm, tm), 1)
    before = jnp.where(ci < ri, 1.0, 0.0).astype(BF16)
    seen = _dot(before, onehot.astype(BF16)) + cnt_acc[...]
    rank1 = jnp.sum(jnp.where(lane == i1, seen, 0.0), axis=-1, keepdims=True)
    rank2 = jnp.sum(jnp.where(lane == i2, seen, 0.0), axis=-1, keepdims=True)
    cnt_acc[...] = cnt_acc[...] + jnp.sum(onehot, axis=0, keepdims=True)
    cnt_ref[...] = cnt_acc[...]

    vals = (i1 - N_GROUPS, i2 - N_GROUPS, g1, g2, rank1, rank2)
    out = jnp.zeros_like(logits)
    for pos, val in enumerate(vals):
        out = jnp.where(lane == float(pos), val, out)
    rt_ref[...] = out
    idx_ref[...] = out.T[0:SUBLANES, :].astype(jnp.int32)


def outproj_ln(o, w_out, x, gt, ln_g, ln_b, sc2, sh2, w_router, b_router, tm=256):
    bsz, t, d = x.shape
    din = o.shape[-1]
    row = lambda: pl.BlockSpec((None, tm, d), lambda b, i: (b, i, 0))
    per_b = lambda: pl.BlockSpec((None, 1, d), lambda b, i: (b, 0, 0))
    const = lambda r, c: pl.BlockSpec((r, c), lambda b, i: (0, 0))
    return pl.pallas_call(
        _outproj_ln_kernel,
        grid=(bsz, t // tm),
        in_specs=[
            pl.BlockSpec((None, tm, din), lambda b, i: (b, i, 0)),
            const(din, d), row(), per_b(), const(1, d), const(1, d), per_b(), per_b(),
            const(d, LANES), const(1, LANES),
        ],
        out_specs=[row(), row(), pl.BlockSpec((None, tm, LANES), lambda b, i: (b, i, 0)),
                   pl.BlockSpec((None, None, SUBLANES, tm), lambda b, i: (b, i, 0, 0)), const(1, LANES)],
        out_shape=[jax.ShapeDtypeStruct((bsz, t, d), F32), jax.ShapeDtypeStruct((bsz, t, d), F32),
                   jax.ShapeDtypeStruct((bsz, t, LANES), F32),
                   jax.ShapeDtypeStruct((bsz, t // tm, SUBLANES, tm), jnp.int32),
                   jax.ShapeDtypeStruct((1, LANES), F32)],
        scratch_shapes=[pltpu.VMEM((1, LANES), F32)],
        compiler_params=_params("arbitrary", "arbitrary"),
        name="outproj_ln",
    )(o, w_out, x, gt, ln_g.reshape(1, d), ln_b.reshape(1, d), sc2, sh2, w_router, b_router)


def _moe_kernel(blk_e_ref, nused_ref, x_ref, wg_ref, wu_ref, wd_ref, o_ref, wg16, wu16, wd16):
    i = pl.program_id(0)
    e = blk_e_ref[i]
    e_prev = blk_e_ref[jnp.maximum(i - 1, 0)]

    @pl.when((i == 0) | (e != e_prev))
    def _():
        wg16[...] = wg_ref[...].astype(BF16)
        wu16[...] = wu_ref[...].astype(BF16)
        wd16[...] = wd_ref[...].astype(BF16)

    @pl.when(i < nused_ref[0])
    def _():
        x = x_ref[...].astype(BF16)
        gate = _dot(x, wg16[...])
        up = _dot(x, wu16[...])
        o_ref[...] = _dot((_silu(gate) * up).astype(BF16), wd16[...])

    @pl.when(i >= nused_ref[0])
    def _():
        o_ref[...] = jnp.zeros_like(o_ref)


def moe_ffn(xg, blk_e, n_used, w_gate, w_up, w_down, layer):
    rows, d = xg.shape
    de = w_gate.shape[-1]
    n_blk = rows // MOE_BLOCK
    grid_spec = pltpu.PrefetchScalarGridSpec(
        num_scalar_prefetch=2,
        grid=(n_blk,),
        in_specs=[
            pl.BlockSpec((MOE_BLOCK, d), lambda i, be, nu: (jnp.minimum(i, nu[0] - 1), 0)),
            pl.BlockSpec((None, None, d, de), lambda i, be, nu: (layer, be[i], 0, 0)),
            pl.BlockSpec((None, None, d, de), lambda i, be, nu: (layer, be[i], 0, 0)),
            pl.BlockSpec((None, None, de, d), lambda i, be, nu: (layer, be[i], 0, 0)),
        ],
        out_specs=pl.BlockSpec((MOE_BLOCK, d), lambda i, be, nu: (i, 0)),
        scratch_shapes=[pltpu.VMEM((d, de), BF16), pltpu.VMEM((d, de), BF16), pltpu.VMEM((de, d), BF16)],
    )
    return pl.pallas_call(
        _moe_kernel,
        grid_spec=grid_spec,
        out_shape=jax.ShapeDtypeStruct((rows, d), F32),
        compiler_params=_params("arbitrary"),
        name="moe_ffn",
    )(blk_e, n_used, xg, w_gate, w_up, w_down)


def _dispatch_kernel(ps_ref, fill_ref, idx_ref, x_ref, xg_out, dest_ref, zbuf, sem, *, tile, n_exp):
    def row_copy(r, d):
        return pltpu.make_async_copy(x_ref.at[pl.ds(r, 1)], xg_out.at[pl.ds(d, 1)], sem)

    @pl.when(pl.program_id(0) == 0)
    def _():
        zbuf[...] = jnp.zeros_like(zbuf)
        pieces = [SUBLANES << b for b in range((MOE_BLOCK // SUBLANES).bit_length() - 1)]

        def fill_copies(e, wait):
            start = fill_ref[e]
            head = (-start) & (SUBLANES - 1)
            rest = fill_ref[n_exp + e] - head
            for j in range(SUBLANES - 1):
                @pl.when(j < head)
                def _(j=j):
                    cp = pltpu.make_async_copy(zbuf.at[pl.ds(0, 1)], xg_out.at[pl.ds(start + j, 1)], sem)
                    cp.wait() if wait else cp.start()
            cur = start + head
            for size in pieces:
                @pl.when((rest & size) != 0)
                def _(cur=cur, size=size):
                    dst = xg_out.at[pl.ds(pl.multiple_of(cur, SUBLANES), size)]
                    cp = pltpu.make_async_copy(zbuf.at[pl.ds(0, size)], dst, sem)
                    cp.wait() if wait else cp.start()
                cur = cur + (rest & size)

        for e in range(n_exp):
            fill_copies(e, wait=False)
        for e in range(n_exp):
            fill_copies(e, wait=True)

        n_rows = xg_out.shape[0]
        half = zbuf.shape[0]

        def tail_copy(j, wait):
            dst = xg_out.at[pl.ds(pl.multiple_of(j * half, half), half)]
            cp = pltpu.make_async_copy(zbuf, dst, sem)
            cp.wait() if wait else cp.start()

        first_tail = fill_ref[2 * n_exp]
        lax.fori_loop(first_tail, n_rows // half, lambda j, c: (tail_copy(j, False), c)[1], 0)
        lax.fori_loop(first_tail, n_rows // half, lambda j, c: (tail_copy(j, True), c)[1], 0)

    copies = []
    for r in range(tile):
        for k in range(TOP_K):
            d = ps_ref[idx_ref[k, r]] + idx_ref[2 * TOP_K + k, r]
            dest_ref[k, r] = d
            copies.append(row_copy(r, d))
            copies[-1].start()
    for cp in copies:
        cp.wait()


def moe_dispatch(hf, idx, pad_start, fill, rows):
    n, d = hf.shape
    n_tiles, _, tile = idx.shape
    grid_spec = pltpu.PrefetchScalarGridSpec(
        num_scalar_prefetch=2,
        grid=(n_tiles,),
        in_specs=[
            pl.BlockSpec((None, SUBLANES, tile), lambda i, ps, fl: (i, 0, 0), memory_space=pltpu.SMEM),
            pl.BlockSpec((tile, d), lambda i, ps, fl: (i, 0)),
        ],
        out_specs=[
            pl.BlockSpec(memory_space=pl.ANY),
            pl.BlockSpec((None, TOP_K, tile), lambda i, ps, fl: (i, 0, 0), memory_space=pltpu.SMEM),
        ],
        scratch_shapes=[pltpu.VMEM((MOE_BLOCK // 2, d), hf.dtype), pltpu.SemaphoreType.DMA(())],
    )
    return pl.pallas_call(
        functools.partial(_dispatch_kernel, tile=tile, n_exp=fill.shape[0] // 2),
        grid_spec=grid_spec,
        out_shape=[jax.ShapeDtypeStruct((rows, d), hf.dtype),
                   jax.ShapeDtypeStruct((n_tiles, TOP_K, tile), jnp.int32)],
        compiler_params=_params("arbitrary"),
        name="moe_dispatch",
    )(pad_start, fill, idx, hf)


def _combine_ln_kernel(dest_ref, dest_next_ref, x_ref, rt_ref, gt_ref, g_ref, b_ref, yb_ref, o_ref, buf, sems,
                       *, tile):
    i = pl.program_id(0)
    slot = i % 2

    def row_copies(dests, s):
        return [pltpu.make_async_copy(yb_ref.at[pl.ds(dests[k, r], 1)], buf.at[s, k, pl.ds(r, 1)], sems.at[s])
                for r in range(tile) for k in range(TOP_K)]

    @pl.when(i == 0)
    def _():
        for cp in row_copies(dest_ref, slot):
            cp.start()

    @pl.when(i + 1 < pl.num_programs(0))
    def _():
        for cp in row_copies(dest_next_ref, 1 - slot):
            cp.start()

    for r in range(tile):
        for k in range(TOP_K):
            pltpu.make_async_copy(yb_ref.at[pl.ds(0, 1)], buf.at[slot, k, pl.ds(r, 1)], sems.at[slot]).wait()
    rt = rt_ref[...]
    y = rt[:, TOP_K:TOP_K + 1] * buf[slot, 0]
    for k in range(1, TOP_K):
        y = y + rt[:, TOP_K + k:TOP_K + k + 1] * buf[slot, k]
    o_ref[...] = _layer_norm(ALPHA * x_ref[...] + (1.0 + gt_ref[...]) * y, g_ref[...], b_ref[...])


def moe_combine_ln(dest, yb, x1, route, gt, ln_g, ln_b):
    bsz, t, d = x1.shape
    n = bsz * t
    n_tiles, _, tile = dest.shape
    per_b = t // tile
    row = lambda w: pl.BlockSpec((tile, w), lambda i: (i, 0))
    out = pl.pallas_call(
        functools.partial(_combine_ln_kernel, tile=tile),
        grid=(n_tiles,),
        in_specs=[
            pl.BlockSpec((None, TOP_K, tile), lambda i: (i, 0, 0), memory_space=pltpu.SMEM),
            pl.BlockSpec((None, TOP_K, tile), lambda i: (jnp.minimum(i + 1, n_tiles - 1), 0, 0),
                         memory_space=pltpu.SMEM),
            row(d), row(LANES),
            pl.BlockSpec((None, 1, d), lambda i: (i // per_b, 0, 0)),
            pl.BlockSpec((1, d), lambda i: (0, 0)), pl.BlockSpec((1, d), lambda i: (0, 0)),
            pl.BlockSpec(memory_space=pl.ANY),
        ],
        out_specs=row(d),
        out_shape=jax.ShapeDtypeStruct((n, d), F32),
        scratch_shapes=[pltpu.VMEM((2, TOP_K, tile, d), F32), pltpu.SemaphoreType.DMA((2,))],
        compiler_params=_params("arbitrary"),
        name="moe_combine_ln",
    )(dest, dest, x1.reshape(n, d), route.reshape(n, LANES), gt, ln_g.reshape(1, d), ln_b.reshape(1, d), yb)
    return out.reshape(bsz, t, d)


def hier_moe_ln(x1, hf, route, idx, counts, gt, ln_g, ln_b, w_gate, w_up, w_down, layer):
    bsz, t, d = hf.shape
    n = bsz * t
    n_exp = w_gate.shape[1]
    counts = counts[0, N_GROUPS:N_GROUPS + n_exp].astype(jnp.int32)
    padded = (counts + MOE_BLOCK - 1) // MOE_BLOCK * MOE_BLOCK
    pad_end = jnp.cumsum(padded)
    pad_start = (pad_end - padded).astype(jnp.int32)
    n_blk = -(-(n * TOP_K) // MOE_BLOCK) + n_exp
    blk_first = jnp.arange(n_blk, dtype=jnp.int32) * MOE_BLOCK
    blk_e = jnp.minimum(jnp.sum((pad_end[None, :] <= blk_first[:, None]).astype(jnp.int32), axis=1), n_exp - 1)
    n_used = (pad_end[-1:] // MOE_BLOCK).astype(jnp.int32)
    idx = idx.reshape((-1,) + idx.shape[2:])
    first_tail_half = pad_end[-1:] // (MOE_BLOCK // 2)
    fill = jnp.concatenate([pad_start + counts, padded - counts, first_tail_half]).astype(jnp.int32)
    xg, dest = moe_dispatch(hf.reshape(n, d), idx, pad_start, fill, n_blk * MOE_BLOCK)
    yb = moe_ffn(xg, blk_e.astype(jnp.int32), n_used, w_gate, w_up, w_down, layer)
    return moe_combine_ln(dest, yb, x1, route, gt, ln_g, ln_b)


def _pad_cols(w, mult=LANES):
    pad = (-w.shape[-1]) % mult
    return jnp.pad(w, ((0, 0), (0, pad))) if pad else w


def kernel(x, c, ada_w, ada_b, ln1_g, ln1_b, ln2_g, ln2_b, gdn_w_in, gdn_conv_w, gdn_a_log, gdn_dt_bias, gdn_norm_g, gdn_w_out, lru_w_in, lru_conv_w, lru_conv_b, lru_w_a, lru_b_a, lru_w_x, lru_b_x, lru_lambda, lru_w_out, nsa_w_in, nsa_pe_k, nsa_pe_v, nsa_ck_w1, nsa_ck_w2, nsa_cv_w1, nsa_cv_w2, nsa_w_out, moe_w_grp, moe_b_grp, moe_w_exp, moe_b_exp, moe_w_gate, moe_w_up, moe_w_down):
    bsz, t, d = x.shape
    depth = ada_w.shape[0]
    mod = ada_modulation(c, ada_w, ada_b).reshape(depth, bsz, 6, 1, d)
    ja = jb = jc = 0
    for i in range(depth):
        sh1, sc1, gt1, sh2, sc2, gt2 = (mod[i, :, k] for k in range(6))
        kind = i % N_MIXERS
        if kind == 0:
            p = modproj(x, sc1, sh1, _pad_cols(gdn_w_in[ja]).astype(BF16))
            o = gdn_core(p, gdn_conv_w[ja], gdn_a_log[ja], gdn_dt_bias[ja], gdn_norm_g[ja])
            w_out = gdn_w_out[ja]
            ja += 1
        elif kind == 1:
            p = modproj(x, sc1, sh1, lru_w_in[jb].astype(BF16))
            o = lru_core(p, lru_conv_w[jb], lru_conv_b[jb], lru_w_a[jb], lru_b_a[jb], lru_w_x[jb], lru_b_x[jb],
                         lru_lambda[jb])
            w_out = lru_w_out[jb]
            jb += 1
        else:
            o = nsa_mixer_core(x, sc1, sh1, nsa_w_in[jc], nsa_pe_k[jc], nsa_pe_v[jc], nsa_ck_w1[jc], nsa_ck_w2[jc],
                               nsa_cv_w1[jc], nsa_cv_w2[jc])
            w_out = nsa_w_out[jc]
            jc += 1
        w_router = _pad_cols(jnp.concatenate([moe_w_grp[i], moe_w_exp[i]], axis=1))
        b_router = _pad_cols(jnp.concatenate([moe_b_grp[i], moe_b_exp[i]])[None, :])
        x1, hf, route, idx, counts = outproj_ln(o, w_out.astype(BF16), x, gt1, ln1_g[i], ln1_b[i], sc2, sh2,
                                                w_router, b_router)
        x = hier_moe_ln(x1, hf, route, idx, counts, gt2, ln2_g[i], ln2_b[i], moe_w_gate, moe_w_up, moe_w_down, i)
    return x
```

```python
import functools

import jax
import jax.numpy as jnp
from jax import lax
from jax.experimental import pallas as pl
from jax.experimental.pallas import tpu as pltpu

F32 = jnp.float32
BF16 = jnp.bfloat16

D_MODEL = 1024
DEPTH = 4
N_MIXERS = 3
LN_EPS = 1e-5
NORM_EPS = 1e-6
CONV_K = 4
ALPHA = (2.0 * DEPTH) ** 0.25
GDN_HEADS = 8
GDN_DK = 128
GDN_CHUNK = 64
GDN_QKV = 3 * GDN_HEADS * GDN_DK
LRU_WIDTH = 1024
LRU_BLOCKS = 16
LRU_BW = LRU_WIDTH // LRU_BLOCKS
RG_C = 8.0
NSA_GROUPS = 4
NSA_HPG = 4
NSA_DK = 64
NSA_Q = NSA_GROUPS * NSA_HPG * NSA_DK
NSA_KV = NSA_GROUPS * NSA_DK
CMP_LEN = 32
CMP_STRIDE = 16
SLC_LEN = 64
N_SEL = 16
WINDOW = 512
Q_BLOCK = 128
SEL_KEY_CHUNK = 512
N_GROUPS = 4
EXP_PER_GROUP = 8
N_EXPERTS = N_GROUPS * EXP_PER_GROUP
TOP_K = 2
MOE_BLOCK = 256
NEG = -1e30
BIG = 1e30
LOWEST = -3e38

LANES = 128
SUBLANES = 8
VMEM_LIMIT_BYTES = 48 * 1024 * 1024


def _params(*sem):
    return pltpu.CompilerParams(dimension_semantics=sem, vmem_limit_bytes=VMEM_LIMIT_BYTES)


def _sigmoid(x):
    return 1.0 / (1.0 + jnp.exp(-x))


def _silu(x):
    return x * _sigmoid(x)


def _softplus(x):
    return jnp.maximum(x, 0.0) + jnp.log(1.0 + jnp.exp(-jnp.abs(x)))


def _gelu(x):
    return 0.5 * x * (1.0 + jnp.tanh(0.7978845608028654 * (x + 0.044715 * (x * x * x))))


def _dot(a, b):
    return jnp.dot(a, b, preferred_element_type=F32)


def _dot_nt(a, b):
    return lax.dot_general(a, b, (((1,), (1,)), ((), ())), preferred_element_type=F32)


def _dot_tn(a, b):
    return lax.dot_general(a, b, (((0,), (0,)), ((), ())), preferred_element_type=F32)


def _split3(x):
    hi = x.astype(BF16)
    r1 = x - hi.astype(F32)
    mid = r1.astype(BF16)
    lo = (r1 - mid.astype(F32)).astype(BF16)
    return hi, mid, lo


def _dot_lhs_exact(m_bf16, x):
    hi, mid, lo = _split3(x)
    return _dot(m_bf16, hi) + _dot(m_bf16, mid) + _dot(m_bf16, lo)


def _dot_f32(a, b):
    ah, am, al = _split3(a)
    bh, bm, bl = _split3(b)
    return (_dot(ah, bh) + _dot(ah, bm) + _dot(am, bh)) + (_dot(ah, bl) + _dot(al, bh) + _dot(am, bm))


def _dot_hi(a, b):
    ah = a.astype(BF16)
    al = (a - ah.astype(F32)).astype(BF16)
    bh = b.astype(BF16)
    bl = (b - bh.astype(F32)).astype(BF16)
    return _dot(ah, bh) + (_dot(ah, bl) + _dot(al, bh))


def _layer_norm(z, g, b):
    mu = jnp.mean(z, axis=-1, keepdims=True)
    zc = z - mu
    var = jnp.mean(zc * zc, axis=-1, keepdims=True)
    return zc * lax.rsqrt(var + LN_EPS) * g + b


def _causal_conv(xbuf, cw_ref, col0, width, tb):
    cols = slice(col0, col0 + width)
    acc = cw_ref[0:1, cols] * xbuf[SUBLANES - 3:SUBLANES - 3 + tb, cols]
    for k in range(1, CONV_K):
        off = SUBLANES - 3 + k
        acc = acc + cw_ref[k:k + 1, cols] * xbuf[off:off + tb, cols]
    return acc


def _ada_kernel(c_ref, w_ref, b_ref, o_ref):
    cond = _silu(c_ref[...])
    o_ref[...] = _dot_f32(cond, w_ref[...]) + b_ref[...]


def ada_modulation(c, ada_w, ada_b, col_tile=1536):
    depth, d, n6 = ada_w.shape
    bsz = c.shape[0]
    return pl.pallas_call(
        _ada_kernel,
        grid=(depth, n6 // col_tile),
        in_specs=[
            pl.BlockSpec((bsz, d), lambda i, j: (0, 0)),
            pl.BlockSpec((None, d, col_tile), lambda i, j: (i, 0, j)),
            pl.BlockSpec((None, 1, col_tile), lambda i, j: (i, 0, j)),
        ],
        out_specs=pl.BlockSpec((None, bsz, col_tile), lambda i, j: (i, 0, j)),
        out_shape=jax.ShapeDtypeStruct((depth, bsz, n6), F32),
        compiler_params=_params("arbitrary", "arbitrary"),
        name="ada_modulation",
    )(c, ada_w, ada_b.reshape(depth, 1, n6))


def _modproj_kernel(x_ref, sc_ref, sh_ref, w_ref, o_ref, *, col_chunk):
    h = (x_ref[...] * (1.0 + sc_ref[...]) + sh_ref[...]).astype(BF16)
    ncols = o_ref.shape[-1]
    for c0 in range(0, ncols, col_chunk):
        c1 = min(c0 + col_chunk, ncols)
        o_ref[:, c0:c1] = _dot(h, w_ref[:, c0:c1])


def modproj(x, sc, sh, w, tm=256, col_chunk=1024):
    bsz, t, d = x.shape
    ncols = w.shape[1]
    return pl.pallas_call(
        functools.partial(_modproj_kernel, col_chunk=col_chunk),
        grid=(bsz, t // tm),
        in_specs=[
            pl.BlockSpec((None, tm, d), lambda b, i: (b, i, 0)),
            pl.BlockSpec((None, 1, d), lambda b, i: (b, 0, 0)),
            pl.BlockSpec((None, 1, d), lambda b, i: (b, 0, 0)),
            pl.BlockSpec((d, ncols), lambda b, i: (0, 0)),
        ],
        out_specs=pl.BlockSpec((None, tm, ncols), lambda b, i: (b, i, 0)),
        out_shape=jax.ShapeDtypeStruct((bsz, t, ncols), F32),
        compiler_params=_params("arbitrary", "arbitrary"),
        name="modproj",
    )(x, sc, sh, w)


def _gdn_kernel(qkv_ref, z_ref, ab_ref, cw_ref, alog_ref, dtb_ref, ng_ref, o_ref, xbuf, s_ref, *, tb):
    nh, dk, chunk = GDN_HEADS, GDN_DK, GDN_CHUNK

    @pl.when(pl.program_id(1) == 0)
    def _():
        xbuf[0:SUBLANES, :] = jnp.zeros((SUBLANES, GDN_QKV), F32)
        s_ref[...] = jnp.zeros_like(s_ref)

    xbuf[SUBLANES:SUBLANES + tb, :] = qkv_ref[...]

    ab = ab_ref[...]
    g_all = -jnp.exp(alog_ref[...]) * _softplus(ab + dtb_ref[...])
    beta_all = _sigmoid(ab)
    ri = lax.broadcasted_iota(jnp.int32, (tb, tb), 0)
    ci = lax.broadcasted_iota(jnp.int32, (tb, tb), 1)
    same = (ri // chunk) == (ci // chunk)
    causal = same & (ci <= ri)
    strict = same & (ci < ri)
    eye = jnp.where(ri == ci, 1.0, 0.0)
    tri = jnp.where(causal, 1.0, 0.0).astype(BF16)
    ones_blk = jnp.where(same, 1.0, 0.0).astype(BF16)
    gc_all = _dot_lhs_exact(tri, g_all)
    glast_all = _dot_lhs_exact(ones_blk, g_all)
    gc_t = gc_all.T
    egc_all = jnp.exp(gc_all)
    kdec_all = jnp.exp(glast_all - gc_all)

    n_iter = 5
    n_chunks = tb // chunk
    hd = [dict() for _ in range(nh)]

    def prepare(h):
        q = _silu(_causal_conv(xbuf, cw_ref, h * dk, dk, tb))
        k = _silu(_causal_conv(xbuf, cw_ref, (nh + h) * dk, dk, tb))
        v = _silu(_causal_conv(xbuf, cw_ref, (2 * nh + h) * dk, dk, tb))
        q = q * lax.rsqrt(jnp.sum(q * q, axis=-1, keepdims=True) + NORM_EPS)
        k = k * lax.rsqrt(jnp.sum(k * k, axis=-1, keepdims=True) + NORM_EPS)
        beta = beta_all[:, nh + h:nh + h + 1]
        eg = egc_all[:, h:h + 1]
        decay = jnp.exp(jnp.where(causal, gc_all[:, h:h + 1] - gc_t[h:h + 1, :], NEG))
        kb = k * beta
        k16 = k.astype(BF16)
        qs = q * (dk ** -0.5)
        lmat = jnp.where(strict, _dot_nt(kb.astype(BF16), k16) * decay, 0.0)
        hd[h].update(
            xpow=lmat, tinv=eye - lmat,
            rhs=jnp.concatenate([v * beta, kb * eg], axis=1).astype(BF16),
            qk16=(_dot_nt(qs.astype(BF16), k16) * decay).astype(BF16),
            qg16=(qs * eg).astype(BF16),
            ks16=(k * kdec_all[:, h:h + 1]).astype(BF16),
            gl=glast_all[:, h:h + 1], state=s_ref[h], o_parts=[], vn_parts=[])

    def neumann_step(group):
        for h in group:
            x16 = hd[h]['xpow'].astype(BF16)
            hd[h]['xpow'] = _dot(x16, x16)
        for h in group:
            hd[h]['tinv'] = hd[h]['tinv'] + _dot(hd[h]['tinv'].astype(BF16), hd[h]['xpow'].astype(BF16))

    def solve(group):
        for h in group:
            uw = _dot(hd[h]['tinv'].astype(BF16), hd[h]['rhs'])
            hd[h]['u'] = uw[:, :dk]
            hd[h]['w16'] = uw[:, dk:].astype(BF16)

    def chunk_step(group, c):
        rows = slice(c * chunk, (c + 1) * chunk)
        for h in group:
            d = hd[h]
            s16 = d['state'].astype(BF16)
            vn16 = (d['u'][rows] - _dot(d['w16'][rows], s16)).astype(BF16)
            d['o_parts'].append(_dot(d['qg16'][rows], s16))
            d['state'] = d['state'] * jnp.exp(d['gl'][c * chunk:c * chunk + 1, :]) + _dot_tn(d['ks16'][rows], vn16)
            d['vn_parts'].append(vn16)

    def finish(group):
        for h in group:
            d = hd[h]
            s_ref[h] = d['state']
            o = jnp.concatenate(d['o_parts'], axis=0) + _dot(d['qk16'], jnp.concatenate(d['vn_parts'], axis=0))
            o = o * lax.rsqrt(jnp.mean(o * o, axis=-1, keepdims=True) + NORM_EPS) * ng_ref[...]
            o_ref[:, h * dk:(h + 1) * dk] = o * _silu(z_ref[:, h * dk:(h + 1) * dk])

    first, second = list(range(nh // 2)), list(range(nh // 2, nh))
    for h in first:
        prepare(h)
    for it in range(max(n_iter, len(second))):
        if it < n_iter:
            neumann_step(first)
        if it < len(second):
            prepare(second[it])
    xbuf[0:SUBLANES, :] = xbuf[tb:tb + SUBLANES, :]
    solve(first)
    for it in range(max(n_iter, n_chunks)):
        if it < n_iter:
            neumann_step(second)
        if it < n_chunks:
            chunk_step(first, it)
    solve(second)
    finish(first)
    for c in range(n_chunks):
        chunk_step(second, c)
    finish(second)


def gdn_core(p, conv_w, a_log, dt_bias, norm_g, tb=256):
    bsz, t, _ = p.shape
    nh, dk = GDN_HEADS, GDN_DK
    v_w = nh * dk
    alog = jnp.zeros((1, LANES), F32).at[0, :nh].set(a_log)
    dtb = jnp.zeros((1, LANES), F32).at[0, :nh].set(dt_bias)
    ab_blk = (GDN_QKV + v_w) // LANES
    return pl.pallas_call(
        functools.partial(_gdn_kernel, tb=tb),
        grid=(bsz, t // tb),
        in_specs=[
            pl.BlockSpec((None, tb, GDN_QKV), lambda b, i: (b, i, 0)),
            pl.BlockSpec((None, tb, v_w), lambda b, i: (b, i, GDN_QKV // v_w)),
            pl.BlockSpec((None, tb, LANES), lambda b, i: (b, i, ab_blk)),
            pl.BlockSpec((CONV_K, GDN_QKV), lambda b, i: (0, 0)),
            pl.BlockSpec((1, LANES), lambda b, i: (0, 0)),
            pl.BlockSpec((1, LANES), lambda b, i: (0, 0)),
            pl.BlockSpec((1, dk), lambda b, i: (0, 0)),
        ],
        out_specs=pl.BlockSpec((None, tb, v_w), lambda b, i: (b, i, 0)),
        out_shape=jax.ShapeDtypeStruct((bsz, t, v_w), F32),
        scratch_shapes=[pltpu.VMEM((tb + SUBLANES, GDN_QKV), F32), pltpu.VMEM((nh, dk, dk), F32)],
        compiler_params=_params("arbitrary", "arbitrary"),
        name="gdn_core",
    )(p, p, p, conv_w, alog, dtb, norm_g.reshape(1, dk))


def _lru_kernel(gate_ref, xb_ref, cw_ref, cb_ref, wax_ref, ba_ref, bx_ref, lam_ref, o_ref, xbuf, h_ref, *, tb):
    width = LRU_WIDTH
    grp = 2 * LANES

    @pl.when(pl.program_id(1) == 0)
    def _():
        xbuf[0:SUBLANES, :] = jnp.zeros((SUBLANES, width), F32)
        h_ref[...] = jnp.zeros_like(h_ref)

    xbuf[SUBLANES:SUBLANES + tb, :] = xb_ref[...]
    row = lax.broadcasted_iota(jnp.int32, (tb, grp), 0)

    for j in range(width // grp):
        cols = slice(j * grp, (j + 1) * grp)
        xc = _causal_conv(xbuf, cw_ref, j * grp, grp, tb) + cb_ref[:, cols]
        res = _dot(xc.astype(BF16), wax_ref[j])
        r = _sigmoid(res[:, :grp] + ba_ref[:, cols])
        ig = _sigmoid(res[:, grp:] + bx_ref[:, cols])
        log_a = -RG_C * r * _softplus(-lam_ref[:, cols])
        a = jnp.exp(log_a)
        mult = jnp.sqrt(jnp.maximum(1.0 - jnp.exp(2.0 * log_a), 0.0))
        u = mult * ig * xc
        s = 1
        while s < tb:
            a_sh = jnp.where(row < s, 1.0, pltpu.roll(a, s, 0))
            u_sh = jnp.where(row < s, 0.0, pltpu.roll(u, s, 0))
            u = a * u_sh + u
            a = a * a_sh
            s *= 2
        hs = u + a * h_ref[:, cols]
        h_ref[:, cols] = hs[tb - 1:tb, :]
        o_ref[:, cols] = hs * _gelu(gate_ref[:, cols])

    xbuf[0:SUBLANES, :] = xbuf[tb:tb + SUBLANES, :]


def _block_diag_groups(w):
    per = (2 * LANES) // LRU_BW
    wg = w.reshape(LRU_BLOCKS // per, per, LRU_BW, LRU_BW)
    eye = jnp.eye(per, dtype=w.dtype)
    return jnp.einsum('gaij,ab->gaibj', wg, eye).reshape(LRU_BLOCKS // per, per * LRU_BW, per * LRU_BW)


def lru_core(p, conv_w, conv_b, w_a, b_a, w_x, b_x, lam, tb=256):
    bsz, t, _ = p.shape
    width = LRU_WIDTH
    grp = 2 * LANES
    wax = jnp.concatenate([_block_diag_groups(w_a), _block_diag_groups(w_x)], axis=-1).astype(BF16)
    vec = lambda a: a.reshape(1, width)
    return pl.pallas_call(
        functools.partial(_lru_kernel, tb=tb),
        grid=(bsz, t // tb),
        in_specs=[
            pl.BlockSpec((None, tb, width), lambda b, i: (b, i, 0)),
            pl.BlockSpec((None, tb, width), lambda b, i: (b, i, 1)),
            pl.BlockSpec((CONV_K, width), lambda b, i: (0, 0)),
            pl.BlockSpec((1, width), lambda b, i: (0, 0)),
            pl.BlockSpec((width // grp, grp, 2 * grp), lambda b, i: (0, 0, 0)),
            pl.BlockSpec((1, width), lambda b, i: (0, 0)),
            pl.BlockSpec((1, width), lambda b, i: (0, 0)),
            pl.BlockSpec((1, width), lambda b, i: (0, 0)),
        ],
        out_specs=pl.BlockSpec((None, tb, width), lambda b, i: (b, i, 0)),
        out_shape=jax.ShapeDtypeStruct((bsz, t, width), F32),
        scratch_shapes=[pltpu.VMEM((tb + SUBLANES, width), F32), pltpu.VMEM((1, width), F32)],
        compiler_params=_params("arbitrary", "arbitrary"),
        name="lru_core",
    )(p, p, conv_w, vec(conv_b), wax, vec(b_a), vec(b_x), vec(lam))


DEN_LANE = NSA_DK


def _nsa_proj_kernel(x_ref, sc_ref, sh_ref, wf_ref, wkv_ref, bkv_ref, pf_ref, pkv_ref, *, tm, n_slc, pad_tiles):
    g = NSA_GROUPS
    slab = n_slc + LANES
    i = pl.program_id(1)

    @pl.when(i < pad_tiles)
    def _():
        lane = lax.broadcasted_iota(jnp.int32, pkv_ref.shape, 1)
        kw0 = g * slab + g * LANES
        flag = (lane >= kw0) & (lane < kw0 + g * LANES) & ((lane - kw0) % LANES == DEN_LANE)
        pkv_ref[...] = jnp.where(flag, 1.0, 0.0).astype(pkv_ref.dtype)

    @pl.when(i >= pad_tiles)
    def _():
        h = (x_ref[...] * (1.0 + sc_ref[...]) + sh_ref[...]).astype(BF16)
        nf = pf_ref.shape[-1]
        for c0 in range(0, nf, 1024):
            c1 = min(c0 + 1024, nf)
            pf_ref[:, c0:c1] = _dot(h, wf_ref[:, c0:c1])
        tok = (i - pad_tiles) * tm + lax.broadcasted_iota(jnp.int32, (tm, n_slc), 0)
        onehot = jnp.where(tok // SLC_LEN == lax.broadcasted_iota(jnp.int32, (tm, n_slc), 1), 1.0, 0.0)
        for c in range(g):
            pkv_ref[:, c * slab:c * slab + n_slc] = onehot.astype(pkv_ref.dtype)
            pkv_ref[:, c * slab + n_slc:(c + 1) * slab] = _dot(h, wkv_ref[:, c * LANES:(c + 1) * LANES]).astype(
                pkv_ref.dtype)
        rest = _dot(h, wkv_ref[:, g * LANES:]) + bkv_ref[...]
        pkv_ref[:, g * slab:] = rest.astype(pkv_ref.dtype)


def nsa_proj(x, sc, sh, w_in, tm=256):
    bsz, t, d = x.shape
    g, dk = NSA_GROUPS, NSA_DK
    n_slc = t // SLC_LEN
    pad_tiles = WINDOW // tm
    pad_lanes = lambda w: jnp.pad(w, ((0, 0),) * (w.ndim - 1) + ((0, LANES - w.shape[-1]),))
    streams = pad_lanes(w_in[:, NSA_Q:NSA_Q + 6 * NSA_KV].reshape(d, 6, g, dk))
    gate_w = pad_lanes(w_in[:, NSA_Q + 6 * NSA_KV:].reshape(d, g, 3 * NSA_HPG))
    wf = jnp.concatenate([w_in[:, :NSA_Q], streams[:, 0:2].reshape(d, -1), gate_w.reshape(d, -1)], axis=1)
    wkv = streams[:, 2:6].reshape(d, -1)
    den = jnp.zeros((LANES,), F32).at[DEN_LANE].set(1.0)
    bkv = jnp.concatenate([jnp.tile(den, g), jnp.zeros((g * LANES,), F32), jnp.tile(den, g)])[None, :]
    nf, nkv = wf.shape[1], g * (n_slc + LANES) + 3 * g * LANES
    tok_tile = lambda b, i: (b, jnp.maximum(i - pad_tiles, 0), 0)
    return pl.pallas_call(
        functools.partial(_nsa_proj_kernel, tm=tm, n_slc=n_slc, pad_tiles=pad_tiles),
        grid=(bsz, t // tm + pad_tiles),
        in_specs=[
            pl.BlockSpec((None, tm, d), tok_tile),
            pl.BlockSpec((None, 1, d), lambda b, i: (b, 0, 0)),
            pl.BlockSpec((None, 1, d), lambda b, i: (b, 0, 0)),
            pl.BlockSpec(wf.shape, lambda b, i: (0, 0)),
            pl.BlockSpec(wkv.shape, lambda b, i: (0, 0)),
            pl.BlockSpec(bkv.shape, lambda b, i: (0, 0)),
        ],
        out_specs=[pl.BlockSpec((None, tm, nf), tok_tile),
                   pl.BlockSpec((None, tm, nkv), lambda b, i: (b, i, 0))],
        out_shape=[jax.ShapeDtypeStruct((bsz, t, nf), F32),
                   jax.ShapeDtypeStruct((bsz, t + WINDOW, nkv), BF16)],
        compiler_params=_params("arbitrary", "arbitrary"),
        name="nsa_proj",
    )(x, sc, sh, wf.astype(BF16), wkv.astype(BF16), bkv)


def _cmp_kernel(x_ref, pe_ref, w1a_ref, w1b_ref, w2_ref, o_ref):
    nseg = o_ref.shape[0]
    first = second = None
    for l in range(CMP_STRIDE):
        xl = x_ref[pl.ds(l, nseg, stride=CMP_STRIDE), :]
        a = _dot((xl + pe_ref[l:l + 1, :]).astype(BF16), w1a_ref[l])
        b = _dot((xl + pe_ref[CMP_STRIDE + l:CMP_STRIDE + l + 1, :]).astype(BF16), w1b_ref[l])
        first = a if first is None else first + a
        second = b if second is None else second + b
    hid = _gelu(first + pltpu.roll(second, nseg - 1, 0))
    o_ref[...] = _dot(hid.astype(BF16), w2_ref[...]).astype(o_ref.dtype)


def nsa_compress(pf, pe, w1, w2):
    bsz, t, _ = pf.shape
    g, dk = NSA_GROUPS, NSA_DK
    nseg = t // CMP_STRIDE
    hid = w1.shape[-1]
    pad_mid = lambda w: jnp.pad(w, ((0, 0), (0, 0), (0, LANES - dk), (0, 0)))
    w1r = w1.reshape(2, CMP_LEN, dk, hid)
    w1a = pad_mid(w1r[:, :CMP_STRIDE]).astype(BF16)
    w1b = pad_mid(w1r[:, CMP_STRIDE:]).astype(BF16)
    pe_p = jnp.pad(pe, ((0, 0), (0, 0), (0, LANES - dk)))
    w2p = jnp.pad(w2, ((0, 0), (0, 0), (0, LANES - dk))).astype(BF16)
    first_slab = NSA_Q // LANES
    return pl.pallas_call(
        _cmp_kernel,
        grid=(2, bsz, g),
        in_specs=[
            pl.BlockSpec((None, t, LANES), lambda a, b, c: (b, 0, first_slab + a * g + c)),
            pl.BlockSpec((None, CMP_LEN, LANES), lambda a, b, c: (a, 0, 0)),
            pl.BlockSpec((None, CMP_STRIDE, LANES, hid), lambda a, b, c: (a, 0, 0, 0)),
            pl.BlockSpec((None, CMP_STRIDE, LANES, hid), lambda a, b, c: (a, 0, 0, 0)),
            pl.BlockSpec((None, hid, LANES), lambda a, b, c: (a, 0, 0)),
        ],
        out_specs=pl.BlockSpec((None, None, None, nseg, LANES), lambda a, b, c: (a, b, c, 0, 0)),
        out_shape=jax.ShapeDtypeStruct((2, bsz, g, nseg, LANES), BF16),
        compiler_params=_params("arbitrary", "arbitrary", "arbitrary"),
        name="nsa_compress",
    )(pf, pe_p, w1a, w1b, w2p)


def _nsa_kernel(q_ref, gate_ref, kc_ref, vc_ref, ks_ref, vs_ref, kw_ref, vw_ref, wbias_ref, o_ref, *, t_len):
    p_heads, dk, qb = NSA_HPG, NSA_DK, Q_BLOCK
    rows = p_heads * qb
    n_cmp_rows = kc_ref.shape[0]
    n_slc = t_len // SLC_LEN
    n_sel = min(N_SEL, n_slc)
    kc_len = SEL_KEY_CHUNK
    t0 = pl.program_id(2) * qb

    qg = q_ref[...]
    halves = []
    for p in range(p_heads):
        tile = qg[:, (p // 2) * LANES:(p // 2 + 1) * LANES]
        halves.append(tile if p % 2 == 0 else pltpu.roll(tile, dk, 1))
    lane = lax.broadcasted_iota(jnp.int32, (rows, LANES), 1)
    qz = jnp.where(lane < dk, jnp.concatenate(halves, axis=0) * (dk ** -0.5), 0.0)
    qz16 = qz.astype(BF16)
    tpos = t0 + (lax.broadcasted_iota(jnp.int32, (rows, 1), 0) % qb)

    w_len = qb + WINDOW
    w0 = pl.multiple_of(t0, qb)
    qw16 = jnp.where(lane == DEN_LANE, NEG, qz).astype(BF16)
    s_w = _dot_nt(qw16, kw_ref[pl.ds(w0, w_len), :]) + wbias_ref[...]
    e_w = jnp.exp(s_w - jnp.max(s_w, axis=-1, keepdims=True))
    acc_w = _dot(e_w.astype(BF16), vw_ref[pl.ds(w0, w_len), :])
    o_w = acc_w * (1.0 / acc_w[:, DEN_LANE:DEN_LANE + 1])

    s_c = _dot_nt(qz16, kc_ref[...])
    n_idx = lax.broadcasted_iota(jnp.int32, (rows, n_cmp_rows), 1)
    vis = (n_idx * CMP_STRIDE + (CMP_LEN - 1)) <= tpos
    s_c = jnp.where(vis, s_c, NEG)
    e_c = jnp.where(vis, jnp.exp(s_c - jnp.max(s_c, axis=-1, keepdims=True)), 0.0)
    l_c = jnp.sum(e_c, axis=-1, keepdims=True)
    p_c = e_c * (1.0 / jnp.where(l_c > 0.0, l_c, 1.0))
    o_c = _dot(p_c.astype(BF16), vc_ref[...])

    p_sum = p_c[0:qb]
    for p in range(1, p_heads):
        p_sum = p_sum + p_c[p * qb:(p + 1) * qb]
    sj = lax.broadcasted_iota(jnp.int32, (n_slc, n_cmp_rows), 0) * SLC_LEN
    cn = lax.broadcasted_iota(jnp.int32, (n_slc, n_cmp_rows), 1) * CMP_STRIDE
    ov = jnp.maximum(jnp.minimum(cn + CMP_LEN, sj + SLC_LEN) - jnp.maximum(cn, sj), 0)
    overlap_t = (ov.astype(F32) * (1.0 / CMP_LEN)).astype(BF16)
    imp = None
    for piece in _split3(p_sum):
        part = _dot_nt(overlap_t, piece)
        imp = part if imp is None else imp + part
    s_idx = lax.broadcasted_iota(jnp.int32, (n_slc, qb), 0)
    tq = t0 + lax.broadcasted_iota(jnp.int32, (n_slc, qb), 1)
    cur = tq // SLC_LEN
    forced = (s_idx == 0) | (s_idx == cur) | (s_idx == cur - 1)
    imp = jnp.where(forced, BIG, imp)
    imp = jnp.where(s_idx * SLC_LEN <= tq, imp, -BIG)
    s_idx_f = s_idx.astype(F32)
    sel_t = jnp.zeros((n_slc, qb), F32)
    for _ in range(n_sel):
        m = jnp.max(imp, axis=0, keepdims=True)
        first = jnp.min(jnp.where(imp == m, s_idx_f, float(n_slc)), axis=0, keepdims=True)
        pick = s_idx_f == first
        sel_t = jnp.where(pick, 1.0, sel_t)
        imp = jnp.where(pick, LOWEST, imp)
    own_blk = t0 // SLC_LEN
    sel_t = jnp.where((s_idx == own_blk) | (s_idx == own_blk + 1) | (s_idx * SLC_LEN > tq), 0.0, sel_t)
    selneg = ((sel_t.T - 1.0) * BIG).astype(BF16)
    q_aug = jnp.concatenate([jnp.concatenate([selneg] * p_heads, axis=0), qz16], axis=1)

    own0 = pl.multiple_of(WINDOW + t0, qb)
    s_own = _dot_nt(qz16, ks_ref[pl.ds(own0, qb), n_slc:n_slc + LANES])
    s_own = jnp.where((t0 + lax.broadcasted_iota(jnp.int32, (rows, qb), 1)) <= tpos, s_own, NEG)
    m_own = jnp.max(s_own, axis=-1, keepdims=True)
    acc_own = _dot(jnp.exp((s_own - m_own).astype(BF16)), vs_ref[pl.ds(own0, qb), :])

    def chunk_rows(c):
        return pl.ds(pl.multiple_of(WINDOW + c * kc_len, kc_len), kc_len)

    def sel_update(c, carry, scores):
        m, acc = carry
        m_new = jnp.maximum(m, jnp.max(scores, axis=-1, keepdims=True))
        prob = jnp.exp((scores - m_new).astype(BF16))
        return m_new, jnp.exp(m - m_new) * acc + _dot(prob, vs_ref[chunk_rows(c), :])

    def sel_scores(c):
        return _dot_nt(q_aug, ks_ref[chunk_rows(c), :])

    def sel_pair(i, carry):
        s_a = sel_scores(2 * i)
        s_b = sel_scores(2 * i + 1)
        return sel_update(2 * i + 1, sel_update(2 * i, carry, s_a), s_b)

    n_chunks = t0 // kc_len + 1
    carry = lax.fori_loop(0, n_chunks // 2, sel_pair, (m_own, acc_own))
    _, acc_s = lax.cond(n_chunks % 2 == 1,
                        lambda cr: sel_update(n_chunks - 1, cr, sel_scores(n_chunks - 1)),
                        lambda cr: cr, carry)
    o_s = acc_s * (1.0 / acc_s[:, DEN_LANE:DEN_LANE + 1])

    gts = _sigmoid(gate_ref[...])
    outs = []
    for p in range(p_heads):
        r = slice(p * qb, (p + 1) * qb)
        outs.append(gts[:, 3 * p:3 * p + 1] * o_c[r] + gts[:, 3 * p + 1:3 * p + 2] * o_s[r]
                    + gts[:, 3 * p + 2:3 * p + 3] * o_w[r])
    lane_o = lax.broadcasted_iota(jnp.int32, (qb, LANES), 1)
    for t in range(p_heads // 2):
        o_ref[:, t * LANES:(t + 1) * LANES] = jnp.where(lane_o < dk, outs[2 * t], pltpu.roll(outs[2 * t + 1], dk, 1))


def nsa_attention(pf, pkv, kvcmp):
    bsz, t, _ = pf.shape
    g, dk, qb = NSA_GROUPS, NSA_DK, Q_BLOCK
    rows = NSA_HPG * qb
    w_len = qb + WINDOW
    qw = NSA_HPG * dk
    slab = t // SLC_LEN + LANES
    gate_slab = NSA_Q // LANES + 2 * g
    kv_base = g * slab // LANES
    ri = jnp.arange(rows, dtype=jnp.int32)[:, None] % qb
    ki = jnp.arange(w_len, dtype=jnp.int32)[None, :]
    wbias = jnp.where((ki > ri) & (ki <= ri + WINDOW), 0.0, NEG).astype(F32)
    cmp_spec = pl.BlockSpec((None, None) + kvcmp.shape[3:], lambda b, c, j: (b, c, 0, 0))
    kv_spec = lambda first: pl.BlockSpec((None, t + WINDOW, LANES), lambda b, c, j: (b, 0, first + c))
    return pl.pallas_call(
        functools.partial(_nsa_kernel, t_len=t),
        grid=(bsz, g, t // qb),
        in_specs=[
            pl.BlockSpec((None, qb, qw), lambda b, c, j: (b, j, c)),
            pl.BlockSpec((None, qb, LANES), lambda b, c, j: (b, j, gate_slab + c)),
            cmp_spec, cmp_spec,
            pl.BlockSpec((None, t + WINDOW, slab), lambda b, c, j: (b, 0, c)),
            kv_spec(kv_base), kv_spec(kv_base + g), kv_spec(kv_base + 2 * g),
            pl.BlockSpec((rows, w_len), lambda b, c, j: (0, 0)),
        ],
        out_specs=pl.BlockSpec((None, qb, qw), lambda b, c, j: (b, j, c)),
        out_shape=jax.ShapeDtypeStruct((bsz, t, g * qw), F32),
        compiler_params=_params("arbitrary", "arbitrary", "arbitrary"),
        name="nsa_attention",
    )(pf, pf, kvcmp[0], kvcmp[1], pkv, pkv, pkv, pkv, wbias)


def nsa_mixer_core(x, sc, sh, w_in, pe_k, pe_v, ck_w1, ck_w2, cv_w1, cv_w2):
    pf, pkv = nsa_proj(x, sc, sh, w_in)
    kvcmp = nsa_compress(pf, jnp.stack([pe_k, pe_v]), jnp.stack([ck_w1, cv_w1]), jnp.stack([ck_w2, cv_w2]))
    return nsa_attention(pf, pkv, kvcmp)


def _outproj_ln_kernel(o_ref, w_ref, x_ref, gt_ref, g_ref, b_ref, sc_ref, sh_ref, wr_ref, br_ref,
                       x1_ref, hf_ref, rt_ref, idx_ref, cnt_ref, cnt_acc):
    @pl.when((pl.program_id(0) == 0) & (pl.program_id(1) == 0))
    def _():
        cnt_acc[...] = jnp.zeros_like(cnt_acc)

    y = _dot(o_ref[...].astype(BF16), w_ref[...])
    x1 = _layer_norm(ALPHA * x_ref[...] + (1.0 + gt_ref[...]) * y, g_ref[...], b_ref[...])
    x1_ref[...] = x1
    hf = x1 * (1.0 + sc_ref[...]) + sh_ref[...]
    hf_ref[...] = hf

    logits = _dot_hi(hf, wr_ref[...]) + br_ref[...]
    lane = lax.broadcasted_iota(jnp.int32, logits.shape, 1).astype(F32)
    is_grp = lane < N_GROUPS
    gl = jnp.where(is_grp, logits, LOWEST)
    gmax = jnp.max(gl, axis=-1, keepdims=True)
    gi = jnp.min(jnp.where(gl == gmax, lane, float(LANES)), axis=-1, keepdims=True)
    gp = 1.0 / jnp.sum(jnp.where(is_grp, jnp.exp(logits - gmax), 0.0), axis=-1, keepdims=True)
    lo = N_GROUPS + EXP_PER_GROUP * gi
    el = jnp.where((lane >= lo) & (lane < lo + EXP_PER_GROUP), logits, LOWEST)
    m1 = jnp.max(el, axis=-1, keepdims=True)
    i1 = jnp.min(jnp.where(el == m1, lane, float(LANES)), axis=-1, keepdims=True)
    el2 = jnp.where(lane == i1, LOWEST, el)
    m2 = jnp.max(el2, axis=-1, keepdims=True)
    i2 = jnp.min(jnp.where(el2 == m2, lane, float(LANES)), axis=-1, keepdims=True)
    e2 = jnp.exp(m2 - m1)
    g1 = gp / (1.0 + e2)
    g2 = gp * e2 / (1.0 + e2)

    tm = logits.shape[0]
    onehot = jnp.where((lane == i1) | (lane == i2), 1.0, 0.0)
    ri = lax.broadcasted_iota(jnp.int32, (tm, tm), 0)
    ci = lax.broadcasted_iota(jnp.int32, (tm, tm), 1)
    before = jnp.where(ci < ri, 1.0, 0.0).astype(BF16)
    seen = _dot(before, onehot.astype(BF16)) + cnt_acc[...]
    rank1 = jnp.sum(jnp.where(lane == i1, seen, 0.0), axis=-1, keepdims=True)
    rank2 = jnp.sum(jnp.where(lane == i2, seen, 0.0), axis=-1, keepdims=True)
    cnt_acc[...] = cnt_acc[...] + jnp.sum(onehot, axis=0, keepdims=True)
    cnt_ref[...] = cnt_acc[...]

    vals = (i1 - N_GROUPS, i2 - N_GROUPS, g1, g2, rank1, rank2)
    out = jnp.zeros_like(logits)
    for pos, val in enumerate(vals):
        out = jnp.where(lane == float(pos), val, out)
    rt_ref[...] = out
    idx_ref[...] = out.T[0:SUBLANES, :].astype(jnp.int32)


def outproj_ln(o, w_out, x, gt, ln_g, ln_b, sc2, sh2, w_router, b_router, tm=256):
    bsz, t, d = x.shape
    din = o.shape[-1]
    row = lambda: pl.BlockSpec((None, tm, d), lambda b, i: (b, i, 0))
    per_b = lambda: pl.BlockSpec((None, 1, d), lambda b, i: (b, 0, 0))
    const = lambda r, c: pl.BlockSpec((r, c), lambda b, i: (0, 0))
    return pl.pallas_call(
        _outproj_ln_kernel,
        grid=(bsz, t // tm),
        in_specs=[
            pl.BlockSpec((None, tm, din), lambda b, i: (b, i, 0)),
            const(din, d), row(), per_b(), const(1, d), const(1, d), per_b(), per_b(),
            const(d, LANES), const(1, LANES),
        ],
        out_specs=[row(), row(), pl.BlockSpec((None, tm, LANES), lambda b, i: (b, i, 0)),
                   pl.BlockSpec((None, None, SUBLANES, tm), lambda b, i: (b, i, 0, 0)), const(1, LANES)],
        out_shape=[jax.ShapeDtypeStruct((bsz, t, d), F32), jax.ShapeDtypeStruct((bsz, t, d), F32),
                   jax.ShapeDtypeStruct((bsz, t, LANES), F32),
                   jax.ShapeDtypeStruct((bsz, t // tm, SUBLANES, tm), jnp.int32),
                   jax.ShapeDtypeStruct((1, LANES), F32)],
        scratch_shapes=[pltpu.VMEM((1, LANES), F32)],
        compiler_params=_params("arbitrary", "arbitrary"),
        name="outproj_ln",
    )(o, w_out, x, gt, ln_g.reshape(1, d), ln_b.reshape(1, d), sc2, sh2, w_router, b_router)


def _moe_kernel(blk_e_ref, nused_ref, x_ref, wg_ref, wu_ref, wd_ref, o_ref, wg16, wu16, wd16):
    i = pl.program_id(0)
    e = blk_e_ref[i]
    e_prev = blk_e_ref[jnp.maximum(i - 1, 0)]

    @pl.when((i == 0) | (e != e_prev))
    def _():
        wg16[...] = wg_ref[...].astype(BF16)
        wu16[...] = wu_ref[...].astype(BF16)
        wd16[...] = wd_ref[...].astype(BF16)

    @pl.when(i < nused_ref[0])
    def _():
        x = x_ref[...].astype(BF16)
        gate = _dot(x, wg16[...])
        up = _dot(x, wu16[...])
        o_ref[...] = _dot((_silu(gate) * up).astype(BF16), wd16[...])

    @pl.when(i >= nused_ref[0])
    def _():
        o_ref[...] = jnp.zeros_like(o_ref)


def moe_ffn(xg, blk_e, n_used, w_gate, w_up, w_down, layer):
    rows, d = xg.shape
    de = w_gate.shape[-1]
    n_blk = rows // MOE_BLOCK
    grid_spec = pltpu.PrefetchScalarGridSpec(
        num_scalar_prefetch=2,
        grid=(n_blk,),
        in_specs=[
            pl.BlockSpec((MOE_BLOCK, d), lambda i, be, nu: (jnp.minimum(i, nu[0] - 1), 0)),
            pl.BlockSpec((None, None, d, de), lambda i, be, nu: (layer, be[i], 0, 0)),
            pl.BlockSpec((None, None, d, de), lambda i, be, nu: (layer, be[i], 0, 0)),
            pl.BlockSpec((None, None, de, d), lambda i, be, nu: (layer, be[i], 0, 0)),
        ],
        out_specs=pl.BlockSpec((MOE_BLOCK, d), lambda i, be, nu: (i, 0)),
        scratch_shapes=[pltpu.VMEM((d, de), BF16), pltpu.VMEM((d, de), BF16), pltpu.VMEM((de, d), BF16)],
    )
    return pl.pallas_call(
        _moe_kernel,
        grid_spec=grid_spec,
        out_shape=jax.ShapeDtypeStruct((rows, d), F32),
        compiler_params=_params("arbitrary"),
        name="moe_ffn",
    )(blk_e, n_used, xg, w_gate, w_up, w_down)


def _dispatch_kernel(ps_ref, fill_ref, idx_ref, x_ref, xg_out, dest_ref, zbuf, sem, *, tile, n_exp):
    def row_copy(r, d):
        return pltpu.make_async_copy(x_ref.at[pl.ds(r, 1)], xg_out.at[pl.ds(d, 1)], sem)

    @pl.when(pl.program_id(0) == 0)
    def _():
        zbuf[...] = jnp.zeros_like(zbuf)
        pieces = [SUBLANES << b for b in range((MOE_BLOCK // SUBLANES).bit_length() - 1)]

        def fill_copies(e, wait):
            start = fill_ref[e]
            head = (-start) & (SUBLANES - 1)
            rest = fill_ref[n_exp + e] - head
            for j in range(SUBLANES - 1):
                @pl.when(j < head)
                def _(j=j):
                    cp = pltpu.make_async_copy(zbuf.at[pl.ds(0, 1)], xg_out.at[pl.ds(start + j, 1)], sem)
                    cp.wait() if wait else cp.start()
            cur = start + head
            for size in pieces:
                @pl.when((rest & size) != 0)
                def _(cur=cur, size=size):
                    dst = xg_out.at[pl.ds(pl.multiple_of(cur, SUBLANES), size)]
                    cp = pltpu.make_async_copy(zbuf.at[pl.ds(0, size)], dst, sem)
                    cp.wait() if wait else cp.start()
                cur = cur + (rest & size)

        for e in range(n_exp):
            fill_copies(e, wait=False)
        for e in range(n_exp):
            fill_copies(e, wait=True)

        n_rows = xg_out.shape[0]
        half = zbuf.shape[0]

        def tail_copy(j, wait):
            dst = xg_out.at[pl.ds(pl.multiple_of(j * half, half), half)]
            cp = pltpu.make_async_copy(zbuf, dst, sem)
            cp.wait() if wait else cp.start()

        first_tail = fill_ref[2 * n_exp]
        lax.fori_loop(first_tail, n_rows // half, lambda j, c: (tail_copy(j, False), c)[1], 0)
        lax.fori_loop(first_tail, n_rows // half, lambda j, c: (tail_copy(j, True), c)[1], 0)

    copies = []
    for r in range(tile):
        for k in range(TOP_K):
            d = ps_ref[idx_ref[k, r]] + idx_ref[2 * TOP_K + k, r]
            dest_ref[k, r] = d
            copies.append(row_copy(r, d))
            copies[-1].start(priority=k % 2)
    for cp in copies:
        cp.wait()


def moe_dispatch(hf, idx, pad_start, fill, rows):
    n, d = hf.shape
    n_tiles, _, tile = idx.shape
    grid_spec = pltpu.PrefetchScalarGridSpec(
        num_scalar_prefetch=2,
        grid=(n_tiles,),
        in_specs=[
            pl.BlockSpec((None, SUBLANES, tile), lambda i, ps, fl: (i, 0, 0), memory_space=pltpu.SMEM),
            pl.BlockSpec((tile, d), lambda i, ps, fl: (i, 0)),
        ],
        out_specs=[
            pl.BlockSpec(memory_space=pl.ANY),
            pl.BlockSpec((None, TOP_K, tile), lambda i, ps, fl: (i, 0, 0), memory_space=pltpu.SMEM),
        ],
        scratch_shapes=[pltpu.VMEM((MOE_BLOCK // 2, d), hf.dtype), pltpu.SemaphoreType.DMA(())],
    )
    return pl.pallas_call(
        functools.partial(_dispatch_kernel, tile=tile, n_exp=fill.shape[0] // 2),
        grid_spec=grid_spec,
        out_shape=[jax.ShapeDtypeStruct((rows, d), hf.dtype),
                   jax.ShapeDtypeStruct((n_tiles, TOP_K, tile), jnp.int32)],
        compiler_params=_params("arbitrary"),
        name="moe_dispatch",
    )(pad_start, fill, idx, hf)


def _combine_ln_kernel(dest_ref, dest_next_ref, x_ref, rt_ref, gt_ref, g_ref, b_ref, yb_ref, o_ref, buf, sems,
                       *, tile):
    i = pl.program_id(0)
    slot = i % 2

    def row_copies(dests, s):
        return [pltpu.make_async_copy(yb_ref.at[pl.ds(dests[k, r], 1)], buf.at[s, k, pl.ds(r, 1)], sems.at[s])
                for r in range(tile) for k in range(TOP_K)]

    @pl.when(i == 0)
    def _():
        for n, cp in enumerate(row_copies(dest_ref, slot)):
            cp.start(priority=n % 2)

    @pl.when(i + 1 < pl.num_programs(0))
    def _():
        for n, cp in enumerate(row_copies(dest_next_ref, 1 - slot)):
            cp.start(priority=n % 2)

    for r in range(tile):
        for k in range(TOP_K):
            pltpu.make_async_copy(yb_ref.at[pl.ds(0, 1)], buf.at[slot, k, pl.ds(r, 1)], sems.at[slot]).wait()
    rt = rt_ref[...]
    y = rt[:, TOP_K:TOP_K + 1] * buf[slot, 0]
    for k in range(1, TOP_K):
        y = y + rt[:, TOP_K + k:TOP_K + k + 1] * buf[slot, k]
    o_ref[...] = _layer_norm(ALPHA * x_ref[...] + (1.0 + gt_ref[...]) * y, g_ref[...], b_ref[...])


def moe_combine_ln(dest, yb, x1, route, gt, ln_g, ln_b):
    bsz, t, d = x1.shape
    n = bsz * t
    n_tiles, _, tile = dest.shape
    per_b = t // tile
    row = lambda w: pl.BlockSpec((tile, w), lambda i: (i, 0))
    out = pl.pallas_call(
        functools.partial(_combine_ln_kernel, tile=tile),
        grid=(n_tiles,),
        in_specs=[
            pl.BlockSpec((None, TOP_K, tile), lambda i: (i, 0, 0), memory_space=pltpu.SMEM),
            pl.BlockSpec((None, TOP_K, tile), lambda i: (jnp.minimum(i + 1, n_tiles - 1), 0, 0),
                         memory_space=pltpu.SMEM),
            row(d), row(LANES),
            pl.BlockSpec((None, 1, d), lambda i: (i // per_b, 0, 0)),
            pl.BlockSpec((1, d), lambda i: (0, 0)), pl.BlockSpec((1, d), lambda i: (0, 0)),
            pl.BlockSpec(memory_space=pl.ANY),
        ],
        out_specs=row(d),
        out_shape=jax.ShapeDtypeStruct((n, d), F32),
        scratch_shapes=[pltpu.VMEM((2, TOP_K, tile, d), F32), pltpu.SemaphoreType.DMA((2,))],
        compiler_params=_params("arbitrary"),
        name="moe_combine_ln",
    )(dest, dest, x1.reshape(n, d), route.reshape(n, LANES), gt, ln_g.reshape(1, d), ln_b.reshape(1, d), yb)
    return out.reshape(bsz, t, d)


def hier_moe_ln(x1, hf, route, idx, counts, gt, ln_g, ln_b, w_gate, w_up, w_down, layer):
    bsz, t, d = hf.shape
    n = bsz * t
    n_exp = w_gate.shape[1]
    counts = counts[0, N_GROUPS:N_GROUPS + n_exp].astype(jnp.int32)
    padded = (counts + MOE_BLOCK - 1) // MOE_BLOCK * MOE_BLOCK
    pad_end = jnp.cumsum(padded)
    pad_start = (pad_end - padded).astype(jnp.int32)
    n_blk = -(-(n * TOP_K) // MOE_BLOCK) + n_exp
    blk_first = jnp.arange(n_blk, dtype=jnp.int32) * MOE_BLOCK
    blk_e = jnp.minimum(jnp.sum((pad_end[None, :] <= blk_first[:, None]).astype(jnp.int32), axis=1), n_exp - 1)
    n_used = (pad_end[-1:] // MOE_BLOCK).astype(jnp.int32)
    idx = idx.reshape((-1,) + idx.shape[2:])
    first_tail_half = pad_end[-1:] // (MOE_BLOCK // 2)
    fill = jnp.concatenate([pad_start + counts, padded - counts, first_tail_half]).astype(jnp.int32)
    xg, dest = moe_dispatch(hf.reshape(n, d), idx, pad_start, fill, n_blk * MOE_BLOCK)
    yb = moe_ffn(xg, blk_e.astype(jnp.int32), n_used, w_gate, w_up, w_down, layer)
    return moe_combine_ln(dest, yb, x1, route, gt, ln_g, ln_b)


def _pad_cols(w, mult=LANES):
    pad = (-w.shape[-1]) % mult
    return jnp.pad(w, ((0, 0), (0, pad))) if pad else w


def kernel(x, c, ada_w, ada_b, ln1_g, ln1_b, ln2_g, ln2_b, gdn_w_in, gdn_conv_w, gdn_a_log, gdn_dt_bias, gdn_norm_g, gdn_w_out, lru_w_in, lru_conv_w, lru_conv_b, lru_w_a, lru_b_a, lru_w_x, lru_b_x, lru_lambda, lru_w_out, nsa_w_in, nsa_pe_k, nsa_pe_v, nsa_ck_w1, nsa_ck_w2, nsa_cv_w1, nsa_cv_w2, nsa_w_out, moe_w_grp, moe_b_grp, moe_w_exp, moe_b_exp, moe_w_gate, moe_w_up, moe_w_down):
    bsz, t, d = x.shape
    depth = ada_w.shape[0]
    mod = ada_modulation(c, ada_w, ada_b).reshape(depth, bsz, 6, 1, d)
    ja = jb = jc = 0
    for i in range(depth):
        sh1, sc1, gt1, sh2, sc2, gt2 = (mod[i, :, k] for k in range(6))
        kind = i % N_MIXERS
        if kind == 0:
            p = modproj(x, sc1, sh1, _pad_cols(gdn_w_in[ja]).astype(BF16))
            o = gdn_core(p, gdn_conv_w[ja], gdn_a_log[ja], gdn_dt_bias[ja], gdn_norm_g[ja])
            w_out = gdn_w_out[ja]
            ja += 1
        elif kind == 1:
            p = modproj(x, sc1, sh1, lru_w_in[jb].astype(BF16))
            o = lru_core(p, lru_conv_w[jb], lru_conv_b[jb], lru_w_a[jb], lru_b_a[jb], lru_w_x[jb], lru_b_x[jb],
                         lru_lambda[jb])
            w_out = lru_w_out[jb]
            jb += 1
        else:
            o = nsa_mixer_core(x, sc1, sh1, nsa_w_in[jc], nsa_pe_k[jc], nsa_pe_v[jc], nsa_ck_w1[jc], nsa_ck_w2[jc],
                               nsa_cv_w1[jc], nsa_cv_w2[jc])
            w_out = nsa_w_out[jc]
            jc += 1
        w_router = _pad_cols(jnp.concatenate([moe_w_grp[i], moe_w_exp[i]], axis=1))
        b_router = _pad_cols(jnp.concatenate([moe_b_grp[i], moe_b_exp[i]])[None, :])
        x1, hf, route, idx, counts = outproj_ln(o, w_out.astype(BF16), x, gt1, ln1_g[i], ln1_b[i], sc2, sh2,
                                                w_router, b_router)
        x = hier_moe_ln(x1, hf, route, idx, counts, gt2, ln2_g[i], ln2_b[i], moe_w_gate, moe_w_up, moe_w_down, i)
    return x
```

```python
import functools

import jax
import jax.numpy as jnp
from jax import lax
from jax.experimental import pallas as pl
from jax.experimental.pallas import tpu as pltpu

F32 = jnp.float32
BF16 = jnp.bfloat16

D_MODEL = 1024
DEPTH = 4
N_MIXERS = 3
LN_EPS = 1e-5
NORM_EPS = 1e-6
CONV_K = 4
ALPHA = (2.0 * DEPTH) ** 0.25
GDN_HEADS = 8
GDN_DK = 128
GDN_CHUNK = 64
GDN_QKV = 3 * GDN_HEADS * GDN_DK
LRU_WIDTH = 1024
LRU_BLOCKS = 16
LRU_BW = LRU_WIDTH // LRU_BLOCKS
RG_C = 8.0
NSA_GROUPS = 4
NSA_HPG = 4
NSA_DK = 64
NSA_Q = NSA_GROUPS * NSA_HPG * NSA_DK
NSA_KV = NSA_GROUPS * NSA_DK
CMP_LEN = 32
CMP_STRIDE = 16
SLC_LEN = 64
N_SEL = 16
WINDOW = 512
Q_BLOCK = 128
SEL_KEY_CHUNK = 512
N_GROUPS = 4
EXP_PER_GROUP = 8
N_EXPERTS = N_GROUPS * EXP_PER_GROUP
TOP_K = 2
MOE_BLOCK = 256
NEG = -1e30
BIG = 1e30
LOWEST = -3e38

LANES = 128
SUBLANES = 8
VMEM_LIMIT_BYTES = 48 * 1024 * 1024


def _params(*sem):
    return pltpu.CompilerParams(dimension_semantics=sem, vmem_limit_bytes=VMEM_LIMIT_BYTES)


def _sigmoid(x):
    return 1.0 / (1.0 + jnp.exp(-x))


def _silu(x):
    return x * _sigmoid(x)


def _softplus(x):
    return jnp.maximum(x, 0.0) + jnp.log(1.0 + jnp.exp(-jnp.abs(x)))


def _gelu(x):
    return 0.5 * x * (1.0 + jnp.tanh(0.7978845608028654 * (x + 0.044715 * (x * x * x))))


def _dot(a, b):
    return jnp.dot(a, b, preferred_element_type=F32)


def _dot_nt(a, b):
    return lax.dot_general(a, b, (((1,), (1,)), ((), ())), preferred_element_type=F32)


def _dot_tn(a, b):
    return lax.dot_general(a, b, (((0,), (0,)), ((), ())), preferred_element_type=F32)


def _split3(x):
    hi = x.astype(BF16)
    r1 = x - hi.astype(F32)
    mid = r1.astype(BF16)
    lo = (r1 - mid.astype(F32)).astype(BF16)
    return hi, mid, lo


def _dot_lhs_exact(m_bf16, x):
    hi, mid, lo = _split3(x)
    return _dot(m_bf16, hi) + _dot(m_bf16, mid) + _dot(m_bf16, lo)


def _dot_f32(a, b):
    ah, am, al = _split3(a)
    bh, bm, bl = _split3(b)
    return (_dot(ah, bh) + _dot(ah, bm) + _dot(am, bh)) + (_dot(ah, bl) + _dot(al, bh) + _dot(am, bm))


def _dot_hi(a, b):
    ah = a.astype(BF16)
    al = (a - ah.astype(F32)).astype(BF16)
    bh = b.astype(BF16)
    bl = (b - bh.astype(F32)).astype(BF16)
    return _dot(ah, bh) + (_dot(ah, bl) + _dot(al, bh))


def _layer_norm(z, g, b):
    mu = jnp.mean(z, axis=-1, keepdims=True)
    zc = z - mu
    var = jnp.mean(zc * zc, axis=-1, keepdims=True)
    return zc * lax.rsqrt(var + LN_EPS) * g + b


def _causal_conv(xbuf, cw_ref, col0, width, tb):
    cols = slice(col0, col0 + width)
    acc = cw_ref[0:1, cols] * xbuf[SUBLANES - 3:SUBLANES - 3 + tb, cols]
    for k in range(1, CONV_K):
        off = SUBLANES - 3 + k
        acc = acc + cw_ref[k:k + 1, cols] * xbuf[off:off + tb, cols]
    return acc


def _ada_kernel(c_ref, w_ref, b_ref, o_ref):
    cond = _silu(c_ref[...])
    o_ref[...] = _dot_f32(cond, w_ref[...]) + b_ref[...]


def ada_modulation(c, ada_w, ada_b, col_tile=1536):
    depth, d, n6 = ada_w.shape
    bsz = c.shape[0]
    return pl.pallas_call(
        _ada_kernel,
        grid=(depth, n6 // col_tile),
        in_specs=[
            pl.BlockSpec((bsz, d), lambda i, j: (0, 0)),
            pl.BlockSpec((None, d, col_tile), lambda i, j: (i, 0, j)),
            pl.BlockSpec((None, 1, col_tile), lambda i, j: (i, 0, j)),
        ],
        out_specs=pl.BlockSpec((None, bsz, col_tile), lambda i, j: (i, 0, j)),
        out_shape=jax.ShapeDtypeStruct((depth, bsz, n6), F32),
        compiler_params=_params("arbitrary", "arbitrary"),
        name="ada_modulation",
    )(c, ada_w, ada_b.reshape(depth, 1, n6))


def _modproj_kernel(x_ref, sc_ref, sh_ref, w_ref, o_ref, *, col_chunk):
    h = (x_ref[...] * (1.0 + sc_ref[...]) + sh_ref[...]).astype(BF16)
    ncols = o_ref.shape[-1]
    for c0 in range(0, ncols, col_chunk):
        c1 = min(c0 + col_chunk, ncols)
        o_ref[:, c0:c1] = _dot(h, w_ref[:, c0:c1])


def modproj(x, sc, sh, w, tm=256, col_chunk=1024):
    bsz, t, d = x.shape
    ncols = w.shape[1]
    return pl.pallas_call(
        functools.partial(_modproj_kernel, col_chunk=col_chunk),
        grid=(bsz, t // tm),
        in_specs=[
            pl.BlockSpec((None, tm, d), lambda b, i: (b, i, 0)),
            pl.BlockSpec((None, 1, d), lambda b, i: (b, 0, 0)),
            pl.BlockSpec((None, 1, d), lambda b, i: (b, 0, 0)),
            pl.BlockSpec((d, ncols), lambda b, i: (0, 0)),
        ],
        out_specs=pl.BlockSpec((None, tm, ncols), lambda b, i: (b, i, 0)),
        out_shape=jax.ShapeDtypeStruct((bsz, t, ncols), F32),
        compiler_params=_params("arbitrary", "arbitrary"),
        name="modproj",
    )(x, sc, sh, w)


def _gdn_kernel(qkv_ref, z_ref, ab_ref, cw_ref, alog_ref, dtb_ref, ng_ref, o_ref, xbuf, s_ref, *, tb):
    nh, dk, chunk = GDN_HEADS, GDN_DK, GDN_CHUNK

    @pl.when(pl.program_id(1) == 0)
    def _():
        xbuf[0:SUBLANES, :] = jnp.zeros((SUBLANES, GDN_QKV), F32)
        s_ref[...] = jnp.zeros_like(s_ref)

    xbuf[SUBLANES:SUBLANES + tb, :] = qkv_ref[...]

    ab = ab_ref[...]
    g_all = -jnp.exp(alog_ref[...]) * _softplus(ab + dtb_ref[...])
    beta_all = _sigmoid(ab)
    ri = lax.broadcasted_iota(jnp.int32, (tb, tb), 0)
    ci = lax.broadcasted_iota(jnp.int32, (tb, tb), 1)
    same = (ri // chunk) == (ci // chunk)
    causal = same & (ci <= ri)
    strict = same & (ci < ri)
    eye = jnp.where(ri == ci, 1.0, 0.0)
    tri = jnp.where(causal, 1.0, 0.0).astype(BF16)
    ones_blk = jnp.where(same, 1.0, 0.0).astype(BF16)
    gc_all = _dot_lhs_exact(tri, g_all)
    glast_all = _dot_lhs_exact(ones_blk, g_all)
    gc_t = gc_all.T
    egc_all = jnp.exp(gc_all)
    kdec_all = jnp.exp(glast_all - gc_all)

    heads = range(nh)
    lmats, rhss, qk16s, qg16s, ks16s, gls = [], [], [], [], [], []
    for h in heads:
        q = _silu(_causal_conv(xbuf, cw_ref, h * dk, dk, tb))
        k = _silu(_causal_conv(xbuf, cw_ref, (nh + h) * dk, dk, tb))
        v = _silu(_causal_conv(xbuf, cw_ref, (2 * nh + h) * dk, dk, tb))
        q = q * lax.rsqrt(jnp.sum(q * q, axis=-1, keepdims=True) + NORM_EPS)
        k = k * lax.rsqrt(jnp.sum(k * k, axis=-1, keepdims=True) + NORM_EPS)
        beta = beta_all[:, nh + h:nh + h + 1]
        eg = egc_all[:, h:h + 1]
        decay = jnp.exp(jnp.where(causal, gc_all[:, h:h + 1] - gc_t[h:h + 1, :], NEG))
        kb = k * beta
        k16 = k.astype(BF16)
        qs = q * (dk ** -0.5)
        lmats.append(jnp.where(strict, _dot_nt(kb.astype(BF16), k16) * decay, 0.0))
        rhss.append(jnp.concatenate([v * beta, kb * eg], axis=1).astype(BF16))
        qk16s.append((_dot_nt(qs.astype(BF16), k16) * decay).astype(BF16))
        qg16s.append((qs * eg).astype(BF16))
        ks16s.append((k * kdec_all[:, h:h + 1]).astype(BF16))
        gls.append(glast_all[:, h:h + 1])
    xbuf[0:SUBLANES, :] = xbuf[tb:tb + SUBLANES, :]

    xpows = lmats
    tinvs = [eye - lm for lm in lmats]
    for _ in range(5):
        xpows = [_dot(x16, x16) for x16 in [x.astype(BF16) for x in xpows]]
        tinvs = [t + _dot(t.astype(BF16), x.astype(BF16)) for t, x in zip(tinvs, xpows)]
    uws = [_dot(t.astype(BF16), rhs) for t, rhs in zip(tinvs, rhss)]
    us = [uw[:, :dk] for uw in uws]
    w16s = [uw[:, dk:].astype(BF16) for uw in uws]

    states = [s_ref[h] for h in heads]
    o_parts = [[] for _ in heads]
    vn_parts = [[] for _ in heads]
    for c in range(tb // chunk):
        rows = slice(c * chunk, (c + 1) * chunk)
        for h in heads:
            s16 = states[h].astype(BF16)
            vn16 = (us[h][rows] - _dot(w16s[h][rows], s16)).astype(BF16)
            o_parts[h].append(_dot(qg16s[h][rows], s16))
            states[h] = (states[h] * jnp.exp(gls[h][c * chunk:c * chunk + 1, :])
                         + _dot_tn(ks16s[h][rows], vn16))
            vn_parts[h].append(vn16)
    for h in heads:
        s_ref[h] = states[h]
        o = jnp.concatenate(o_parts[h], axis=0) + _dot(qk16s[h], jnp.concatenate(vn_parts[h], axis=0))
        o = o * lax.rsqrt(jnp.mean(o * o, axis=-1, keepdims=True) + NORM_EPS) * ng_ref[...]
        o_ref[:, h * dk:(h + 1) * dk] = o * _silu(z_ref[:, h * dk:(h + 1) * dk])


def gdn_core(p, conv_w, a_log, dt_bias, norm_g, tb=128):
    bsz, t, _ = p.shape
    nh, dk = GDN_HEADS, GDN_DK
    v_w = nh * dk
    alog = jnp.zeros((1, LANES), F32).at[0, :nh].set(a_log)
    dtb = jnp.zeros((1, LANES), F32).at[0, :nh].set(dt_bias)
    ab_blk = (GDN_QKV + v_w) // LANES
    return pl.pallas_call(
        functools.partial(_gdn_kernel, tb=tb),
        grid=(bsz, t // tb),
        in_specs=[
            pl.BlockSpec((None, tb, GDN_QKV), lambda b, i: (b, i, 0)),
            pl.BlockSpec((None, tb, v_w), lambda b, i: (b, i, GDN_QKV // v_w)),
            pl.BlockSpec((None, tb, LANES), lambda b, i: (b, i, ab_blk)),
            pl.BlockSpec((CONV_K, GDN_QKV), lambda b, i: (0, 0)),
            pl.BlockSpec((1, LANES), lambda b, i: (0, 0)),
            pl.BlockSpec((1, LANES), lambda b, i: (0, 0)),
            pl.BlockSpec((1, dk), lambda b, i: (0, 0)),
        ],
        out_specs=pl.BlockSpec((None, tb, v_w), lambda b, i: (b, i, 0)),
        out_shape=jax.ShapeDtypeStruct((bsz, t, v_w), F32),
        scratch_shapes=[pltpu.VMEM((tb + SUBLANES, GDN_QKV), F32), pltpu.VMEM((nh, dk, dk), F32)],
        compiler_params=_params("arbitrary", "arbitrary"),
        name="gdn_core",
    )(p, p, p, conv_w, alog, dtb, norm_g.reshape(1, dk))


def _lru_kernel(gate_ref, xb_ref, cw_ref, cb_ref, wax_ref, ba_ref, bx_ref, lam_ref, o_ref, xbuf, h_ref, *, tb):
    width = LRU_WIDTH
    grp = 2 * LANES

    @pl.when(pl.program_id(1) == 0)
    def _():
        xbuf[0:SUBLANES, :] = jnp.zeros((SUBLANES, width), F32)
        h_ref[...] = jnp.zeros_like(h_ref)

    xbuf[SUBLANES:SUBLANES + tb, :] = xb_ref[...]
    row = lax.broadcasted_iota(jnp.int32, (tb, grp), 0)

    for j in range(width // grp):
        cols = slice(j * grp, (j + 1) * grp)
        xc = _causal_conv(xbuf, cw_ref, j * grp, grp, tb) + cb_ref[:, cols]
        res = _dot(xc.astype(BF16), wax_ref[j])
        r = _sigmoid(res[:, :grp] + ba_ref[:, cols])
        ig = _sigmoid(res[:, grp:] + bx_ref[:, cols])
        log_a = -RG_C * r * _softplus(-lam_ref[:, cols])
        a = jnp.exp(log_a)
        mult = jnp.sqrt(jnp.maximum(1.0 - jnp.exp(2.0 * log_a), 0.0))
        u = mult * ig * xc
        s = 1
        while s < tb:
            a_sh = jnp.where(row < s, 1.0, pltpu.roll(a, s, 0))
            u_sh = jnp.where(row < s, 0.0, pltpu.roll(u, s, 0))
            u = a * u_sh + u
            a = a * a_sh
            s *= 2
        hs = u + a * h_ref[:, cols]
        h_ref[:, cols] = hs[tb - 1:tb, :]
        o_ref[:, cols] = hs * _gelu(gate_ref[:, cols])

    xbuf[0:SUBLANES, :] = xbuf[tb:tb + SUBLANES, :]


def _block_diag_groups(w):
    per = (2 * LANES) // LRU_BW
    wg = w.reshape(LRU_BLOCKS // per, per, LRU_BW, LRU_BW)
    eye = jnp.eye(per, dtype=w.dtype)
    return jnp.einsum('gaij,ab->gaibj', wg, eye).reshape(LRU_BLOCKS // per, per * LRU_BW, per * LRU_BW)


def lru_core(p, conv_w, conv_b, w_a, b_a, w_x, b_x, lam, tb=256):
    bsz, t, _ = p.shape
    width = LRU_WIDTH
    grp = 2 * LANES
    wax = jnp.concatenate([_block_diag_groups(w_a), _block_diag_groups(w_x)], axis=-1).astype(BF16)
    vec = lambda a: a.reshape(1, width)
    return pl.pallas_call(
        functools.partial(_lru_kernel, tb=tb),
        grid=(bsz, t // tb),
        in_specs=[
            pl.BlockSpec((None, tb, width), lambda b, i: (b, i, 0)),
            pl.BlockSpec((None, tb, width), lambda b, i: (b, i, 1)),
            pl.BlockSpec((CONV_K, width), lambda b, i: (0, 0)),
            pl.BlockSpec((1, width), lambda b, i: (0, 0)),
            pl.BlockSpec((width // grp, grp, 2 * grp), lambda b, i: (0, 0, 0)),
            pl.BlockSpec((1, width), lambda b, i: (0, 0)),
            pl.BlockSpec((1, width), lambda b, i: (0, 0)),
            pl.BlockSpec((1, width), lambda b, i: (0, 0)),
        ],
        out_specs=pl.BlockSpec((None, tb, width), lambda b, i: (b, i, 0)),
        out_shape=jax.ShapeDtypeStruct((bsz, t, width), F32),
        scratch_shapes=[pltpu.VMEM((tb + SUBLANES, width), F32), pltpu.VMEM((1, width), F32)],
        compiler_params=_params("arbitrary", "arbitrary"),
        name="lru_core",
    )(p, p, conv_w, vec(conv_b), wax, vec(b_a), vec(b_x), vec(lam))


DEN_LANE = NSA_DK


def _nsa_proj_kernel(x_ref, sc_ref, sh_ref, wf_ref, wkv_ref, bkv_ref, pf_ref, pkv_ref, *, tm, n_slc, pad_tiles):
    g = NSA_GROUPS
    slab = n_slc + LANES
    i = pl.program_id(1)

    @pl.when(i < pad_tiles)
    def _():
        lane = lax.broadcasted_iota(jnp.int32, pkv_ref.shape, 1)
        kw0 = g * slab + g * LANES
        flag = (lane >= kw0) & (lane < kw0 + g * LANES) & ((lane - kw0) % LANES == DEN_LANE)
        pkv_ref[...] = jnp.where(flag, 1.0, 0.0).astype(pkv_ref.dtype)

    @pl.when(i >= pad_tiles)
    def _():
        h = (x_ref[...] * (1.0 + sc_ref[...]) + sh_ref[...]).astype(BF16)
        nf = pf_ref.shape[-1]
        for c0 in range(0, nf, 1024):
            c1 = min(c0 + 1024, nf)
            pf_ref[:, c0:c1] = _dot(h, wf_ref[:, c0:c1])
        tok = (i - pad_tiles) * tm + lax.broadcasted_iota(jnp.int32, (tm, n_slc), 0)
        onehot = jnp.where(tok // SLC_LEN == lax.broadcasted_iota(jnp.int32, (tm, n_slc), 1), 1.0, 0.0)
        for c in range(g):
            pkv_ref[:, c * slab:c * slab + n_slc] = onehot.astype(pkv_ref.dtype)
            pkv_ref[:, c * slab + n_slc:(c + 1) * slab] = _dot(h, wkv_ref[:, c * LANES:(c + 1) * LANES]).astype(
                pkv_ref.dtype)
        rest = _dot(h, wkv_ref[:, g * LANES:]) + bkv_ref[...]
        pkv_ref[:, g * slab:] = rest.astype(pkv_ref.dtype)


def nsa_proj(x, sc, sh, w_in, tm=256):
    bsz, t, d = x.shape
    g, dk = NSA_GROUPS, NSA_DK
    n_slc = t // SLC_LEN
    pad_tiles = WINDOW // tm
    pad_lanes = lambda w: jnp.pad(w, ((0, 0),) * (w.ndim - 1) + ((0, LANES - w.shape[-1]),))
    streams = pad_lanes(w_in[:, NSA_Q:NSA_Q + 6 * NSA_KV].reshape(d, 6, g, dk))
    gate_w = pad_lanes(w_in[:, NSA_Q + 6 * NSA_KV:].reshape(d, g, 3 * NSA_HPG))
    wf = jnp.concatenate([w_in[:, :NSA_Q], streams[:, 0:2].reshape(d, -1), gate_w.reshape(d, -1)], axis=1)
    wkv = streams[:, 2:6].reshape(d, -1)
    den = jnp.zeros((LANES,), F32).at[DEN_LANE].set(1.0)
    bkv = jnp.concatenate([jnp.tile(den, g), jnp.zeros((g * LANES,), F32), jnp.tile(den, g)])[None, :]
    nf, nkv = wf.shape[1], g * (n_slc + LANES) + 3 * g * LANES
    tok_tile = lambda b, i: (b, jnp.maximum(i - pad_tiles, 0), 0)
    return pl.pallas_call(
        functools.partial(_nsa_proj_kernel, tm=tm, n_slc=n_slc, pad_tiles=pad_tiles),
        grid=(bsz, t // tm + pad_tiles),
        in_specs=[
            pl.BlockSpec((None, tm, d), tok_tile),
            pl.BlockSpec((None, 1, d), lambda b, i: (b, 0, 0)),
            pl.BlockSpec((None, 1, d), lambda b, i: (b, 0, 0)),
            pl.BlockSpec(wf.shape, lambda b, i: (0, 0)),
            pl.BlockSpec(wkv.shape, lambda b, i: (0, 0)),
            pl.BlockSpec(bkv.shape, lambda b, i: (0, 0)),
        ],
        out_specs=[pl.BlockSpec((None, tm, nf), tok_tile),
                   pl.BlockSpec((None, tm, nkv), lambda b, i: (b, i, 0))],
        out_shape=[jax.ShapeDtypeStruct((bsz, t, nf), F32),
                   jax.ShapeDtypeStruct((bsz, t + WINDOW, nkv), BF16)],
        compiler_params=_params("arbitrary", "arbitrary"),
        name="nsa_proj",
    )(x, sc, sh, wf.astype(BF16), wkv.astype(BF16), bkv)


def _cmp_kernel(x_ref, pe_ref, w1a_ref, w1b_ref, w2_ref, o_ref):
    nseg = o_ref.shape[0]
    first = second = None
    for l in range(CMP_STRIDE):
        xl = x_ref[pl.ds(l, nseg, stride=CMP_STRIDE), :]
        a = _dot((xl + pe_ref[l:l + 1, :]).astype(BF16), w1a_ref[l])
        b = _dot((xl + pe_ref[CMP_STRIDE + l:CMP_STRIDE + l + 1, :]).astype(BF16), w1b_ref[l])
        first = a if first is None else first + a
        second = b if second is None else second + b
    hid = _gelu(first + pltpu.roll(second, nseg - 1, 0))
    o_ref[...] = _dot(hid.astype(BF16), w2_ref[...]).astype(o_ref.dtype)


def nsa_compress(pf, pe, w1, w2):
    bsz, t, _ = pf.shape
    g, dk = NSA_GROUPS, NSA_DK
    nseg = t // CMP_STRIDE
    hid = w1.shape[-1]
    pad_mid = lambda w: jnp.pad(w, ((0, 0), (0, 0), (0, LANES - dk), (0, 0)))
    w1r = w1.reshape(2, CMP_LEN, dk, hid)
    w1a = pad_mid(w1r[:, :CMP_STRIDE]).astype(BF16)
    w1b = pad_mid(w1r[:, CMP_STRIDE:]).astype(BF16)
    pe_p = jnp.pad(pe, ((0, 0), (0, 0), (0, LANES - dk)))
    w2p = jnp.pad(w2, ((0, 0), (0, 0), (0, LANES - dk))).astype(BF16)
    first_slab = NSA_Q // LANES
    return pl.pallas_call(
        _cmp_kernel,
        grid=(2, bsz, g),
        in_specs=[
            pl.BlockSpec((None, t, LANES), lambda a, b, c: (b, 0, first_slab + a * g + c)),
            pl.BlockSpec((None, CMP_LEN, LANES), lambda a, b, c: (a, 0, 0)),
            pl.BlockSpec((None, CMP_STRIDE, LANES, hid), lambda a, b, c: (a, 0, 0, 0)),
            pl.BlockSpec((None, CMP_STRIDE, LANES, hid), lambda a, b, c: (a, 0, 0, 0)),
            pl.BlockSpec((None, hid, LANES), lambda a, b, c: (a, 0, 0)),
        ],
        out_specs=pl.BlockSpec((None, None, None, nseg, LANES), lambda a, b, c: (a, b, c, 0, 0)),
        out_shape=jax.ShapeDtypeStruct((2, bsz, g, nseg, LANES), BF16),
        compiler_params=_params("arbitrary", "arbitrary", "arbitrary"),
        name="nsa_compress",
    )(pf, pe_p, w1a, w1b, w2p)


def _nsa_kernel(q_ref, gate_ref, kc_ref, vc_ref, ks_ref, vs_ref, kw_ref, vw_ref, wbias_ref, o_ref, *, t_len):
    p_heads, dk, qb = NSA_HPG, NSA_DK, Q_BLOCK
    rows = p_heads * qb
    n_cmp_rows = kc_ref.shape[0]
    n_slc = t_len // SLC_LEN
    n_sel = min(N_SEL, n_slc)
    kc_len = SEL_KEY_CHUNK
    t0 = pl.program_id(2) * qb

    qg = q_ref[...]
    halves = []
    for p in range(p_heads):
        tile = qg[:, (p // 2) * LANES:(p // 2 + 1) * LANES]
        halves.append(tile if p % 2 == 0 else pltpu.roll(tile, dk, 1))
    lane = lax.broadcasted_iota(jnp.int32, (rows, LANES), 1)
    qz = jnp.where(lane < dk, jnp.concatenate(halves, axis=0) * (dk ** -0.5), 0.0)
    qz16 = qz.astype(BF16)
    tpos = t0 + (lax.broadcasted_iota(jnp.int32, (rows, 1), 0) % qb)

    w_len = qb + WINDOW
    w0 = pl.multiple_of(t0, qb)
    qw16 = jnp.where(lane == DEN_LANE, NEG, qz).astype(BF16)
    s_w = _dot_nt(qw16, kw_ref[pl.ds(w0, w_len), :]) + wbias_ref[...]
    e_w = jnp.exp(s_w - jnp.max(s_w, axis=-1, keepdims=True))
    acc_w = _dot(e_w.astype(BF16), vw_ref[pl.ds(w0, w_len), :])
    o_w = acc_w * (1.0 / acc_w[:, DEN_LANE:DEN_LANE + 1])

    s_c = _dot_nt(qz16, kc_ref[...])
    n_idx = lax.broadcasted_iota(jnp.int32, (rows, n_cmp_rows), 1)
    vis = (n_idx * CMP_STRIDE + (CMP_LEN - 1)) <= tpos
    s_c = jnp.where(vis, s_c, NEG)
    e_c = jnp.where(vis, jnp.exp(s_c - jnp.max(s_c, axis=-1, keepdims=True)), 0.0)
    l_c = jnp.sum(e_c, axis=-1, keepdims=True)
    p_c = e_c * (1.0 / jnp.where(l_c > 0.0, l_c, 1.0))
    o_c = _dot(p_c.astype(BF16), vc_ref[...])

    p_sum = p_c[0:qb]
    for p in range(1, p_heads):
        p_sum = p_sum + p_c[p * qb:(p + 1) * qb]
    sj = lax.broadcasted_iota(jnp.int32, (n_slc, n_cmp_rows), 0) * SLC_LEN
    cn = lax.broadcasted_iota(jnp.int32, (n_slc, n_cmp_rows), 1) * CMP_STRIDE
    ov = jnp.maximum(jnp.minimum(cn + CMP_LEN, sj + SLC_LEN) - jnp.maximum(cn, sj), 0)
    overlap_t = (ov.astype(F32) * (1.0 / CMP_LEN)).astype(BF16)
    imp = None
    for piece in _split3(p_sum):
        part = _dot_nt(overlap_t, piece)
        imp = part if imp is None else imp + part
    s_idx = lax.broadcasted_iota(jnp.int32, (n_slc, qb), 0)
    tq = t0 + lax.broadcasted_iota(jnp.int32, (n_slc, qb), 1)
    cur = tq // SLC_LEN
    forced = (s_idx == 0) | (s_idx == cur) | (s_idx == cur - 1)
    imp = jnp.where(forced, BIG, imp)
    imp = jnp.where(s_idx * SLC_LEN <= tq, imp, -BIG)
    s_idx_f = s_idx.astype(F32)
    sel_t = jnp.zeros((n_slc, qb), F32)
    for _ in range(n_sel):
        m = jnp.max(imp, axis=0, keepdims=True)
        first = jnp.min(jnp.where(imp == m, s_idx_f, float(n_slc)), axis=0, keepdims=True)
        pick = s_idx_f == first
        sel_t = jnp.where(pick, 1.0, sel_t)
        imp = jnp.where(pick, LOWEST, imp)
    own_blk = t0 // SLC_LEN
    sel_t = jnp.where((s_idx == own_blk) | (s_idx == own_blk + 1) | (s_idx * SLC_LEN > tq), 0.0, sel_t)
    selneg = ((sel_t.T - 1.0) * BIG).astype(BF16)
    q_aug = jnp.concatenate([jnp.concatenate([selneg] * p_heads, axis=0), qz16], axis=1)

    own0 = pl.multiple_of(WINDOW + t0, qb)
    s_own = _dot_nt(qz16, ks_ref[pl.ds(own0, qb), n_slc:n_slc + LANES])
    s_own = jnp.where((t0 + lax.broadcasted_iota(jnp.int32, (rows, qb), 1)) <= tpos, s_own, NEG)
    m_own = jnp.max(s_own, axis=-1, keepdims=True)
    acc_own = _dot(jnp.exp((s_own - m_own).astype(BF16)), vs_ref[pl.ds(own0, qb), :])

    def chunk_rows(c):
        return pl.ds(pl.multiple_of(WINDOW + c * kc_len, kc_len), kc_len)

    def sel_update(c, carry, scores):
        m, acc = carry
        m_new = jnp.maximum(m, jnp.max(scores, axis=-1, keepdims=True))
        prob = jnp.exp((scores - m_new).astype(BF16))
        return m_new, jnp.exp(m - m_new) * acc + _dot(prob, vs_ref[chunk_rows(c), :])

    def sel_scores(c):
        return _dot_nt(q_aug, ks_ref[chunk_rows(c), :])

    def sel_pair(i, carry):
        s_a = sel_scores(2 * i)
        s_b = sel_scores(2 * i + 1)
        return sel_update(2 * i + 1, sel_update(2 * i, carry, s_a), s_b)

    n_chunks = t0 // kc_len + 1
    carry = lax.fori_loop(0, n_chunks // 2, sel_pair, (m_own, acc_own))
    _, acc_s = lax.cond(n_chunks % 2 == 1,
                        lambda cr: sel_update(n_chunks - 1, cr, sel_scores(n_chunks - 1)),
                        lambda cr: cr, carry)
    o_s = acc_s * (1.0 / acc_s[:, DEN_LANE:DEN_LANE + 1])

    gts = _sigmoid(gate_ref[...])
    outs = []
    for p in range(p_heads):
        r = slice(p * qb, (p + 1) * qb)
        outs.append(gts[:, 3 * p:3 * p + 1] * o_c[r] + gts[:, 3 * p + 1:3 * p + 2] * o_s[r]
                    + gts[:, 3 * p + 2:3 * p + 3] * o_w[r])
    lane_o = lax.broadcasted_iota(jnp.int32, (qb, LANES), 1)
    for t in range(p_heads // 2):
        o_ref[:, t * LANES:(t + 1) * LANES] = jnp.where(lane_o < dk, outs[2 * t], pltpu.roll(outs[2 * t + 1], dk, 1))


def nsa_attention(pf, pkv, kvcmp):
    bsz, t, _ = pf.shape
    g, dk, qb = NSA_GROUPS, NSA_DK, Q_BLOCK
    rows = NSA_HPG * qb
    w_len = qb + WINDOW
    qw = NSA_HPG * dk
    slab = t // SLC_LEN + LANES
    gate_slab = NSA_Q // LANES + 2 * g
    kv_base = g * slab // LANES
    ri = jnp.arange(rows, dtype=jnp.int32)[:, None] % qb
    ki = jnp.arange(w_len, dtype=jnp.int32)[None, :]
    wbias = jnp.where((ki > ri) & (ki <= ri + WINDOW), 0.0, NEG).astype(F32)
    cmp_spec = pl.BlockSpec((None, None) + kvcmp.shape[3:], lambda b, c, j: (b, c, 0, 0))
    kv_spec = lambda first: pl.BlockSpec((None, t + WINDOW, LANES), lambda b, c, j: (b, 0, first + c))
    return pl.pallas_call(
        functools.partial(_nsa_kernel, t_len=t),
        grid=(bsz, g, t // qb),
        in_specs=[
            pl.BlockSpec((None, qb, qw), lambda b, c, j: (b, j, c)),
            pl.BlockSpec((None, qb, LANES), lambda b, c, j: (b, j, gate_slab + c)),
            cmp_spec, cmp_spec,
            pl.BlockSpec((None, t + WINDOW, slab), lambda b, c, j: (b, 0, c)),
            kv_spec(kv_base), kv_spec(kv_base + g), kv_spec(kv_base + 2 * g),
            pl.BlockSpec((rows, w_len), lambda b, c, j: (0, 0)),
        ],
        out_specs=pl.BlockSpec((None, qb, qw), lambda b, c, j: (b, j, c)),
        out_shape=jax.ShapeDtypeStruct((bsz, t, g * qw), F32),
        compiler_params=_params("arbitrary", "arbitrary", "arbitrary"),
        name="nsa_attention",
    )(pf, pf, kvcmp[0], kvcmp[1], pkv, pkv, pkv, pkv, wbias)


def nsa_mixer_core(x, sc, sh, w_in, pe_k, pe_v, ck_w1, ck_w2, cv_w1, cv_w2):
    pf, pkv = nsa_proj(x, sc, sh, w_in)
    kvcmp = nsa_compress(pf, jnp.stack([pe_k, pe_v]), jnp.stack([ck_w1, cv_w1]), jnp.stack([ck_w2, cv_w2]))
    return nsa_attention(pf, pkv, kvcmp)


def _outproj_ln_kernel(o_ref, w_ref, x_ref, gt_ref, g_ref, b_ref, sc_ref, sh_ref, wr_ref, br_ref,
                       x1_ref, hf_ref, rt_ref, idx_ref, cnt_ref, cnt_acc):
    @pl.when((pl.program_id(0) == 0) & (pl.program_id(1) == 0))
    def _():
        cnt_acc[...] = jnp.zeros_like(cnt_acc)

    y = _dot(o_ref[...].astype(BF16), w_ref[...])
    x1 = _layer_norm(ALPHA * x_ref[...] + (1.0 + gt_ref[...]) * y, g_ref[...], b_ref[...])
    x1_ref[...] = x1
    hf = x1 * (1.0 + sc_ref[...]) + sh_ref[...]
    hf_ref[...] = hf

    logits = _dot_hi(hf, wr_ref[...]) + br_ref[...]
    lane = lax.broadcasted_iota(jnp.int32, logits.shape, 1).astype(F32)
    is_grp = lane < N_GROUPS
    gl = jnp.where(is_grp, logits, LOWEST)
    gmax = jnp.max(gl, axis=-1, keepdims=True)
    gi = jnp.min(jnp.where(gl == gmax, lane, float(LANES)), axis=-1, keepdims=True)
    gp = 1.0 / jnp.sum(jnp.where(is_grp, jnp.exp(logits - gmax), 0.0), axis=-1, keepdims=True)
    lo = N_GROUPS + EXP_PER_GROUP * gi
    el = jnp.where((lane >= lo) & (lane < lo + EXP_PER_GROUP), logits, LOWEST)
    m1 = jnp.max(el, axis=-1, keepdims=True)
    i1 = jnp.min(jnp.where(el == m1, lane, float(LANES)), axis=-1, keepdims=True)
    el2 = jnp.where(lane == i1, LOWEST, el)
    m2 = jnp.max(el2, axis=-1, keepdims=True)
    i2 = jnp.min(jnp.where(el2 == m2, lane, float(LANES)), axis=-1, keepdims=True)
    e2 = jnp.exp(m2 - m1)
    g1 = gp / (1.0 + e2)
    g2 = gp * e2 / (1.0 + e2)

    tm = logits.shape[0]
    onehot = jnp.where((lane == i1) | (lane == i2), 1.0, 0.0)
    ri = lax.broadcasted_iota(jnp.int32, (tm, tm), 0)
    ci = lax.broadcasted_iota(jnp.int32, (tm, tm), 1)
    before = jnp.where(ci < ri, 1.0, 0.0).astype(BF16)
    seen = _dot(before, onehot.astype(BF16)) + cnt_acc[...]
    rank1 = jnp.sum(jnp.where(lane == i1, seen, 0.0), axis=-1, keepdims=True)
    rank2 = jnp.sum(jnp.where(lane == i2, seen, 0.0), axis=-1, keepdims=True)
    cnt_acc[...] = cnt_acc[...] + jnp.sum(onehot, axis=0, keepdims=True)
    cnt_ref[...] = cnt_acc[...]

    vals = (i1 - N_GROUPS, i2 - N_GROUPS, g1, g2, rank1, rank2)
    out = jnp.zeros_like(logits)
    for pos, val in enumerate(vals):
        out = jnp.where(lane == float(pos), val, out)
    rt_ref[...] = out
    idx_ref[...] = out.T[0:SUBLANES, :].astype(jnp.int32)


def outproj_ln(o, w_out, x, gt, ln_g, ln_b, sc2, sh2, w_router, b_router, tm=256):
    bsz, t, d = x.shape
    din = o.shape[-1]
    row = lambda: pl.BlockSpec((None, tm, d), lambda b, i: (b, i, 0))
    per_b = lambda: pl.BlockSpec((None, 1, d), lambda b, i: (b, 0, 0))
    const = lambda r, c: pl.BlockSpec((r, c), lambda b, i: (0, 0))
    return pl.pallas_call(
        _outproj_ln_kernel,
        grid=(bsz, t // tm),
        in_specs=[
            pl.BlockSpec((None, tm, din), lambda b, i: (b, i, 0)),
            const(din, d), row(), per_b(), const(1, d), const(1, d), per_b(), per_b(),
            const(d, LANES), const(1, LANES),
        ],
        out_specs=[row(), row(), pl.BlockSpec((None, tm, LANES), lambda b, i: (b, i, 0)),
                   pl.BlockSpec((None, None, SUBLANES, tm), lambda b, i: (b, i, 0, 0)), const(1, LANES)],
        out_shape=[jax.ShapeDtypeStruct((bsz, t, d), F32), jax.ShapeDtypeStruct((bsz, t, d), F32),
                   jax.ShapeDtypeStruct((bsz, t, LANES), F32),
                   jax.ShapeDtypeStruct((bsz, t // tm, SUBLANES, tm), jnp.int32),
                   jax.ShapeDtypeStruct((1, LANES), F32)],
        scratch_shapes=[pltpu.VMEM((1, LANES), F32)],
        compiler_params=_params("arbitrary", "arbitrary"),
        name="outproj_ln",
    )(o, w_out, x, gt, ln_g.reshape(1, d), ln_b.reshape(1, d), sc2, sh2, w_router, b_router)


def _moe_kernel(blk_e_ref, nused_ref, x_ref, wg_ref, wu_ref, wd_ref, o_ref, wg16, wu16, wd16):
    i = pl.program_id(0)
    e = blk_e_ref[i]
    e_prev = blk_e_ref[jnp.maximum(i - 1, 0)]

    @pl.when((i == 0) | (e != e_prev))
    def _():
        wg16[...] = wg_ref[...].astype(BF16)
        wu16[...] = wu_ref[...].astype(BF16)
        wd16[...] = wd_ref[...].astype(BF16)

    @pl.when(i < nused_ref[0])
    def _():
        x = x_ref[...].astype(BF16)
        gate = _dot(x, wg16[...])
        up = _dot(x, wu16[...])
        o_ref[...] = _dot((_silu(gate) * up).astype(BF16), wd16[...])

    @pl.when(i >= nused_ref[0])
    def _():
        o_ref[...] = jnp.zeros_like(o_ref)


def moe_ffn(xg, blk_e, n_used, w_gate, w_up, w_down, layer):
    rows, d = xg.shape
    de = w_gate.shape[-1]
    n_blk = rows // MOE_BLOCK
    grid_spec = pltpu.PrefetchScalarGridSpec(
        num_scalar_prefetch=2,
        grid=(n_blk,),
        in_specs=[
            pl.BlockSpec((MOE_BLOCK, d), lambda i, be, nu: (jnp.minimum(i, nu[0] - 1), 0)),
            pl.BlockSpec((None, None, d, de), lambda i, be, nu: (layer, be[i], 0, 0)),
            pl.BlockSpec((None, None, d, de), lambda i, be, nu: (layer, be[i], 0, 0)),
            pl.BlockSpec((None, None, de, d), lambda i, be, nu: (layer, be[i], 0, 0)),
        ],
        out_specs=pl.BlockSpec((MOE_BLOCK, d), lambda i, be, nu: (i, 0)),
        scratch_shapes=[pltpu.VMEM((d, de), BF16), pltpu.VMEM((d, de), BF16), pltpu.VMEM((de, d), BF16)],
    )
    return pl.pallas_call(
        _moe_kernel,
        grid_spec=grid_spec,
        out_shape=jax.ShapeDtypeStruct((rows, d), F32),
        compiler_params=_params("arbitrary"),
        name="moe_ffn",
    )(blk_e, n_used, xg, w_gate, w_up, w_down)


def _dispatch_kernel(ps_ref, fill_ref, idx_ref, x_ref, xg_out, dest_ref, zbuf, sem, *, tile, n_exp):
    def row_copy(r, d):
        return pltpu.make_async_copy(x_ref.at[pl.ds(r, 1)], xg_out.at[pl.ds(d, 1)], sem)

    @pl.when(pl.program_id(0) == 0)
    def _():
        zbuf[...] = jnp.zeros_like(zbuf)
        pieces = [SUBLANES << b for b in range((MOE_BLOCK // SUBLANES).bit_length() - 1)]

        def fill_copies(e, wait):
            start = fill_ref[e]
            head = (-start) & (SUBLANES - 1)
            rest = fill_ref[n_exp + e] - head
            for j in range(SUBLANES - 1):
                @pl.when(j < head)
                def _(j=j):
                    cp = pltpu.make_async_copy(zbuf.at[pl.ds(0, 1)], xg_out.at[pl.ds(start + j, 1)], sem)
                    cp.wait() if wait else cp.start()
            cur = start + head
            for size in pieces:
                @pl.when((rest & size) != 0)
                def _(cur=cur, size=size):
                    dst = xg_out.at[pl.ds(pl.multiple_of(cur, SUBLANES), size)]
                    cp = pltpu.make_async_copy(zbuf.at[pl.ds(0, size)], dst, sem)
                    cp.wait() if wait else cp.start()
                cur = cur + (rest & size)

        for e in range(n_exp):
            fill_copies(e, wait=False)
        for e in range(n_exp):
            fill_copies(e, wait=True)

        n_rows = xg_out.shape[0]
        half = zbuf.shape[0]

        def tail_copy(j, wait):
            dst = xg_out.at[pl.ds(pl.multiple_of(j * half, half), half)]
            cp = pltpu.make_async_copy(zbuf, dst, sem)
            cp.wait() if wait else cp.start()

        first_tail = fill_ref[2 * n_exp]
        lax.fori_loop(first_tail, n_rows // half, lambda j, c: (tail_copy(j, False), c)[1], 0)
        lax.fori_loop(first_tail, n_rows // half, lambda j, c: (tail_copy(j, True), c)[1], 0)

    copies = []
    for r in range(tile):
        for k in range(TOP_K):
            d = ps_ref[idx_ref[k, r]] + idx_ref[2 * TOP_K + k, r]
            dest_ref[k, r] = d
            copies.append(row_copy(r, d))
            copies[-1].start()
    for cp in copies:
        cp.wait()


def moe_dispatch(hf, idx, pad_start, fill, rows):
    n, d = hf.shape
    n_tiles, _, tile = idx.shape
    grid_spec = pltpu.PrefetchScalarGridSpec(
        num_scalar_prefetch=2,
        grid=(n_tiles,),
        in_specs=[
            pl.BlockSpec((None, SUBLANES, tile), lambda i, ps, fl: (i, 0, 0), memory_space=pltpu.SMEM),
            pl.BlockSpec((tile, d), lambda i, ps, fl: (i, 0)),
        ],
        out_specs=[
            pl.BlockSpec(memory_space=pl.ANY),
            pl.BlockSpec((None, TOP_K, tile), lambda i, ps, fl: (i, 0, 0), memory_space=pltpu.SMEM),
        ],
        scratch_shapes=[pltpu.VMEM((MOE_BLOCK // 2, d), hf.dtype), pltpu.SemaphoreType.DMA(())],
    )
    return pl.pallas_call(
        functools.partial(_dispatch_kernel, tile=tile, n_exp=fill.shape[0] // 2),
        grid_spec=grid_spec,
        out_shape=[jax.ShapeDtypeStruct((rows, d), hf.dtype),
                   jax.ShapeDtypeStruct((n_tiles, TOP_K, tile), jnp.int32)],
        compiler_params=_params("arbitrary"),
        name="moe_dispatch",
    )(pad_start, fill, idx, hf)


def _combine_ln_kernel(dest_ref, dest_next_ref, x_ref, rt_ref, gt_ref, g_ref, b_ref, yb_ref, o_ref, buf, sems,
                       *, tile):
    i = pl.program_id(0)
    slot = i % 2

    def row_copies(dests, s):
        return [pltpu.make_async_copy(yb_ref.at[pl.ds(dests[k, r], 1)], buf.at[s, k, pl.ds(r, 1)], sems.at[s])
                for r in range(tile) for k in range(TOP_K)]

    @pl.when(i == 0)
    def _():
        for cp in row_copies(dest_ref, slot):
            cp.start()

    @pl.when(i + 1 < pl.num_programs(0))
    def _():
        for cp in row_copies(dest_next_ref, 1 - slot):
            cp.start()

    for r in range(tile):
        for k in range(TOP_K):
            pltpu.make_async_copy(yb_ref.at[pl.ds(0, 1)], buf.at[slot, k, pl.ds(r, 1)], sems.at[slot]).wait()
    rt = rt_ref[...]
    y = rt[:, TOP_K:TOP_K + 1] * buf[slot, 0]
    for k in range(1, TOP_K):
        y = y + rt[:, TOP_K + k:TOP_K + k + 1] * buf[slot, k]
    o_ref[...] = _layer_norm(ALPHA * x_ref[...] + (1.0 + gt_ref[...]) * y, g_ref[...], b_ref[...])


def moe_combine_ln(dest, yb, x1, route, gt, ln_g, ln_b):
    bsz, t, d = x1.shape
    n = bsz * t
    n_tiles, _, tile = dest.shape
    per_b = t // tile
    row = lambda w: pl.BlockSpec((tile, w), lambda i: (i, 0))
    out = pl.pallas_call(
        functools.partial(_combine_ln_kernel, tile=tile),
        grid=(n_tiles,),
        in_specs=[
            pl.BlockSpec((None, TOP_K, tile), lambda i: (i, 0, 0), memory_space=pltpu.SMEM),
            pl.BlockSpec((None, TOP_K, tile), lambda i: (jnp.minimum(i + 1, n_tiles - 1), 0, 0),
                         memory_space=pltpu.SMEM),
            row(d), row(LANES),
            pl.BlockSpec((None, 1, d), lambda i: (i // per_b, 0, 0)),
            pl.BlockSpec((1, d), lambda i: (0, 0)), pl.BlockSpec((1, d), lambda i: (0, 0)),
            pl.BlockSpec(memory_space=pl.ANY),
        ],
        out_specs=row(d),
        out_shape=jax.ShapeDtypeStruct((n, d), F32),
        scratch_shapes=[pltpu.VMEM((2, TOP_K, tile, d), F32), pltpu.SemaphoreType.DMA((2,))],
        compiler_params=_params("arbitrary"),
        name="moe_combine_ln",
    )(dest, dest, x1.reshape(n, d), route.reshape(n, LANES), gt, ln_g.reshape(1, d), ln_b.reshape(1, d), yb)
    return out.reshape(bsz, t, d)


def hier_moe_ln(x1, hf, route, idx, counts, gt, ln_g, ln_b, w_gate, w_up, w_down, layer):
    bsz, t, d = hf.shape
    n = bsz * t
    n_exp = w_gate.shape[1]
    counts = counts[0, N_GROUPS:N_GROUPS + n_exp].astype(jnp.int32)
    padded = (counts + MOE_BLOCK - 1) // MOE_BLOCK * MOE_BLOCK
    pad_end = jnp.cumsum(padded)
    pad_start = (pad_end - padded).astype(jnp.int32)
    n_blk = -(-(n * TOP_K) // MOE_BLOCK) + n_exp
    blk_first = jnp.arange(n_blk, dtype=jnp.int32) * MOE_BLOCK
    blk_e = jnp.minimum(jnp.sum((pad_end[None, :] <= blk_first[:, None]).astype(jnp.int32), axis=1), n_exp - 1)
    n_used = (pad_end[-1:] // MOE_BLOCK).astype(jnp.int32)
    idx = idx.reshape((-1,) + idx.shape[2:])
    first_tail_half = pad_end[-1:] // (MOE_BLOCK // 2)
    fill = jnp.concatenate([pad_start + counts, padded - counts, first_tail_half]).astype(jnp.int32)
    xg, dest = moe_dispatch(hf.reshape(n, d), idx, pad_start, fill, n_blk * MOE_BLOCK)
    yb = moe_ffn(xg, blk_e.astype(jnp.int32), n_used, w_gate, w_up, w_down, layer)
    return moe_combine_ln(dest, yb, x1, route, gt, ln_g, ln_b)


def _pad_cols(w, mult=LANES):
    pad = (-w.shape[-1]) % mult
    return jnp.pad(w, ((0, 0), (0, pad))) if pad else w


def kernel(x, c, ada_w, ada_b, ln1_g, ln1_b, ln2_g, ln2_b, gdn_w_in, gdn_conv_w, gdn_a_log, gdn_dt_bias, gdn_norm_g, gdn_w_out, lru_w_in, lru_conv_w, lru_conv_b, lru_w_a, lru_b_a, lru_w_x, lru_b_x, lru_lambda, lru_w_out, nsa_w_in, nsa_pe_k, nsa_pe_v, nsa_ck_w1, nsa_ck_w2, nsa_cv_w1, nsa_cv_w2, nsa_w_out, moe_w_grp, moe_b_grp, moe_w_exp, moe_b_exp, moe_w_gate, moe_w_up, moe_w_down):
    bsz, t, d = x.shape
    depth = ada_w.shape[0]
    mod = ada_modulation(c, ada_w, ada_b).reshape(depth, bsz, 6, 1, d)
    ja = jb = jc = 0
    for i in range(depth):
        sh1, sc1, gt1, sh2, sc2, gt2 = (mod[i, :, k] for k in range(6))
        kind = i % N_MIXERS
        if kind == 0:
            p = modproj(x, sc1, sh1, _pad_cols(gdn_w_in[ja]).astype(BF16))
            o = gdn_core(p, gdn_conv_w[ja], gdn_a_log[ja], gdn_dt_bias[ja], gdn_norm_g[ja])
            w_out = gdn_w_out[ja]
            ja += 1
        elif kind == 1:
            p = modproj(x, sc1, sh1, lru_w_in[jb].astype(BF16))
            o = lru_core(p, lru_conv_w[jb], lru_conv_b[jb], lru_w_a[jb], lru_b_a[jb], lru_w_x[jb], lru_b_x[jb],
                         lru_lambda[jb])
            w_out = lru_w_out[jb]
            jb += 1
        else:
            o = nsa_mixer_core(x, sc1, sh1, nsa_w_in[jc], nsa_pe_k[jc], nsa_pe_v[jc], nsa_ck_w1[jc], nsa_ck_w2[jc],
                               nsa_cv_w1[jc], nsa_cv_w2[jc])
            w_out = nsa_w_out[jc]
            jc += 1
        w_router = _pad_cols(jnp.concatenate([moe_w_grp[i], moe_w_exp[i]], axis=1))
        b_router = _pad_cols(jnp.concatenate([moe_b_grp[i], moe_b_exp[i]])[None, :])
        x1, hf, route, idx, counts = outproj_ln(o, w_out.astype(BF16), x, gt1, ln1_g[i], ln1_b[i], sc2, sh2,
                                                w_router, b_router)
        x = hier_moe_ln(x1, hf, route, idx, counts, gt2, ln2_g[i], ln2_b[i], moe_w_gate, moe_w_up, moe_w_down, i)
    return x
```
